```python
import jax, jax.numpy as jnp
from jax import lax
import numpy as np

D_MODEL = 1024
BATCH = 2
SEQ = 16384
DEPTH = 2

CHUNK = 64
Q_BLOCK = 128
D_MIX = D_MODEL
N_MIXERS = 4
GROUP_WIDTH = D_MIX // N_MIXERS
HEAD_DIM = 64
N_HEADS = GROUP_WIDTH // HEAD_DIM
SSD_STATE = 128
CONV_WIDTH = 4
SSD_XBC = GROUP_WIDTH + 2 * SSD_STATE
D_PROJ = 12 * GROUP_WIDTH + SSD_XBC + 2 * N_HEADS
D_FF = ((8 * D_MODEL // 3 + 255) // 256) * 256
PLE_DIM = 256
RMS_EPS = 1e-6
ROPE_BASE = 10000.0

kernel_name = "hybrid_fox_retnet_ssd_hgrn2_macaron"


def rmsnorm(x, w):
    xf = x.astype(jnp.float32)
    y = xf * lax.rsqrt(jnp.mean(xf * xf, axis=-1, keepdims=True) + RMS_EPS)
    return (y * w.astype(jnp.float32)).astype(x.dtype)


def head_rmsnorm(t, w):
    B, S = t.shape[0], t.shape[1]
    return rmsnorm(t, w.reshape(N_HEADS, HEAD_DIM)).reshape(B, S, -1)


def swiglu(h, w_up, w_down):
    gate, up = jnp.split(h @ w_up, 2, axis=-1)
    return (jax.nn.silu(gate) * up) @ w_down


def rotary(t, pos):
    half = t.shape[-1] // 2
    inv_freq = ROPE_BASE ** (-jnp.arange(half, dtype=jnp.float32) / half)
    ang = pos[:, None] * inv_freq[None, :]
    cos, sin = jnp.cos(ang), jnp.sin(ang)
    t1, t2 = t[..., :half].astype(jnp.float32), t[..., half:].astype(jnp.float32)
    return jnp.concatenate([t1 * cos - t2 * sin, t1 * sin + t2 * cos], axis=-1)


def causal_depthwise_conv(x, w, b):
    K, C = w.shape
    y = lax.conv_general_dilated(x, w[:, None, :], window_strides=(1,), padding=[(K - 1, 0)],
                                 dimension_numbers=("NWC", "WIO", "NWC"), feature_group_count=C)
    return y + b


def _to_chunks(t):
    B, H, S, d = t.shape
    return jnp.moveaxis(t.reshape(B, H, S // CHUNK, CHUNK, d), 2, 0)


def chunked_linear_recurrence(q, k, v, log_decay):
    out_dtype = v.dtype
    B, H, S, dk = q.shape
    dv = v.shape[-1]
    scalar_decay = log_decay.shape[-1] == 1
    qc, kc, vc, gc = (_to_chunks(t.astype(jnp.float32)) for t in (q, k, v, log_decay))
    bc = jnp.cumsum(gc, axis=-2)
    causal = jnp.tril(jnp.ones((CHUNK, CHUNK), dtype=bool))

    def step(state, blk):
        qi, ki, vi, bi = blk
        b_last = bi[..., -1:, :]
        if scalar_decay:
            seg = bi[..., :, None, 0] - bi[..., None, :, 0]
            decay = jnp.exp(jnp.where(causal, seg, -jnp.inf))
            scores = jnp.einsum("bhtd,bhsd->bhts", qi, ki) * decay
        else:
            seg = bi[..., :, None, :] - bi[..., None, :, :]
            decay = jnp.exp(jnp.where(causal[:, :, None], seg, -jnp.inf))
            scores = jnp.einsum("bhtd,bhsd,bhtsd->bhts", qi, ki, decay)
        out = (jnp.einsum("bhts,bhse->bhte", scores, vi)
               + jnp.einsum("bhtd,bhde->bhte", qi * jnp.exp(bi), state))
        new_state = (jnp.exp(b_last[..., 0, :])[..., :, None] * state
                     + jnp.einsum("bhsd,bhse->bhde", ki * jnp.exp(b_last - bi), vi))
        return new_state, out

    state0 = jnp.zeros((B, H, dk, dv), jnp.float32)
    _, out = lax.scan(step, state0, (qc, kc, vc, bc))
    return jnp.moveaxis(out, 0, 2).reshape(B, H, S, dv).astype(out_dtype)


def forgetting_attention(q, k, v, log_f):
    out_dtype = v.dtype
    B, H, S, d = q.shape
    q, k, v = (t.astype(jnp.float32) for t in (q, k, v))
    c = jnp.cumsum(log_f.astype(jnp.float32), axis=-1)
    nq = S // Q_BLOCK
    q_blocks = jnp.moveaxis(q.reshape(B, H, nq, Q_BLOCK, d), 2, 0)
    c_blocks = jnp.moveaxis(c.reshape(B, H, nq, Q_BLOCK), 2, 0)
    key_pos = jnp.arange(S)
    scale = HEAD_DIM ** -0.5

    def block(args):
        qi, ci, start = args
        s = jnp.einsum("bhqd,bhkd->bhqk", qi, k) * scale + ci[..., :, None] - c[..., None, :]
        q_pos = start + jnp.arange(Q_BLOCK)
        s = jnp.where(q_pos[:, None] >= key_pos[None, :], s, -jnp.inf)
        return jnp.einsum("bhqk,bhkd->bhqd", jax.nn.softmax(s, axis=-1), v)

    out = lax.map(block, (q_blocks, c_blocks, jnp.arange(nq) * Q_BLOCK))
    return jnp.moveaxis(out, 0, 2).reshape(B, H, S, d).astype(out_dtype)


def token_mixer(u, lb, w_in, fox_f_bias, ret_norm, conv_w, conv_b, dt_bias, a_log, ssd_d,
                ssd_norm, hgrn_norm, w_out):
    B, S, _ = u.shape
    proj = u @ w_in
    sizes = ((GROUP_WIDTH,) * 3 + (N_HEADS,) + (GROUP_WIDTH,) * 4
             + (GROUP_WIDTH, SSD_XBC, N_HEADS) + (GROUP_WIDTH,) * 4)
    splits = np.cumsum(sizes)[:-1].tolist()
    (fq, fk, fv, ff, rq, rk, rv, rg, sz, sxbc, sdt, hq, hf, hi, hg) = jnp.split(proj, splits, axis=-1)

    def heads(t):
        return t.reshape(B, S, N_HEADS, -1).transpose(0, 2, 1, 3)

    def merge(t):
        return t.transpose(0, 2, 1, 3).reshape(B, S, -1)

    log_f = jax.nn.log_sigmoid((ff + fox_f_bias).astype(jnp.float32)).transpose(0, 2, 1)
    y_fox = merge(forgetting_attention(heads(fq), heads(fk), heads(fv), log_f))

    pos = jnp.arange(S, dtype=jnp.float32)
    rq_h = rotary(heads(rq), pos)
    rk_h = rotary(heads(rk), pos) * (HEAD_DIM ** -0.5)
    log_gamma = jnp.log1p(-jnp.exp2(-5.0 - jnp.arange(N_HEADS, dtype=jnp.float32)))
    ret_decay = jnp.broadcast_to(log_gamma[None, :, None, None], (B, N_HEADS, S, 1))
    o_ret = chunked_linear_recurrence(rq_h, rk_h, heads(rv), ret_decay)
    y_ret = head_rmsnorm(o_ret.transpose(0, 2, 1, 3), ret_norm) * jax.nn.silu(rg)

    xbc = jax.nn.silu(causal_depthwise_conv(sxbc, conv_w, conv_b))
    xs, bm, cm = jnp.split(xbc, [GROUP_WIDTH, GROUP_WIDTH + SSD_STATE], axis=-1)
    dt = jax.nn.softplus((sdt + dt_bias).astype(jnp.float32))
    a = -jnp.exp(a_log.astype(jnp.float32))
    dt_h = dt.transpose(0, 2, 1)[..., None]
    xs_h = heads(xs)
    q_ssd = jnp.broadcast_to(cm[:, None], (B, N_HEADS, S, SSD_STATE))
    k_ssd = jnp.broadcast_to(bm[:, None], (B, N_HEADS, S, SSD_STATE))
    o_ssd = (chunked_linear_recurrence(q_ssd, k_ssd, xs_h * dt_h, dt_h * a[:, None, None])
             + ssd_d[:, None, None] * xs_h)
    y_ssd = rmsnorm(merge(o_ssd) * jax.nn.silu(sz), ssd_norm)

    lb = lb.astype(jnp.float32)
    log_f_h = jnp.logaddexp(jnp.log(jnp.maximum(lb, 0.0)),
                            jnp.log1p(-lb) + jax.nn.log_sigmoid(hf.astype(jnp.float32)))
    k_h = -jnp.expm1(log_f_h)
    o_hg = chunked_linear_recurrence(heads(hq), heads(k_h), heads(hi), heads(log_f_h))
    y_hg = head_rmsnorm(o_hg.transpose(0, 2, 1, 3), hgrn_norm) * jax.nn.silu(hg)

    y = jnp.concatenate([y_fox, y_ret, y_ssd, y_hg], axis=-1)
    return (y @ w_out).astype(u.dtype)


def setup_inputs(seed: int = 0) -> dict:
    key = jax.random.key(seed)
    ks = jax.random.split(key, 26)
    f32 = jnp.float32

    def nrm(k, shape, scale):
        return jax.random.normal(k, shape, f32) * scale

    def gain(k, shape):
        return 1.0 + 0.02 * jax.random.normal(k, shape, f32)

    dt_init = jnp.exp(jax.random.uniform(ks[11], (DEPTH, N_HEADS), f32,
                                         np.log(1e-3), np.log(1e-1)))
    return {
        "x": nrm(ks[0], (BATCH, SEQ, D_MODEL), 1.0),
        "p": nrm(ks[1], (DEPTH, BATCH, SEQ, PLE_DIM), 1.0),
        "ffn1_norm": gain(ks[2], (DEPTH, D_MODEL)),
        "ffn1_w_up": nrm(ks[3], (DEPTH, D_MODEL, 2 * D_FF), D_MODEL ** -0.5),
        "ffn1_w_down": nrm(ks[4], (DEPTH, D_FF, D_MODEL), D_FF ** -0.5),
        "mix_norm": gain(ks[5], (DEPTH, D_MODEL)),
        "w_in": nrm(ks[6], (DEPTH, D_MODEL, D_PROJ), D_MODEL ** -0.5),
        "fox_f_bias": 2.0 + nrm(ks[7], (DEPTH, N_HEADS), 0.5),
        "ret_norm": gain(ks[8], (DEPTH, GROUP_WIDTH)),
        "conv_w": nrm(ks[9], (DEPTH, CONV_WIDTH, SSD_XBC), CONV_WIDTH ** -0.5),
        "conv_b": nrm(ks[10], (DEPTH, SSD_XBC), 0.02),
        "dt_bias": dt_init + jnp.log(-jnp.expm1(-dt_init)),
        "a_log": jnp.log(jax.random.uniform(ks[12], (DEPTH, N_HEADS), f32, 1.0, 16.0)),
        "ssd_d": 1.0 + nrm(ks[13], (DEPTH, N_HEADS), 0.1),
        "ssd_norm": gain(ks[14], (DEPTH, GROUP_WIDTH)),
        "hgrn_lower_bounds": 1.0 + nrm(ks[15], (DEPTH, GROUP_WIDTH), 0.1),
        "hgrn_norm": gain(ks[16], (DEPTH, GROUP_WIDTH)),
        "w_out": nrm(ks[17], (DEPTH, D_MIX, D_MODEL), D_MIX ** -0.5),
        "ffn2_norm": gain(ks[18], (DEPTH, D_MODEL)),
        "ffn2_w_up": nrm(ks[19], (DEPTH, D_MODEL, 2 * D_FF), D_MODEL ** -0.5),
        "ffn2_w_down": nrm(ks[20], (DEPTH, D_FF, D_MODEL), D_FF ** -0.5),
        "ple_norm": gain(ks[21], (DEPTH, D_MODEL)),
        "ple_w_gate": nrm(ks[22], (DEPTH, D_MODEL, D_MODEL), D_MODEL ** -0.5),
        "ple_w_proj": nrm(ks[23], (DEPTH, PLE_DIM, D_MODEL), PLE_DIM ** -0.5),
        "final_norm": gain(ks[24], (D_MODEL,)),
    }


def reference(x, p, ffn1_norm, ffn1_w_up, ffn1_w_down, mix_norm, w_in, fox_f_bias, ret_norm,
              conv_w, conv_b, dt_bias, a_log, ssd_d, ssd_norm, hgrn_lower_bounds, hgrn_norm,
              w_out, ffn2_norm, ffn2_w_up, ffn2_w_down, ple_norm, ple_w_gate, ple_w_proj,
              final_norm):
    lbs = jnp.cumsum(jax.nn.softmax(hgrn_lower_bounds.astype(jnp.float32), axis=0), axis=0)
    lbs = lbs - lbs[0]
    h = x
    for i in range(DEPTH):
        h = h + 0.5 * swiglu(rmsnorm(h, ffn1_norm[i]), ffn1_w_up[i], ffn1_w_down[i])
        h = h + token_mixer(rmsnorm(h, mix_norm[i]), lbs[i], w_in[i], fox_f_bias[i], ret_norm[i],
                            conv_w[i], conv_b[i], dt_bias[i], a_log[i], ssd_d[i], ssd_norm[i],
                            hgrn_norm[i], w_out[i])
        h = h + 0.5 * swiglu(rmsnorm(h, ffn2_norm[i]), ffn2_w_up[i], ffn2_w_down[i])
        gate = jax.nn.sigmoid(rmsnorm(h, ple_norm[i]) @ ple_w_gate[i])
        h = h + gate * (p[i] @ ple_w_proj[i])
    return rmsnorm(h, final_norm)
```

```python
import functools
import math

import jax
import jax.numpy as jnp
import numpy as np
from jax import lax
from jax.experimental import pallas as pl
from jax.experimental.pallas import tpu as pltpu

F32 = jnp.float32
BF16 = jnp.bfloat16

HEAD_DIM = 64
N_HEADS = 4
GROUP_WIDTH = N_HEADS * HEAD_DIM
SSD_STATE = 128
CONV_WIDTH = 4
RMS_EPS = 1e-6
ROPE_BASE = 10000.0
LOG2E = 1.4426950408889634

V7X_LANES = 128
V7X_SUBLANES = 8
V7X_VMEM_BYTES = 64 * 1024 * 1024
MIB = 1024 * 1024

TM_FFN = 512
FF_CHUNK = 256
TQ_FOX = 256
C_RET = 128
C_SSD = 128
C_HGRN = 64
TL_GATES = 2048


def _cparams(semantics, vmem_mib):
    return pltpu.CompilerParams(dimension_semantics=semantics, vmem_limit_bytes=int(vmem_mib * MIB))


def _rmsnorm(x, w):
    return x * lax.rsqrt(jnp.mean(x * x, axis=-1, keepdims=True) + RMS_EPS) * w


def _silu(x):
    return x * jax.nn.sigmoid(x)


def _resident(shape):
    return pl.BlockSpec(shape, lambda *_: (0,) * len(shape), pipeline_mode=pl.Buffered(1))


def _ffn_body(h_ref, nw_ref, wup_ref, wdn_ref, o_ref, act_ref, *, d_ff):
    h = h_ref[...]
    x = _rmsnorm(h, nw_ref[...]).astype(BF16)
    for c in range(d_ff // FF_CHUNK):
        lo = c * FF_CHUNK
        gate = jnp.dot(x, wup_ref[:, lo:lo + FF_CHUNK], preferred_element_type=F32)
        up = jnp.dot(x, wup_ref[:, d_ff + lo:d_ff + lo + FF_CHUNK], preferred_element_type=F32)
        act_ref[:, lo:lo + FF_CHUNK] = (_silu(gate) * up).astype(BF16)
    o_ref[...] = h + 0.5 * jnp.dot(act_ref[...], wdn_ref[...], preferred_element_type=F32)


def _ffn(h, norm_w, w_up, w_down):
    n, d = h.shape
    d_ff = w_down.shape[0]
    return pl.pallas_call(
        functools.partial(_ffn_body, d_ff=d_ff),
        grid=(n // TM_FFN,),
        in_specs=[
            pl.BlockSpec((TM_FFN, d), lambda i: (i, 0)),
            _resident((1, d)),
            _resident((d, 2 * d_ff)),
            _resident((d_ff, d)),
        ],
        out_specs=pl.BlockSpec((TM_FFN, d), lambda i: (i, 0)),
        out_shape=jax.ShapeDtypeStruct((n, d), F32),
        scratch_shapes=[pltpu.VMEM((TM_FFN, d_ff), BF16)],
        compiler_params=_cparams(("parallel",), 48),
        name="ffn",
    )(h, norm_w.reshape(1, d), w_up, w_down)


def _inproj_body(h_ref, nw_ref, wb_ref, wf_ref, wst_ref, pb_ref, pf_ref, st_ref, *, q_scale):
    x = _rmsnorm(h_ref[...], nw_ref[...]).astype(BF16)
    gw = GROUP_WIDTH
    for c in range(wb_ref.shape[1] // gw):
        acc = jnp.dot(x, wb_ref[:, c * gw:(c + 1) * gw], preferred_element_type=F32)
        if c == 0:
            acc = acc * q_scale
        pb_ref[:, c * gw:(c + 1) * gw] = acc.astype(BF16)
    for c in range(wf_ref.shape[1] // gw):
        pf_ref[:, c * gw:(c + 1) * gw] = jnp.dot(x, wf_ref[:, c * gw:(c + 1) * gw],
                                                preferred_element_type=F32)
    st_ref[...] = lax.dot_general(wst_ref[...], x, (((1,), (1,)), ((), ())),
                                  preferred_element_type=F32)


def _inproj(h, norm_w, wb, wf, wst):
    n, d = h.shape
    nb, nf = wb.shape[1], wf.shape[1]
    return pl.pallas_call(
        functools.partial(_inproj_body, q_scale=HEAD_DIM ** -0.5 * LOG2E),
        grid=(n // TM_FFN,),
        in_specs=[
            pl.BlockSpec((TM_FFN, d), lambda i: (i, 0)),
            _resident((1, d)),
            _resident((d, nb)),
            _resident((d, nf)),
            _resident((2 * N_HEADS, d)),
        ],
        out_specs=[
            pl.BlockSpec((TM_FFN, nb), lambda i: (i, 0)),
            pl.BlockSpec((TM_FFN, nf), lambda i: (i, 0)),
            pl.BlockSpec((2 * N_HEADS, TM_FFN), lambda i: (0, i)),
        ],
        out_shape=[
            jax.ShapeDtypeStruct((n, nb), BF16),
            jax.ShapeDtypeStruct((n, nf), F32),
            jax.ShapeDtypeStruct((2 * N_HEADS, n), F32),
        ],
        compiler_params=_cparams(("parallel",), 48),
        name="inproj",
    )(h, norm_w.reshape(1, d), wb, wf, wst)


def _gates_body(st_ref, bias_ref, alog_ref, g_ref, dt_ref, carry_ref):
    @pl.when(pl.program_id(1) == 0)
    def _():
        carry_ref[...] = jnp.zeros_like(carry_ref)

    x = st_ref[...] + bias_ref[...]
    tl = x.shape[1]
    row = lax.broadcasted_iota(jnp.int32, x.shape, 0)
    lane = lax.broadcasted_iota(jnp.int32, x.shape, 1)
    lane_in_chunk = lane % C_SSD
    is_fox = row < N_HEADS
    dt = jax.nn.softplus(x)
    val = jnp.where(is_fox, jax.nn.log_sigmoid(x), dt * (-jnp.exp(alog_ref[...])))
    full, local = val, val
    shift = 1
    while shift < tl:
        full = full + jnp.where(lane >= shift, pltpu.roll(full, shift, 1), 0.0)
        if shift < C_SSD:
            local = local + jnp.where(lane_in_chunk >= shift, pltpu.roll(local, shift, 1), 0.0)
        shift *= 2
    cum = full + carry_ref[...]
    carry_ref[...] = jnp.sum(jnp.where(lane == tl - 1, cum, 0.0), axis=1, keepdims=True)
    g_ref[...] = jnp.where(is_fox, cum * LOG2E, local)
    dt_ref[...] = dt


def _gates(st, bias, alog, batch):
    rows, n = st.shape
    s = n // batch
    nt = s // TL_GATES
    spec = pl.BlockSpec((rows, TL_GATES), lambda b, t: (0, b * nt + t))
    col = pl.BlockSpec((rows, 1), lambda b, t: (0, 0))
    return pl.pallas_call(
        _gates_body,
        grid=(batch, nt),
        in_specs=[spec, col, col],
        out_specs=[spec, spec],
        out_shape=[jax.ShapeDtypeStruct((rows, n), F32)] * 2,
        scratch_shapes=[pltpu.VMEM((rows, 1), F32)],
        compiler_params=_cparams(("parallel", "arbitrary"), 16),
        name="gates",
    )(st, bias, alog)


def _rope_body(inv_ref, cos_ref, sin_ref):
    rows = cos_ref.shape[0]
    pos = (lax.broadcasted_iota(jnp.int32, cos_ref.shape, 0) + pl.program_id(0) * rows).astype(F32)
    lane = lax.broadcasted_iota(jnp.int32, cos_ref.shape, 1)
    ang = pos * inv_ref[...]
    cos_ref[...] = jnp.cos(ang)
    sin = jnp.sin(ang)
    sin_ref[...] = jnp.where(lane % HEAD_DIM < HEAD_DIM // 2, -sin, sin)


def _rope_tables(s):
    half = HEAD_DIM // 2
    inv_freq = ROPE_BASE ** (-jnp.arange(half, dtype=F32) / half)
    inv = jnp.tile(inv_freq, V7X_LANES // half).reshape(1, V7X_LANES)
    rows = 1024
    return pl.pallas_call(
        _rope_body,
        grid=(s // rows,),
        in_specs=[pl.BlockSpec((1, V7X_LANES), lambda i: (0, 0))],
        out_specs=[pl.BlockSpec((rows, V7X_LANES), lambda i: (i, 0))] * 2,
        out_shape=[jax.ShapeDtypeStruct((s, V7X_LANES), F32)] * 2,
        compiler_params=_cparams(("parallel",), 16),
        name="rope_tables",
    )(inv)


def _fox_body(q_ref, k_ref, v_ref, c_ref, o_ref, m_ref, l_ref, acc_ref):
    i = pl.program_id(2)
    tq = q_ref.shape[0]
    q = q_ref[...]
    lane = lax.broadcasted_iota(jnp.int32, q.shape, 1)
    heads = [(lane < HEAD_DIM), (lane >= HEAD_DIM)]
    qh = [jnp.where(sel, q, jnp.zeros_like(q)) for sel in heads]
    cref = [c_ref[i, hh:hh + 1, 0:1] for hh in range(2)]
    m_ref[...] = jnp.full(m_ref.shape, -1e30, F32)
    l_ref[...] = jnp.zeros_like(l_ref)
    acc_ref[...] = jnp.zeros_like(acc_ref)

    def block(j, masked):
        start = pl.multiple_of(j * tq, tq)
        k = k_ref[pl.ds(start, tq), :]
        v = v_ref[pl.ds(start, tq), :]
        for hh in range(2):
            s = lax.dot_general(qh[hh], k, (((1,), (1,)), ((), ())), preferred_element_type=F32)
            s = s + (cref[hh] - c_ref[j, hh:hh + 1, :])
            if masked:
                r = lax.broadcasted_iota(jnp.int32, s.shape, 0)
                c = lax.broadcasted_iota(jnp.int32, s.shape, 1)
                s = jnp.where(c <= r, s, -1e30)
            m_old = m_ref[hh]
            m_new = jnp.maximum(m_old, jnp.max(s, axis=1, keepdims=True))
            alpha = jnp.exp2(m_old - m_new)
            p = jnp.exp2(s - m_new)
            l_ref[hh] = alpha * l_ref[hh] + jnp.sum(p, axis=1, keepdims=True)
            acc_ref[hh] = alpha * acc_ref[hh] + jnp.dot(p.astype(BF16), v, preferred_element_type=F32)
            m_ref[hh] = m_new

    def body(j, carry):
        block(j, False)
        return carry

    lax.fori_loop(0, i, body, 0)
    block(i, True)
    out = [acc_ref[hh] / l_ref[hh] for hh in range(2)]
    o_ref[...] = jnp.where(heads[0], out[0], out[1]).astype(o_ref.dtype)


def _fox(pb, c4, batch):
    n = pb.shape[0]
    s = n // batch
    nq = s // TQ_FOX
    w = V7X_LANES
    pairs = GROUP_WIDTH // w
    return pl.pallas_call(
        _fox_body,
        grid=(batch, pairs, nq),
        in_specs=[
            pl.BlockSpec((TQ_FOX, w), lambda b, p, i: (b * nq + i, p)),
            pl.BlockSpec((s, w), lambda b, p, i: (b, pairs + p)),
            pl.BlockSpec((s, w), lambda b, p, i: (b, 2 * pairs + p)),
            pl.BlockSpec((None, None, nq, 2, TQ_FOX), lambda b, p, i: (b, p, 0, 0, 0)),
        ],
        out_specs=pl.BlockSpec((TQ_FOX, w), lambda b, p, i: (b * nq + i, p)),
        out_shape=jax.ShapeDtypeStruct((n, GROUP_WIDTH), BF16),
        scratch_shapes=[
            pltpu.VMEM((2, TQ_FOX, 1), F32),
            pltpu.VMEM((2, TQ_FOX, 1), F32),
            pltpu.VMEM((2, TQ_FOX, w), F32),
        ],
        compiler_params=_cparams(("parallel", "parallel", "arbitrary"), 40),
        name="fox",
    )(pb, pb, pb, c4)


def _head_of_lane(shape, axis):
    return lax.broadcasted_iota(jnp.int32, shape, axis) // HEAD_DIM


def _same_head(shape):
    return _head_of_lane(shape, 0) == _head_of_lane(shape, 1)


def _per_head(values, lane_head):
    out = values[N_HEADS - 1]
    for h in range(N_HEADS - 2, -1, -1):
        out = jnp.where(lane_head == h, values[h], out)
    return out


def _head_rmsnorm(o, w, ones_bd):
    ms = jnp.dot((o * o).astype(BF16), ones_bd, preferred_element_type=F32) * (1.0 / HEAD_DIM)
    return o * lax.rsqrt(ms + RMS_EPS) * w


def _ret_body(rq_ref, rk_ref, rv_ref, rg_ref, cos_ref, sin_ref, nw_ref, o_ref,
              state_ref, dmat_ref, tab_ref, bd_ref):
    c = rq_ref.shape[0]
    gw = GROUP_WIDTH
    log_gamma = [math.log1p(-(2.0 ** (-5.0 - h))) for h in range(N_HEADS)]

    @pl.when(pl.program_id(1) == 0)
    def _():
        state_ref[...] = jnp.zeros_like(state_ref)
        r = lax.broadcasted_iota(jnp.int32, (c, c), 0)
        s = lax.broadcasted_iota(jnp.int32, (c, c), 1)
        dist = (r - s).astype(F32)
        for h in range(N_HEADS):
            dmat_ref[h] = jnp.where(s <= r, jnp.exp(dist * log_gamma[h]), 0.0)
        lg = _per_head(log_gamma, _head_of_lane((c, gw), 1))
        t = lax.broadcasted_iota(jnp.int32, (c, gw), 0).astype(F32)
        tab_ref[0] = jnp.exp((t + 1.0) * lg)
        tab_ref[1] = jnp.exp((c - 1.0 - t) * lg)
        tab_ref[2] = jnp.exp(c * lg)
        bd_ref[...] = jnp.where(_same_head((gw, gw)), 1.0, 0.0).astype(BF16)

    lane = lax.broadcasted_iota(jnp.int32, (c, gw), 1)
    lane_head = lane // HEAD_DIM
    first_half = lane % HEAD_DIM < HEAD_DIM // 2
    cosx = jnp.concatenate([cos_ref[...]] * (gw // V7X_LANES), axis=1)
    sinx = jnp.concatenate([sin_ref[...]] * (gw // V7X_LANES), axis=1)

    def rotary(x):
        swapped = jnp.where(first_half, pltpu.roll(x, gw - HEAD_DIM // 2, 1),
                            pltpu.roll(x, HEAD_DIM // 2, 1))
        return x * cosx + swapped * sinx

    q = rotary(rq_ref[...])
    k = rotary(rk_ref[...]) * (HEAD_DIM ** -0.5)
    v = rv_ref[...]
    qb, kb = q.astype(BF16), k.astype(BF16)
    zero = jnp.zeros_like(qb)
    o = jnp.zeros((c, gw), F32)
    for h in range(N_HEADS):
        sel = lane_head == h
        sc = lax.dot_general(jnp.where(sel, qb, zero), kb, (((1,), (1,)), ((), ())),
                             preferred_element_type=F32)
        p = (sc * dmat_ref[h]).astype(BF16)
        o = o + jnp.where(sel, jnp.dot(p, v, preferred_element_type=F32), 0.0)
    st = state_ref[...]
    o = o + lax.dot_general(qb, st.astype(BF16), (((1,), (1,)), ((), ())),
                            preferred_element_type=F32) * tab_ref[0]
    kd = (k * tab_ref[1]).astype(BF16)
    vt = v.astype(F32).T.astype(BF16)
    upd = jnp.dot(vt, kd, preferred_element_type=F32)
    state_ref[...] = st * tab_ref[2, 0:1, :] + jnp.where(_same_head((gw, gw)), upd, 0.0)
    y = _head_rmsnorm(o, nw_ref[...], bd_ref[...]) * _silu(rg_ref[...])
    o_ref[...] = y.astype(o_ref.dtype)


def _retention(pf, pb, cos_t, sin_t, norm_w, batch, cols):
    n = pf.shape[0]
    s = n // batch
    nt = s // C_RET
    gw = GROUP_WIDTH

    def tok(col):
        return pl.BlockSpec((C_RET, gw), lambda b, t: (b * nt + t, col))

    tab = pl.BlockSpec((C_RET, V7X_LANES), lambda b, t: (t, 0))
    return pl.pallas_call(
        _ret_body,
        grid=(batch, nt),
        in_specs=[tok(cols["rq"]), tok(cols["rk"]), tok(cols["rv"]), tok(cols["rg"]), tab, tab,
                  pl.BlockSpec((1, gw), lambda b, t: (0, 0))],
        out_specs=pl.BlockSpec((C_RET, gw), lambda b, t: (b * nt + t, 0)),
        out_shape=jax.ShapeDtypeStruct((n, gw), BF16),
        scratch_shapes=[
            pltpu.VMEM((gw, gw), F32),
            pltpu.VMEM((N_HEADS, C_RET, C_RET), F32),
            pltpu.VMEM((3, C_RET, gw), F32),
            pltpu.VMEM((gw, gw), BF16),
        ],
        compiler_params=_cparams(("parallel", "arbitrary"), 24),
        name="retention",
    )(pf, pf, pb, pf, cos_t, sin_t, norm_w.reshape(1, gw))


def _ssd_body(xbc_ref, z_ref, gcol_ref, grow_ref, cw_ref, cb_ref, d_ref, nw_ref, o_ref,
              xpad_ref, state_ref):
    c = xbc_ref.shape[0]
    gw = GROUP_WIDTH
    pad = V7X_SUBLANES

    @pl.when(pl.program_id(1) == 0)
    def _():
        state_ref[...] = jnp.zeros_like(state_ref)
        xpad_ref[0:pad, :] = jnp.zeros((pad, xpad_ref.shape[1]), F32)

    xpad_ref[pad:pad + c, :] = xbc_ref[...]
    conv = cb_ref[...]
    for j in range(CONV_WIDTH):
        off = pad - (CONV_WIDTH - 1) + j
        conv = conv + cw_ref[j:j + 1, :] * xpad_ref[off:off + c, :]
    xpad_ref[0:pad, :] = xpad_ref[c:c + pad, :]
    xbc = _silu(conv)
    xs = xbc[:, 0:gw]
    bm = xbc[:, gw:gw + SSD_STATE]
    cm = xbc[:, gw + SSD_STATE:gw + 2 * SSD_STATE]

    lane_head = _head_of_lane((c, gw), 1)
    gcol = gcol_ref[...]
    dtx = _per_head([gcol[:, h:h + 1] for h in range(N_HEADS)], lane_head)
    bx = _per_head([gcol[:, N_HEADS + h:N_HEADS + h + 1] for h in range(N_HEADS)], lane_head)
    blast = bx[c - 1:c, :]
    v = xs * dtx
    vb = v.astype(BF16)

    sc = lax.dot_general(cm.astype(BF16), bm.astype(BF16), (((1,), (1,)), ((), ())),
                         preferred_element_type=F32)
    r = lax.broadcasted_iota(jnp.int32, (c, c), 0)
    s = lax.broadcasted_iota(jnp.int32, (c, c), 1)
    o = jnp.zeros((c, gw), F32)
    for h in range(N_HEADS):
        bcol = gcol[:, N_HEADS + h:N_HEADS + h + 1]
        brow = grow_ref[N_HEADS + h:N_HEADS + h + 1, :]
        decay = jnp.where(s <= r, jnp.exp(jnp.minimum(bcol - brow, 0.0)), 0.0)
        p = (sc * decay).astype(BF16)
        o = o + jnp.where(lane_head == h, jnp.dot(p, vb, preferred_element_type=F32), 0.0)
    st = state_ref[...]
    o = o + jnp.dot(cm.astype(BF16), st.astype(BF16), preferred_element_type=F32) * jnp.exp(bx)
    vd = (v * jnp.exp(blast - bx)).astype(BF16)
    state_ref[...] = st * jnp.exp(blast) + jnp.dot(bm.T.astype(BF16), vd, preferred_element_type=F32)
    o = o + d_ref[...] * xs
    y = _rmsnorm(o * _silu(z_ref[...]), nw_ref[...])
    o_ref[...] = y.astype(o_ref.dtype)


def _ssd(pf, gcol, grow, conv_w, conv_b, d_x, norm_w, batch, cols):
    n = pf.shape[0]
    s = n // batch
    nt = s // C_SSD
    gw = GROUP_WIDTH
    xbc_w = gw + 2 * SSD_STATE
    const = lambda shape: pl.BlockSpec(shape, lambda b, t: (0, 0))
    return pl.pallas_call(
        _ssd_body,
        grid=(batch, nt),
        in_specs=[
            pl.BlockSpec((C_SSD, xbc_w), lambda b, t: (b * nt + t, cols["sxbc"])),
            pl.BlockSpec((C_SSD, gw), lambda b, t: (b * nt + t, cols["sz"])),
            pl.BlockSpec((C_SSD, 2 * N_HEADS), lambda b, t: (b * nt + t, 0)),
            pl.BlockSpec((2 * N_HEADS, C_SSD), lambda b, t: (0, b * nt + t)),
            const((CONV_WIDTH, xbc_w)), const((1, xbc_w)), const((1, gw)), const((1, gw)),
        ],
        out_specs=pl.BlockSpec((C_SSD, gw), lambda b, t: (b * nt + t, 0)),
        out_shape=jax.ShapeDtypeStruct((n, gw), BF16),
        scratch_shapes=[
            pltpu.VMEM((C_SSD + 2 * V7X_SUBLANES, xbc_w), F32),
            pltpu.VMEM((SSD_STATE, gw), F32),
        ],
        compiler_params=_cparams(("parallel", "arbitrary"), 24),
        name="ssd",
    )(pf, pf, gcol, grow, conv_w, conv_b.reshape(1, xbc_w), d_x, norm_w.reshape(1, gw))


def _hgrn_body(hq_ref, hf_ref, hi_ref, hg_ref, lbw_ref, nw_ref, o_ref,
               state_ref, kpad_ref, bpad_ref, vpad_ref, bd_ref, *, layer):
    c = hq_ref.shape[0]
    gw = GROUP_WIDTH

    @pl.when(pl.program_id(1) == 0)
    def _():
        state_ref[...] = jnp.zeros_like(state_ref)
        zeros = jnp.zeros((c, gw), F32)
        kpad_ref[0:c, :] = zeros
        bpad_ref[0:c, :] = zeros
        vpad_ref[0:c, :] = zeros
        bd_ref[...] = jnp.where(_same_head((gw, gw)), 1.0, 0.0).astype(BF16)

    lbw = lbw_ref[...]
    e = jnp.exp(lbw - jnp.max(lbw, axis=0, keepdims=True))
    sm = e / jnp.sum(e, axis=0, keepdims=True)
    cs = sm[0:1, :]
    first = cs
    for d in range(1, layer + 1):
        cs = cs + sm[d:d + 1, :]
    lb = cs - first

    log_f = jnp.logaddexp(jnp.log(jnp.maximum(lb, 0.0)),
                          jnp.log1p(-lb) + jax.nn.log_sigmoid(hf_ref[...]))
    k = 1.0 - jnp.exp(log_f)
    b = log_f
    row = lax.broadcasted_iota(jnp.int32, (c, gw), 0)
    shift = 1
    while shift < c:
        b = b + jnp.where(row >= shift, pltpu.roll(b, shift, 0), 0.0)
        shift *= 2
    q = hq_ref[...]
    v = hi_ref[...].astype(F32)
    kpad_ref[c:2 * c, :] = k
    bpad_ref[c:2 * c, :] = b
    vpad_ref[c:2 * c, :] = v
    bd = bd_ref[...]

    o = jnp.zeros((c, gw), F32)
    for d in range(c):
        ks = kpad_ref[c - d:2 * c - d, :]
        bs = bpad_ref[c - d:2 * c - d, :]
        vs = vpad_ref[c - d:2 * c - d, :]
        w = (q * ks * jnp.exp(b - bs)).astype(BF16)
        o = o + jnp.dot(w, bd, preferred_element_type=F32) * vs
    st = state_ref[...]
    o = o + lax.dot_general((q * jnp.exp(b)).astype(BF16), st.astype(BF16),
                            (((1,), (1,)), ((), ())), preferred_element_type=F32)
    blast = b[c - 1:c, :]
    kd = (k * jnp.exp(blast - b)).astype(BF16)
    upd = jnp.dot(v.T.astype(BF16), kd, preferred_element_type=F32)
    state_ref[...] = st * jnp.exp(blast) + jnp.where(_same_head((gw, gw)), upd, 0.0)
    y = _head_rmsnorm(o, nw_ref[...], bd) * _silu(hg_ref[...])
    o_ref[...] = y.astype(o_ref.dtype)


def _hgrn(pf, pb, lower_bounds, norm_w, layer, batch, cols):
    n = pf.shape[0]
    s = n // batch
    nt = s // C_HGRN
    gw = GROUP_WIDTH
    depth = lower_bounds.shape[0]

    def tok(col):
        return pl.BlockSpec((C_HGRN, gw), lambda b, t: (b * nt + t, col))

    return pl.pallas_call(
        functools.partial(_hgrn_body, layer=layer),
        grid=(batch, nt),
        in_specs=[tok(cols["hq"]), tok(cols["hf"]), tok(cols["hi"]), tok(cols["hg"]),
                  pl.BlockSpec((depth, gw), lambda b, t: (0, 0)),
                  pl.BlockSpec((1, gw), lambda b, t: (0, 0))],
        out_specs=pl.BlockSpec((C_HGRN, gw), lambda b, t: (b * nt + t, 0)),
        out_shape=jax.ShapeDtypeStruct((n, gw), BF16),
        scratch_shapes=[
            pltpu.VMEM((gw, gw), F32),
            pltpu.VMEM((2 * C_HGRN, gw), F32),
            pltpu.VMEM((2 * C_HGRN, gw), F32),
            pltpu.VMEM((2 * C_HGRN, gw), F32),
            pltpu.VMEM((gw, gw), BF16),
        ],
        compiler_params=_cparams(("parallel", "arbitrary"), 24),
        name="hgrn2",
    )(pf, pf, pb, pf, lower_bounds, norm_w.reshape(1, gw))


def _outproj_body(h_ref, y0_ref, y1_ref, y2_ref, y3_ref, w_ref, o_ref):
    gw = GROUP_WIDTH
    acc = h_ref[...]
    for m, y_ref in enumerate((y0_ref, y1_ref, y2_ref, y3_ref)):
        acc = acc + jnp.dot(y_ref[...], w_ref[m * gw:(m + 1) * gw, :], preferred_element_type=F32)
    o_ref[...] = acc


def _outproj(h, ys, w_out):
    n, d = h.shape
    gw = GROUP_WIDTH
    tok = pl.BlockSpec((TM_FFN, d), lambda i: (i, 0))
    ytok = pl.BlockSpec((TM_FFN, gw), lambda i: (i, 0))
    return pl.pallas_call(
        _outproj_body,
        grid=(n // TM_FFN,),
        in_specs=[tok, ytok, ytok, ytok, ytok, _resident(w_out.shape)],
        out_specs=tok,
        out_shape=jax.ShapeDtypeStruct((n, d), F32),
        compiler_params=_cparams(("parallel",), 32),
        name="outproj",
    )(h, *ys, w_out)


def _ple_body(h_ref, p_ref, nw_ref, wg_ref, wp_ref, fw_ref, o_ref, *, final):
    h = h_ref[...]
    x = _rmsnorm(h, nw_ref[...]).astype(BF16)
    gate = jax.nn.sigmoid(jnp.dot(x, wg_ref[...], preferred_element_type=F32))
    emb = jnp.dot(p_ref[...].astype(BF16), wp_ref[...], preferred_element_type=F32)
    out = h + gate * emb
    if final:
        out = _rmsnorm(out, fw_ref[...])
    o_ref[...] = out


def _ple(h, p, norm_w, w_gate, w_proj, final_w, final):
    n, d = h.shape
    pd = p.shape[1]
    tok = pl.BlockSpec((TM_FFN, d), lambda i: (i, 0))
    return pl.pallas_call(
        functools.partial(_ple_body, final=final),
        grid=(n // TM_FFN,),
        in_specs=[tok, pl.BlockSpec((TM_FFN, pd), lambda i: (i, 0)), _resident((1, d)),
                  _resident((d, d)), _resident((pd, d)), _resident((1, d))],
        out_specs=tok,
        out_shape=jax.ShapeDtypeStruct((n, d), F32),
        compiler_params=_cparams(("parallel",), 32),
        name="ple",
    )(h, p, norm_w.reshape(1, d), w_gate, w_proj, final_w.reshape(1, d))


def _split_w_in(w_in):
    gw = GROUP_WIDTH
    xbc_w = gw + 2 * SSD_STATE
    names = ["fq", "fk", "fv", "ff", "rq", "rk", "rv", "rg", "sz", "sxbc", "sdt", "hq", "hf", "hi", "hg"]
    sizes = [gw, gw, gw, N_HEADS, gw, gw, gw, gw, gw, xbc_w, N_HEADS, gw, gw, gw, gw]
    offs = dict(zip(names, np.concatenate([[0], np.cumsum(sizes)[:-1]]).tolist()))
    width = dict(zip(names, sizes))

    def gather(group):
        return jnp.concatenate([w_in[:, offs[k]:offs[k] + width[k]] for k in group], axis=1)

    b_group = ["fq", "fk", "fv", "rv", "hi"]
    f_group = ["rq", "rk", "rg", "sz", "sxbc", "hq", "hf", "hg"]
    cols = {}
    for group in (b_group, f_group):
        pos = 0
        for k in group:
            cols[k] = pos // (xbc_w if k == "sxbc" else gw)
            pos += width[k]
    wb = gather(b_group).astype(BF16)
    wf = gather(f_group).astype(BF16)
    wst = gather(["ff", "sdt"]).T.astype(BF16)
    return wb, wf, wst, cols


def kernel(x, p, ffn1_norm, ffn1_w_up, ffn1_w_down, mix_norm, w_in, fox_f_bias, ret_norm, conv_w, conv_b, dt_bias, a_log, ssd_d, ssd_norm, hgrn_lower_bounds, hgrn_norm, w_out, ffn2_norm, ffn2_w_up, ffn2_w_down, ple_norm, ple_w_gate, ple_w_proj, final_norm):
    batch, s, d = x.shape
    depth = p.shape[0]
    n = batch * s
    nq = s // TQ_FOX
    h = x.reshape(n, d)
    cos_t, sin_t = _rope_tables(s)
    zeros4 = jnp.zeros((N_HEADS,), F32)
    for i in range(depth):
        h = _ffn(h, ffn1_norm[i], ffn1_w_up[i].astype(BF16), ffn1_w_down[i].astype(BF16))

        wb, wf, wst, cols = _split_w_in(w_in[i])
        pb, pf, st = _inproj(h, mix_norm[i], wb, wf, wst)
        bias = jnp.concatenate([fox_f_bias[i], dt_bias[i]]).reshape(2 * N_HEADS, 1)
        alog = jnp.concatenate([zeros4, a_log[i]]).reshape(2 * N_HEADS, 1)
        g_row, dt_row = _gates(st, bias, alog, batch)
        c4 = g_row[:N_HEADS].reshape(2, 2, batch, nq, TQ_FOX).transpose(2, 0, 3, 1, 4)
        gcol = jnp.concatenate([dt_row[N_HEADS:], g_row[N_HEADS:]], axis=0).T
        d_x = jnp.repeat(ssd_d[i], HEAD_DIM).reshape(1, GROUP_WIDTH)

        y_fox = _fox(pb, c4, batch)
        y_ret = _retention(pf, pb, cos_t, sin_t, ret_norm[i], batch, cols)
        y_ssd = _ssd(pf, gcol, g_row, conv_w[i], conv_b[i], d_x, ssd_norm[i], batch, cols)
        y_hg = _hgrn(pf, pb, hgrn_lower_bounds, hgrn_norm[i], i, batch, cols)
        h = _outproj(h, (y_fox, y_ret, y_ssd, y_hg), w_out[i].astype(BF16))

        h = _ffn(h, ffn2_norm[i], ffn2_w_up[i].astype(BF16), ffn2_w_down[i].astype(BF16))
        h = _ple(h, p[i].reshape(n, -1), ple_norm[i], ple_w_gate[i].astype(BF16),
                 ple_w_proj[i].astype(BF16), final_norm, final=(i == depth - 1))
    return h.reshape(batch, s, d)
```

```python
import functools
import math

import jax
import jax.numpy as jnp
import numpy as np
from jax import lax
from jax.experimental import pallas as pl
from jax.experimental.pallas import tpu as pltpu

F32 = jnp.float32
BF16 = jnp.bfloat16

HEAD_DIM = 64
N_HEADS = 4
GROUP_WIDTH = N_HEADS * HEAD_DIM
SSD_STATE = 128
CONV_WIDTH = 4
RMS_EPS = 1e-6
ROPE_BASE = 10000.0
LOG2E = 1.4426950408889634

V7X_LANES = 128
V7X_SUBLANES = 8
V7X_VMEM_BYTES = 64 * 1024 * 1024
MIB = 1024 * 1024

TM_FFN = 512
FF_CHUNK = 256
TQ_FOX = 256
C_RET = 128
C_SSD = 128
C_HGRN = 64
T_HGRN = 128
TL_GATES = 2048


def _cparams(semantics, vmem_mib):
    return pltpu.CompilerParams(dimension_semantics=semantics, vmem_limit_bytes=int(vmem_mib * MIB))


def _rmsnorm(x, w):
    return x * lax.rsqrt(jnp.mean(x * x, axis=-1, keepdims=True) + RMS_EPS) * w


def _silu(x):
    return x * jax.nn.sigmoid(x)


def _resident(shape):
    return pl.BlockSpec(shape, lambda *_: (0,) * len(shape), pipeline_mode=pl.Buffered(1))


def _ffn_body(h_ref, nw_ref, wup_ref, wdn_ref, o_ref, act_ref, *, d_ff):
    h = h_ref[...]
    x = _rmsnorm(h, nw_ref[...]).astype(BF16)
    for c in range(d_ff // FF_CHUNK):
        lo = c * FF_CHUNK
        gate = jnp.dot(x, wup_ref[:, lo:lo + FF_CHUNK], preferred_element_type=F32)
        up = jnp.dot(x, wup_ref[:, d_ff + lo:d_ff + lo + FF_CHUNK], preferred_element_type=F32)
        act_ref[:, lo:lo + FF_CHUNK] = (_silu(gate) * up).astype(BF16)
    o_ref[...] = h + 0.5 * jnp.dot(act_ref[...], wdn_ref[...], preferred_element_type=F32)


def _ffn(h, norm_w, w_up, w_down):
    n, d = h.shape
    d_ff = w_down.shape[0]
    return pl.pallas_call(
        functools.partial(_ffn_body, d_ff=d_ff),
        grid=(n // TM_FFN,),
        in_specs=[
            pl.BlockSpec((TM_FFN, d), lambda i: (i, 0)),
            _resident((1, d)),
            _resident((d, 2 * d_ff)),
            _resident((d_ff, d)),
        ],
        out_specs=pl.BlockSpec((TM_FFN, d), lambda i: (i, 0)),
        out_shape=jax.ShapeDtypeStruct((n, d), F32),
        scratch_shapes=[pltpu.VMEM((TM_FFN, d_ff), BF16)],
        compiler_params=_cparams(("parallel",), 48),
        name="ffn",
    )(h, norm_w.reshape(1, d), w_up, w_down)


def _inproj_body(h_ref, nw_ref, wb_ref, wf_ref, wqkvt_ref, wst_ref,
                 pb_ref, pf_ref, qt_ref, kt_ref, vt_ref, st_ref, *, q_scale):
    x = _rmsnorm(h_ref[...], nw_ref[...]).astype(BF16)
    gw = GROUP_WIDTH
    for c in range(wb_ref.shape[1] // gw):
        pb_ref[:, c * gw:(c + 1) * gw] = jnp.dot(x, wb_ref[:, c * gw:(c + 1) * gw],
                                                preferred_element_type=F32).astype(BF16)
    for c in range(wf_ref.shape[1] // gw):
        pf_ref[:, c * gw:(c + 1) * gw] = jnp.dot(x, wf_ref[:, c * gw:(c + 1) * gw],
                                                preferred_element_type=F32)
    nt = (((1,), (1,)), ((), ()))
    qkv = lax.dot_general(wqkvt_ref[...], x, nt, preferred_element_type=F32)
    qt_ref[...] = (qkv[0:gw] * q_scale).astype(BF16)
    kt_ref[...] = qkv[gw:2 * gw].astype(BF16)
    vt_ref[...] = qkv[2 * gw:3 * gw].astype(BF16)
    st_ref[...] = lax.dot_general(wst_ref[...], x, nt, preferred_element_type=F32)


def _inproj(h, norm_w, wb, wf, wqkvt, wst):
    n, d = h.shape
    nb, nf = wb.shape[1], wf.shape[1]
    gw = GROUP_WIDTH
    lanes_out = pl.BlockSpec((gw, TM_FFN), lambda i: (0, i))
    return pl.pallas_call(
        functools.partial(_inproj_body, q_scale=HEAD_DIM ** -0.5 * LOG2E),
        grid=(n // TM_FFN,),
        in_specs=[
            pl.BlockSpec((TM_FFN, d), lambda i: (i, 0)),
            _resident((1, d)),
            _resident((d, nb)),
            _resident((d, nf)),
            _resident((3 * gw, d)),
            _resident((2 * N_HEADS, d)),
        ],
        out_specs=[
            pl.BlockSpec((TM_FFN, nb), lambda i: (i, 0)),
            pl.BlockSpec((TM_FFN, nf), lambda i: (i, 0)),
            lanes_out, lanes_out, lanes_out,
            pl.BlockSpec((2 * N_HEADS, TM_FFN), lambda i: (0, i)),
        ],
        out_shape=[
            jax.ShapeDtypeStruct((n, nb), BF16),
            jax.ShapeDtypeStruct((n, nf), F32),
            jax.ShapeDtypeStruct((gw, n), BF16),
            jax.ShapeDtypeStruct((gw, n), BF16),
            jax.ShapeDtypeStruct((gw, n), BF16),
            jax.ShapeDtypeStruct((2 * N_HEADS, n), F32),
        ],
        compiler_params=_cparams(("parallel",), 48),
        name="inproj",
    )(h, norm_w.reshape(1, d), wb, wf, wqkvt, wst)


def _gates_body(st_ref, bias_ref, alog_ref, g_ref, dt_ref, hi_ref, mid_ref, lo_ref, carry_ref):
    @pl.when(pl.program_id(1) == 0)
    def _():
        carry_ref[...] = jnp.zeros_like(carry_ref)

    x = st_ref[...] + bias_ref[...]
    tl = x.shape[1]
    row = lax.broadcasted_iota(jnp.int32, x.shape, 0)
    lane = lax.broadcasted_iota(jnp.int32, x.shape, 1)
    lane_in_chunk = lane % C_SSD
    is_fox = row < N_HEADS
    dt = jax.nn.softplus(x)
    val = jnp.where(is_fox, jax.nn.log_sigmoid(x), dt * (-jnp.exp(alog_ref[...])))
    full, local = val, val
    shift = 1
    while shift < tl:
        full = full + jnp.where(lane >= shift, pltpu.roll(full, shift, 1), 0.0)
        if shift < C_SSD:
            local = local + jnp.where(lane_in_chunk >= shift, pltpu.roll(local, shift, 1), 0.0)
        shift *= 2
    cum = full + carry_ref[...]
    carry_ref[...] = jnp.sum(jnp.where(lane == tl - 1, cum, 0.0), axis=1, keepdims=True)
    c2 = cum * LOG2E
    g_ref[...] = jnp.where(is_fox, c2, local)
    dt_ref[...] = dt
    hi = c2.astype(BF16).astype(F32)
    mid = (c2 - hi).astype(BF16).astype(F32)
    hi_ref[...] = hi
    mid_ref[...] = mid
    lo_ref[...] = (c2 - hi - mid).astype(BF16).astype(F32)


def _gates(st, bias, alog, batch):
    rows, n = st.shape
    s = n // batch
    nt = s // TL_GATES
    spec = pl.BlockSpec((rows, TL_GATES), lambda b, t: (0, b * nt + t))
    col = pl.BlockSpec((rows, 1), lambda b, t: (0, 0))
    return pl.pallas_call(
        _gates_body,
        grid=(batch, nt),
        in_specs=[spec, col, col],
        out_specs=[spec] * 5,
        out_shape=[jax.ShapeDtypeStruct((rows, n), F32)] * 5,
        scratch_shapes=[pltpu.VMEM((rows, 1), F32)],
        compiler_params=_cparams(("parallel", "arbitrary"), 16),
        name="gates",
    )(st, bias, alog)


def _rope_body(inv_ref, cos_ref, sin_ref):
    rows = cos_ref.shape[0]
    pos = (lax.broadcasted_iota(jnp.int32, cos_ref.shape, 0) + pl.program_id(0) * rows).astype(F32)
    lane = lax.broadcasted_iota(jnp.int32, cos_ref.shape, 1)
    ang = pos * inv_ref[...]
    cos_ref[...] = jnp.cos(ang)
    sin = jnp.sin(ang)
    sin_ref[...] = jnp.where(lane % HEAD_DIM < HEAD_DIM // 2, -sin, sin)


def _rope_tables(s):
    half = HEAD_DIM // 2
    inv_freq = ROPE_BASE ** (-jnp.arange(half, dtype=F32) / half)
    inv = jnp.tile(inv_freq, V7X_LANES // half).reshape(1, V7X_LANES)
    rows = 1024
    return pl.pallas_call(
        _rope_body,
        grid=(s // rows,),
        in_specs=[pl.BlockSpec((1, V7X_LANES), lambda i: (0, 0))],
        out_specs=[pl.BlockSpec((rows, V7X_LANES), lambda i: (i, 0))] * 2,
        out_shape=[jax.ShapeDtypeStruct((s, V7X_LANES), F32)] * 2,
        compiler_params=_cparams(("parallel",), 16),
        name="rope_tables",
    )(inv)


FOX_C_PARTS = 3
FOX_NEG = -1e30


FOX_SKIP_LOG2 = 160.0
FOX_FAST_LOG2 = 80.0
FOX_BOUND_SLACK = 1.01


def _fox_plan_body(qt_ref, kt_ref, c_ref, plan_ref, *, tq):
    gw, s = qt_ref.shape
    nblk = s // tq
    lanes = plan_ref.shape[-1]

    def head_norm2(x_ref):
        x = x_ref[...].astype(F32)
        x = x * x
        return jnp.concatenate([jnp.sum(x[h * HEAD_DIM:(h + 1) * HEAD_DIM], axis=0, keepdims=True)
                                for h in range(N_HEADS)], axis=0)

    qn2 = head_norm2(qt_ref)
    kmax2 = jnp.max(head_norm2(kt_ref), axis=1, keepdims=True)
    c = c_ref[0:N_HEADS, :]
    lane = lax.broadcasted_iota(jnp.int32, (N_HEADS, lanes), 1)
    qmax2 = jnp.zeros((N_HEADS, lanes), F32)
    cmax = jnp.zeros((N_HEADS, lanes), F32)
    cmin = jnp.full((N_HEADS, lanes), -jnp.inf, F32)
    for j in range(nblk):
        blk = slice(j * tq, (j + 1) * tq)
        qmax2 = jnp.where(lane == j, jnp.max(qn2[:, blk], axis=1, keepdims=True), qmax2)
        cmax = jnp.where(lane == j, jnp.max(c[:, blk], axis=1, keepdims=True), cmax)
        cmin = jnp.where(lane == j, jnp.min(c[:, blk], axis=1, keepdims=True), cmin)
    shift = 1
    while shift < nblk:
        cmin = jnp.minimum(cmin, jnp.where(lane >= shift, pltpu.roll(cmin, shift, 1), jnp.inf))
        shift *= 2
    spread = 2.0 * FOX_BOUND_SLACK * jnp.sqrt(qmax2 * kmax2)
    limit = cmax + spread + FOX_SKIP_LOG2
    skip = jnp.zeros((N_HEADS, lanes), F32)
    for i in range(nblk):
        lim_i = jnp.sum(jnp.where(lane == i, limit, 0.0), axis=1, keepdims=True)
        n_i = jnp.sum(jnp.where(cmin > lim_i, 1.0, 0.0), axis=1, keepdims=True)
        skip = jnp.where(lane == i, n_i, skip)
    fast = jnp.where(spread <= FOX_FAST_LOG2, 1.0, 0.0)
    rows = []
    for p in range(N_HEADS // 2):
        rows.append(jnp.minimum(skip[2 * p:2 * p + 1], skip[2 * p + 1:2 * p + 2]))
    for p in range(N_HEADS // 2):
        rows.append(jnp.minimum(fast[2 * p:2 * p + 1], fast[2 * p + 1:2 * p + 2]))
    rows.append(jnp.zeros((V7X_SUBLANES - len(rows), lanes), F32))
    plan_ref[...] = jnp.concatenate(rows, axis=0).astype(jnp.int32)


def _fox_plan(qt, kt, g_row, batch):
    gw, n = qt.shape
    s = n // batch
    lanes = V7X_LANES
    assert s // TQ_FOX <= lanes
    seq = pl.BlockSpec((gw, s), lambda b: (0, b))
    return pl.pallas_call(
        functools.partial(_fox_plan_body, tq=TQ_FOX),
        grid=(batch,),
        in_specs=[seq, seq, pl.BlockSpec((g_row.shape[0], s), lambda b: (0, b))],
        out_specs=pl.BlockSpec((None, V7X_SUBLANES, lanes), lambda b: (b, 0, 0)),
        out_shape=jax.ShapeDtypeStruct((batch, V7X_SUBLANES, lanes), jnp.int32),
        compiler_params=_cparams(("parallel",), 56),
        name="fox_plan",
    )(qt, kt, g_row)


def _fox_body(skip_ref, fast_ref, qt_ref, ka_ref, vt_ref, o_ref, m_ref, l_ref, acc_ref):
    b, pair, i = pl.program_id(0), pl.program_id(1), pl.program_id(2)
    w, tq = qt_ref.shape
    tk = 2 * tq
    idx = (b * pl.num_programs(1) + pair) * pl.num_programs(2) + i
    skip = skip_ref[idx]
    qt = qt_ref[...]
    row = lax.broadcasted_iota(jnp.int32, qt.shape, 0)
    qa = []
    for hh in range(2):
        q_rows = jnp.where(row // HEAD_DIM == hh, qt, jnp.zeros_like(qt))
        c_rows = jnp.where(row // FOX_C_PARTS == hh, -1.0, 0.0).astype(BF16)
        qa.append(jnp.concatenate([q_rows, c_rows], axis=0))

    def scores(start, size, hh):
        ka = ka_ref[pl.ds(start, size), :]
        return jnp.dot(ka, qa[hh], preferred_element_type=F32)

    def weighted_values(start, size, hh, p):
        vt = vt_ref[hh * HEAD_DIM:(hh + 1) * HEAD_DIM, pl.ds(start, size)]
        return jnp.dot(vt, p.astype(BF16), preferred_element_type=F32)

    def earlier_blocks(block):
        def body(j, carry):
            block(pl.multiple_of(j * tk, tk), tk)
            return carry

        lax.fori_loop(skip // 2, i // 2, body, 0)

        @pl.when(jnp.logical_and(i % 2 == 1, skip < i))
        def _():
            block(pl.multiple_of((i - 1) * tq, tq), tq)

    diag_start = pl.multiple_of(i * tq, tq)
    key = lax.broadcasted_iota(jnp.int32, (tq, tq), 0)
    qry = lax.broadcasted_iota(jnp.int32, (tq, tq), 1)

    @pl.when(fast_ref[idx] == 1)
    def _():
        for hh in range(2):
            s = scores(diag_start, tq, hh)
            ref = jnp.sum(jnp.where(key == qry, s, 0.0), axis=0, keepdims=True)
            p = jnp.exp2(jnp.where(key <= qry, s - ref, FOX_NEG))
            m_ref[hh] = ref
            l_ref[hh] = jnp.sum(p, axis=0, keepdims=True)
            acc_ref[hh] = weighted_values(diag_start, tq, hh, p)

        def block(start, size):
            for hh in range(2):
                p = jnp.exp2(scores(start, size, hh) - m_ref[hh])
                l_ref[hh] = l_ref[hh] + jnp.sum(p, axis=0, keepdims=True)
                acc_ref[hh] = acc_ref[hh] + weighted_values(start, size, hh, p)

        earlier_blocks(block)

    @pl.when(fast_ref[idx] != 1)
    def _():
        m_ref[...] = jnp.full(m_ref.shape, FOX_NEG, F32)
        l_ref[...] = jnp.zeros_like(l_ref)
        acc_ref[...] = jnp.zeros_like(acc_ref)

        def update(start, size, hh, s):
            m_old = m_ref[hh]
            m_new = jnp.maximum(m_old, jnp.max(s, axis=0, keepdims=True))
            alpha = jnp.exp2(m_old - m_new)
            p = jnp.exp2(s - m_new)
            l_ref[hh] = alpha * l_ref[hh] + jnp.sum(p, axis=0, keepdims=True)
            acc_ref[hh] = alpha * acc_ref[hh] + weighted_values(start, size, hh, p)
            m_ref[hh] = m_new

        def block(start, size):
            for hh in range(2):
                update(start, size, hh, scores(start, size, hh))

        earlier_blocks(block)
        for hh in range(2):
            update(diag_start, tq, hh, jnp.where(key <= qry, scores(diag_start, tq, hh), FOX_NEG))

    out = jnp.concatenate([acc_ref[hh] / l_ref[hh] for hh in range(2)], axis=0)
    o_ref[...] = out.T.astype(o_ref.dtype)


def _fox(qt, kaug, vt, plan, batch):
    gw, n = qt.shape
    s = n // batch
    nq = s // TQ_FOX
    w = V7X_LANES
    pairs = gw // w
    skip = plan[:, 0:pairs, 0:nq].reshape(-1)
    fast = plan[:, pairs:2 * pairs, 0:nq].reshape(-1)
    return pl.pallas_call(
        _fox_body,
        grid_spec=pltpu.PrefetchScalarGridSpec(
            num_scalar_prefetch=2,
            grid=(batch, pairs, nq),
            in_specs=[
                pl.BlockSpec((w, TQ_FOX), lambda b, p, i, *_: (p, b * nq + i)),
                pl.BlockSpec((s, 2 * w), lambda b, p, i, *_: (b, p)),
                pl.BlockSpec((w, s), lambda b, p, i, *_: (p, b)),
            ],
            out_specs=pl.BlockSpec((TQ_FOX, w), lambda b, p, i, *_: (b * nq + i, p)),
            scratch_shapes=[
                pltpu.VMEM((2, 1, TQ_FOX), F32),
                pltpu.VMEM((2, 1, TQ_FOX), F32),
                pltpu.VMEM((2, HEAD_DIM, TQ_FOX), F32),
            ],
        ),
        out_shape=jax.ShapeDtypeStruct((n, gw), BF16),
        compiler_params=_cparams(("parallel", "parallel", "arbitrary"), 48),
        name="fox",
    )(skip, fast, qt, kaug, vt)


def _head_of_lane(shape, axis):
    return lax.broadcasted_iota(jnp.int32, shape, axis) // HEAD_DIM


def _same_head(shape):
    return _head_of_lane(shape, 0) == _head_of_lane(shape, 1)


def _per_head(values, lane_head):
    out = values[N_HEADS - 1]
    for h in range(N_HEADS - 2, -1, -1):
        out = jnp.where(lane_head == h, values[h], out)
    return out


def _head_rmsnorm(o, w, ones_bd):
    ms = jnp.dot((o * o).astype(BF16), ones_bd, preferred_element_type=F32) * (1.0 / HEAD_DIM)
    return o * lax.rsqrt(ms + RMS_EPS) * w


def _ret_body(rq_ref, rk_ref, rv_ref, rg_ref, cos_ref, sin_ref, nw_ref, o_ref,
              state_ref, dmat_ref, tab_ref, bd_ref):
    c = rq_ref.shape[0]
    gw = GROUP_WIDTH
    log_gamma = [math.log1p(-(2.0 ** (-5.0 - h))) for h in range(N_HEADS)]

    @pl.when(pl.program_id(1) == 0)
    def _():
        state_ref[...] = jnp.zeros_like(state_ref)
        r = lax.broadcasted_iota(jnp.int32, (c, c), 0)
        s = lax.broadcasted_iota(jnp.int32, (c, c), 1)
        dist = (r - s).astype(F32)
        for h in range(N_HEADS):
            dmat_ref[h] = jnp.where(s <= r, jnp.exp(dist * log_gamma[h]), 0.0)
        lg = _per_head(log_gamma, _head_of_lane((c, gw), 1))
        t = lax.broadcasted_iota(jnp.int32, (c, gw), 0).astype(F32)
        tab_ref[0] = jnp.exp((t + 1.0) * lg)
        tab_ref[1] = jnp.exp((c - 1.0 - t) * lg)
        tab_ref[2] = jnp.exp(c * lg)
        bd_ref[...] = jnp.where(_same_head((gw, gw)), 1.0, 0.0).astype(BF16)

    lane = lax.broadcasted_iota(jnp.int32, (c, gw), 1)
    lane_head = lane // HEAD_DIM
    first_half = lane % HEAD_DIM < HEAD_DIM // 2
    cosx = jnp.concatenate([cos_ref[...]] * (gw // V7X_LANES), axis=1)
    sinx = jnp.concatenate([sin_ref[...]] * (gw // V7X_LANES), axis=1)

    def rotary(x):
        swapped = jnp.where(first_half, pltpu.roll(x, gw - HEAD_DIM // 2, 1),
                            pltpu.roll(x, HEAD_DIM // 2, 1))
        return x * cosx + swapped * sinx

    q = rotary(rq_ref[...])
    k = rotary(rk_ref[...]) * (HEAD_DIM ** -0.5)
    v = rv_ref[...]
    qb, kb = q.astype(BF16), k.astype(BF16)
    zero = jnp.zeros_like(qb)
    o = jnp.zeros((c, gw), F32)
    for h in range(N_HEADS):
        sel = lane_head == h
        sc = lax.dot_general(jnp.where(sel, qb, zero), kb, (((1,), (1,)), ((), ())),
                             preferred_element_type=F32)
        p = (sc * dmat_ref[h]).astype(BF16)
        o = o + jnp.where(sel, jnp.dot(p, v, preferred_element_type=F32), 0.0)
    st = state_ref[...]
    o = o + lax.dot_general(qb, st.astype(BF16), (((1,), (1,)), ((), ())),
                            preferred_element_type=F32) * tab_ref[0]
    kd = (k * tab_ref[1]).astype(BF16)
    vt = v.astype(F32).T.astype(BF16)
    upd = jnp.dot(vt, kd, preferred_element_type=F32)
    state_ref[...] = st * tab_ref[2, 0:1, :] + jnp.where(_same_head((gw, gw)), upd, 0.0)
    y = _head_rmsnorm(o, nw_ref[...], bd_ref[...]) * _silu(rg_ref[...])
    o_ref[...] = y.astype(o_ref.dtype)


def _retention(pf, pb, cos_t, sin_t, norm_w, batch, cols):
    n = pf.shape[0]
    s = n // batch
    nt = s // C_RET
    gw = GROUP_WIDTH

    def tok(col):
        return pl.BlockSpec((C_RET, gw), lambda b, t: (b * nt + t, col))

    tab = pl.BlockSpec((C_RET, V7X_LANES), lambda b, t: (t, 0))
    return pl.pallas_call(
        _ret_body,
        grid=(batch, nt),
        in_specs=[tok(cols["rq"]), tok(cols["rk"]), tok(cols["rv"]), tok(cols["rg"]), tab, tab,
                  pl.BlockSpec((1, gw), lambda b, t: (0, 0))],
        out_specs=pl.BlockSpec((C_RET, gw), lambda b, t: (b * nt + t, 0)),
        out_shape=jax.ShapeDtypeStruct((n, gw), BF16),
        scratch_shapes=[
            pltpu.VMEM((gw, gw), F32),
            pltpu.VMEM((N_HEADS, C_RET, C_RET), F32),
            pltpu.VMEM((3, C_RET, gw), F32),
            pltpu.VMEM((gw, gw), BF16),
        ],
        compiler_params=_cparams(("parallel", "arbitrary"), 24),
        name="retention",
    )(pf, pf, pb, pf, cos_t, sin_t, norm_w.reshape(1, gw))


def _ssd_body(xbc_ref, z_ref, gcol_ref, grow_ref, cw_ref, cb_ref, d_ref, nw_ref, o_ref,
              xpad_ref, state_ref):
    c = xbc_ref.shape[0]
    gw = GROUP_WIDTH
    pad = V7X_SUBLANES

    @pl.when(pl.program_id(1) == 0)
    def _():
        state_ref[...] = jnp.zeros_like(state_ref)
        xpad_ref[0:pad, :] = jnp.zeros((pad, xpad_ref.shape[1]), F32)

    xpad_ref[pad:pad + c, :] = xbc_ref[...]
    conv = cb_ref[...]
    for j in range(CONV_WIDTH):
        off = pad - (CONV_WIDTH - 1) + j
        conv = conv + cw_ref[j:j + 1, :] * xpad_ref[off:off + c, :]
    xpad_ref[0:pad, :] = xpad_ref[c:c + pad, :]
    xbc = _silu(conv)
    xs = xbc[:, 0:gw]
    bm = xbc[:, gw:gw + SSD_STATE]
    cm = xbc[:, gw + SSD_STATE:gw + 2 * SSD_STATE]

    lane_head = _head_of_lane((c, gw), 1)
    gcol = gcol_ref[...]
    dtx = _per_head([gcol[:, h:h + 1] for h in range(N_HEADS)], lane_head)
    bx = _per_head([gcol[:, N_HEADS + h:N_HEADS + h + 1] for h in range(N_HEADS)], lane_head)
    blast = bx[c - 1:c, :]
    v = xs * dtx
    vb = v.astype(BF16)

    sc = lax.dot_general(cm.astype(BF16), bm.astype(BF16), (((1,), (1,)), ((), ())),
                         preferred_element_type=F32)
    r = lax.broadcasted_iota(jnp.int32, (c, c), 0)
    s = lax.broadcasted_iota(jnp.int32, (c, c), 1)
    o = jnp.zeros((c, gw), F32)
    for h in range(N_HEADS):
        bcol = gcol[:, N_HEADS + h:N_HEADS + h + 1]
        brow = grow_ref[N_HEADS + h:N_HEADS + h + 1, :]
        decay = jnp.where(s <= r, jnp.exp(jnp.minimum(bcol - brow, 0.0)), 0.0)
        p = (sc * decay).astype(BF16)
        o = o + jnp.where(lane_head == h, jnp.dot(p, vb, preferred_element_type=F32), 0.0)
    st = state_ref[...]
    o = o + jnp.dot(cm.astype(BF16), st.astype(BF16), preferred_element_type=F32) * jnp.exp(bx)
    vd = (v * jnp.exp(blast - bx)).astype(BF16)
    state_ref[...] = st * jnp.exp(blast) + jnp.dot(bm.T.astype(BF16), vd, preferred_element_type=F32)
    o = o + d_ref[...] * xs
    y = _rmsnorm(o * _silu(z_ref[...]), nw_ref[...])
    o_ref[...] = y.astype(o_ref.dtype)


def _ssd(pf, gcol, grow, conv_w, conv_b, d_x, norm_w, batch, cols):
    n = pf.shape[0]
    s = n // batch
    nt = s // C_SSD
    gw = GROUP_WIDTH
    xbc_w = gw + 2 * SSD_STATE
    const = lambda shape: pl.BlockSpec(shape, lambda b, t: (0, 0))
    return pl.pallas_call(
        _ssd_body,
        grid=(batch, nt),
        in_specs=[
            pl.BlockSpec((C_SSD, xbc_w), lambda b, t: (b * nt + t, cols["sxbc"])),
            pl.BlockSpec((C_SSD, gw), lambda b, t: (b * nt + t, cols["sz"])),
            pl.BlockSpec((C_SSD, 2 * N_HEADS), lambda b, t: (b * nt + t, 0)),
            pl.BlockSpec((2 * N_HEADS, C_SSD), lambda b, t: (0, b * nt + t)),
            const((CONV_WIDTH, xbc_w)), const((1, xbc_w)), const((1, gw)), const((1, gw)),
        ],
        out_specs=pl.BlockSpec((C_SSD, gw), lambda b, t: (b * nt + t, 0)),
        out_shape=jax.ShapeDtypeStruct((n, gw), BF16),
        scratch_shapes=[
            pltpu.VMEM((C_SSD + 2 * V7X_SUBLANES, xbc_w), F32),
            pltpu.VMEM((SSD_STATE, gw), F32),
        ],
        compiler_params=_cparams(("parallel", "arbitrary"), 24),
        name="ssd",
    )(pf, pf, gcol, grow, conv_w, conv_b.reshape(1, xbc_w), d_x, norm_w.reshape(1, gw))


def _hgrn_w_rows():
    c, sub = C_HGRN, V7X_SUBLANES
    return sum(c if sub * a < c // 2 else c // 2 for a in range(c // sub))


def _hgrn_body(hq_ref, hf_ref, hi_ref, hg_ref, lbw_ref, nw_ref, o_ref,
               state_ref, pad_ref, sh_ref, w_ref, bd_ref, *, layer):
    c, sub = C_HGRN, V7X_SUBLANES
    half = c // 2
    gw = GROUP_WIDTH

    @pl.when(pl.program_id(1) == 0)
    def _():
        state_ref[...] = jnp.zeros_like(state_ref)
        pad_ref[:, 0:c, :] = jnp.zeros((3, c, gw), F32)
        bd_ref[...] = jnp.where(_same_head((gw, gw)), 1.0, 0.0).astype(BF16)

    lbw = lbw_ref[...]
    e = jnp.exp(lbw - jnp.max(lbw, axis=0, keepdims=True))
    sm = e / jnp.sum(e, axis=0, keepdims=True)
    cs = sm[0:1, :]
    first = cs
    for d in range(1, layer + 1):
        cs = cs + sm[d:d + 1, :]
    lb = cs - first
    log_lb, log_1mlb = jnp.log(jnp.maximum(lb, 0.0)), jnp.log1p(-lb)
    bd = bd_ref[...]
    row = lax.broadcasted_iota(jnp.int32, (c, gw), 0)

    for ci in range(hq_ref.shape[0] // c):
        rows = slice(ci * c, (ci + 1) * c)
        log_f = jnp.logaddexp(log_lb, log_1mlb + jax.nn.log_sigmoid(hf_ref[rows, :]))
        k = 1.0 - jnp.exp(log_f)
        b = log_f
        shift = 1
        while shift < c:
            b = b + jnp.where(row >= shift, pltpu.roll(b, shift, 0), 0.0)
            shift *= 2
        b2 = b * LOG2E
        q = hq_ref[rows, :]
        v = hi_ref[rows, :].astype(F32)
        pad_ref[0, c:2 * c, :] = k
        pad_ref[1, c:2 * c, :] = b2
        pad_ref[2, c:2 * c, :] = v

        o_top = jnp.zeros((half, gw), F32)
        o_bot = jnp.zeros((half, gw), F32)
        for r in range(sub):
            if r == 0:
                src = pad_ref
            else:
                for x in range(3):
                    sh_ref[x] = pltpu.roll(pad_ref[x], r, 0)
                src = sh_ref
            spans = []
            pos = 0
            for a in range(c // sub):
                lo = c - sub * a
                r0 = 0 if sub * a < half else half
                ks = src[0, lo + r0:lo + c, :]
                bs = src[1, lo + r0:lo + c, :]
                w = q[r0:] * ks * jnp.exp2(b2[r0:] - bs)
                w_ref[pos:pos + c - r0, :] = w.astype(BF16)
                spans.append((pos, r0, lo))
                pos += c - r0
            seg = jnp.dot(w_ref[...], bd, preferred_element_type=F32)
            for pos, r0, lo in spans:
                term = seg[pos:pos + c - r0] * src[2, lo + r0:lo + c, :]
                if r0 == 0:
                    o_top = o_top + term[:half]
                    o_bot = o_bot + term[half:]
                else:
                    o_bot = o_bot + term
        o = jnp.concatenate([o_top, o_bot], axis=0)
        st = state_ref[...]
        o = o + lax.dot_general((q * jnp.exp2(b2)).astype(BF16), st.astype(BF16),
                                (((1,), (1,)), ((), ())), preferred_element_type=F32)
        blast = b2[c - 1:c, :]
        kd = (k * jnp.exp2(blast - b2)).astype(BF16)
        upd = jnp.dot(v.T.astype(BF16), kd, preferred_element_type=F32)
        state_ref[...] = st * jnp.exp2(blast) + jnp.where(_same_head((gw, gw)), upd, 0.0)
        y = _head_rmsnorm(o, nw_ref[...], bd) * _silu(hg_ref[rows, :])
        o_ref[rows, :] = y.astype(o_ref.dtype)


def _hgrn(pf, pb, lower_bounds, norm_w, layer, batch, cols):
    n = pf.shape[0]
    s = n // batch
    nt = s // T_HGRN
    gw = GROUP_WIDTH
    depth = lower_bounds.shape[0]

    def tok(col):
        return pl.BlockSpec((T_HGRN, gw), lambda b, t: (b * nt + t, col))

    return pl.pallas_call(
        functools.partial(_hgrn_body, layer=layer),
        grid=(batch, nt),
        in_specs=[tok(cols["hq"]), tok(cols["hf"]), tok(cols["hi"]), tok(cols["hg"]),
                  pl.BlockSpec((depth, gw), lambda b, t: (0, 0)),
                  pl.BlockSpec((1, gw), lambda b, t: (0, 0))],
        out_specs=pl.BlockSpec((T_HGRN, gw), lambda b, t: (b * nt + t, 0)),
        out_shape=jax.ShapeDtypeStruct((n, gw), BF16),
        scratch_shapes=[
            pltpu.VMEM((gw, gw), F32),
            pltpu.VMEM((3, 2 * C_HGRN, gw), F32),
            pltpu.VMEM((3, 2 * C_HGRN, gw), F32),
            pltpu.VMEM((_hgrn_w_rows(), gw), BF16),
            pltpu.VMEM((gw, gw), BF16),
        ],
        compiler_params=_cparams(("parallel", "arbitrary"), 24),
        name="hgrn2",
    )(pf, pf, pb, pf, lower_bounds, norm_w.reshape(1, gw))


def _outproj_body(h_ref, y0_ref, y1_ref, y2_ref, y3_ref, w_ref, o_ref):
    gw = GROUP_WIDTH
    acc = h_ref[...]
    for m, y_ref in enumerate((y0_ref, y1_ref, y2_ref, y3_ref)):
        acc = acc + jnp.dot(y_ref[...], w_ref[m * gw:(m + 1) * gw, :], preferred_element_type=F32)
    o_ref[...] = acc


def _outproj(h, ys, w_out):
    n, d = h.shape
    gw = GROUP_WIDTH
    tok = pl.BlockSpec((TM_FFN, d), lambda i: (i, 0))
    ytok = pl.BlockSpec((TM_FFN, gw), lambda i: (i, 0))
    return pl.pallas_call(
        _outproj_body,
        grid=(n // TM_FFN,),
        in_specs=[tok, ytok, ytok, ytok, ytok, _resident(w_out.shape)],
        out_specs=tok,
        out_shape=jax.ShapeDtypeStruct((n, d), F32),
        compiler_params=_cparams(("parallel",), 32),
        name="outproj",
    )(h, *ys, w_out)


def _ple_body(h_ref, p_ref, nw_ref, wg_ref, wp_ref, fw_ref, o_ref, *, final):
    h = h_ref[...]
    x = _rmsnorm(h, nw_ref[...]).astype(BF16)
    gate = jax.nn.sigmoid(jnp.dot(x, wg_ref[...], preferred_element_type=F32))
    emb = jnp.dot(p_ref[...].astype(BF16), wp_ref[...], preferred_element_type=F32)
    out = h + gate * emb
    if final:
        out = _rmsnorm(out, fw_ref[...])
    o_ref[...] = out


def _ple(h, p, norm_w, w_gate, w_proj, final_w, final):
    n, d = h.shape
    pd = p.shape[1]
    tok = pl.BlockSpec((TM_FFN, d), lambda i: (i, 0))
    return pl.pallas_call(
        functools.partial(_ple_body, final=final),
        grid=(n // TM_FFN,),
        in_specs=[tok, pl.BlockSpec((TM_FFN, pd), lambda i: (i, 0)), _resident((1, d)),
                  _resident((d, d)), _resident((pd, d)), _resident((1, d))],
        out_specs=tok,
        out_shape=jax.ShapeDtypeStruct((n, d), F32),
        compiler_params=_cparams(("parallel",), 32),
        name="ple",
    )(h, p, norm_w.reshape(1, d), w_gate, w_proj, final_w.reshape(1, d))


def _split_w_in(w_in):
    gw = GROUP_WIDTH
    xbc_w = gw + 2 * SSD_STATE
    names = ["fq", "fk", "fv", "ff", "rq", "rk", "rv", "rg", "sz", "sxbc", "sdt", "hq", "hf", "hi", "hg"]
    sizes = [gw, gw, gw, N_HEADS, gw, gw, gw, gw, gw, xbc_w, N_HEADS, gw, gw, gw, gw]
    offs = dict(zip(names, np.concatenate([[0], np.cumsum(sizes)[:-1]]).tolist()))
    width = dict(zip(names, sizes))

    def gather(group):
        return jnp.concatenate([w_in[:, offs[k]:offs[k] + width[k]] for k in group], axis=1)

    b_group = ["rv", "hi"]
    f_group = ["rq", "rk", "rg", "sz", "sxbc", "hq", "hf", "hg"]
    cols = {}
    for group in (b_group, f_group):
        pos = 0
        for k in group:
            cols[k] = pos // (xbc_w if k == "sxbc" else gw)
            pos += width[k]
    wb = gather(b_group).astype(BF16)
    wf = gather(f_group).astype(BF16)
    wqkvt = gather(["fq", "fk", "fv"]).T.astype(BF16)
    wst = gather(["ff", "sdt"]).T.astype(BF16)
    return wb, wf, wqkvt, wst, cols


def _fox_key_operand(kt, c_parts):
    n = kt.shape[1]
    w = V7X_LANES
    pairs = GROUP_WIDTH // w
    cp = jnp.stack(c_parts).transpose(2, 1, 0).reshape(n, pairs, -1).astype(BF16)
    cp = jnp.pad(cp, ((0, 0), (0, 0), (0, w - cp.shape[-1])))
    return jnp.concatenate([kt.T.reshape(n, pairs, w), cp], axis=-1).reshape(n, pairs * 2 * w)


def kernel(x, p, ffn1_norm, ffn1_w_up, ffn1_w_down, mix_norm, w_in, fox_f_bias, ret_norm, conv_w, conv_b, dt_bias, a_log, ssd_d, ssd_norm, hgrn_lower_bounds, hgrn_norm, w_out, ffn2_norm, ffn2_w_up, ffn2_w_down, ple_norm, ple_w_gate, ple_w_proj, final_norm):
    batch, s, d = x.shape
    depth = p.shape[0]
    n = batch * s
    h = x.reshape(n, d)
    cos_t, sin_t = _rope_tables(s)
    zeros4 = jnp.zeros((N_HEADS,), F32)
    for i in range(depth):
        h = _ffn(h, ffn1_norm[i], ffn1_w_up[i].astype(BF16), ffn1_w_down[i].astype(BF16))

        wb, wf, wqkvt, wst, cols = _split_w_in(w_in[i])
        pb, pf, qt, kt, vt, st = _inproj(h, mix_norm[i], wb, wf, wqkvt, wst)
        bias = jnp.concatenate([fox_f_bias[i], dt_bias[i]]).reshape(2 * N_HEADS, 1)
        alog = jnp.concatenate([zeros4, a_log[i]]).reshape(2 * N_HEADS, 1)
        g_row, dt_row, c_hi, c_mid, c_lo = _gates(st, bias, alog, batch)
        kaug = _fox_key_operand(kt, [c[:N_HEADS] for c in (c_hi, c_mid, c_lo)])
        gcol = jnp.concatenate([dt_row[N_HEADS:], g_row[N_HEADS:]], axis=0).T
        d_x = jnp.repeat(ssd_d[i], HEAD_DIM).reshape(1, GROUP_WIDTH)

        y_fox = _fox(qt, kaug, vt, _fox_plan(qt, kt, g_row, batch), batch)
        y_ret = _retention(pf, pb, cos_t, sin_t, ret_norm[i], batch, cols)
        y_ssd = _ssd(pf, gcol, g_row, conv_w[i], conv_b[i], d_x, ssd_norm[i], batch, cols)
        y_hg = _hgrn(pf, pb, hgrn_lower_bounds, hgrn_norm[i], i, batch, cols)
        h = _outproj(h, (y_fox, y_ret, y_ssd, y_hg), w_out[i].astype(BF16))

        h = _ffn(h, ffn2_norm[i], ffn2_w_up[i].astype(BF16), ffn2_w_down[i].astype(BF16))
        h = _ple(h, p[i].reshape(n, -1), ple_norm[i], ple_w_gate[i].astype(BF16),
                 ple_w_proj[i].astype(BF16), final_norm, final=(i == depth - 1))
    return h.reshape(batch, s, d)
```

```python
import functools
import math

import jax
import jax.numpy as jnp
import numpy as np
from jax import lax
from jax.experimental import pallas as pl
from jax.experimental.pallas import tpu as pltpu

F32 = jnp.float32
BF16 = jnp.bfloat16

HEAD_DIM = 64
N_HEADS = 4
GROUP_WIDTH = N_HEADS * HEAD_DIM
SSD_STATE = 128
CONV_WIDTH = 4
RMS_EPS = 1e-6
ROPE_BASE = 10000.0
LOG2E = 1.4426950408889634

V7X_LANES = 128
V7X_SUBLANES = 8
V7X_VMEM_BYTES = 64 * 1024 * 1024
MIB = 1024 * 1024

TM_FFN = 512
FF_CHUNK = 256
TQ_FOX = 256
C_RET = 128
C_SSD = 128
C_HGRN = 32
T_HGRN = 256
T_MIX = 512
TL_GATES = 2048


def _cparams(semantics, vmem_mib):
    return pltpu.CompilerParams(dimension_semantics=semantics, vmem_limit_bytes=int(vmem_mib * MIB))


def _rmsnorm(x, w):
    return x * lax.rsqrt(jnp.mean(x * x, axis=-1, keepdims=True) + RMS_EPS) * w


def _silu(x):
    return x * jax.nn.sigmoid(x)


def _resident(shape):
    return pl.BlockSpec(shape, lambda *_: (0,) * len(shape), pipeline_mode=pl.Buffered(1))


def _ffn_math(h, nw_ref, wup_ref, wdn_ref, act_ref):
    d_ff = wdn_ref.shape[0]
    x = _rmsnorm(h, nw_ref[...]).astype(BF16)
    for c in range(d_ff // FF_CHUNK):
        lo = c * FF_CHUNK
        gate = jnp.dot(x, wup_ref[:, lo:lo + FF_CHUNK], preferred_element_type=F32)
        up = jnp.dot(x, wup_ref[:, d_ff + lo:d_ff + lo + FF_CHUNK], preferred_element_type=F32)
        act_ref[:, lo:lo + FF_CHUNK] = (_silu(gate) * up).astype(BF16)
    return h + 0.5 * jnp.dot(act_ref[...], wdn_ref[...], preferred_element_type=F32)


def _ffn_body(h_ref, nw_ref, wup_ref, wdn_ref, o_ref, act_ref):
    o_ref[...] = _ffn_math(h_ref[...], nw_ref, wup_ref, wdn_ref, act_ref)


def _ffn(h, norm_w, w_up, w_down):
    n, d = h.shape
    d_ff = w_down.shape[0]
    return pl.pallas_call(
        _ffn_body,
        grid=(n // TM_FFN,),
        in_specs=[
            pl.BlockSpec((TM_FFN, d), lambda i: (i, 0)),
            _resident((1, d)),
            _resident((d, 2 * d_ff)),
            _resident((d_ff, d)),
        ],
        out_specs=pl.BlockSpec((TM_FFN, d), lambda i: (i, 0)),
        out_shape=jax.ShapeDtypeStruct((n, d), F32),
        scratch_shapes=[pltpu.VMEM((TM_FFN, d_ff), BF16)],
        compiler_params=_cparams(("parallel",), 48),
        name="ffn",
    )(h, norm_w.reshape(1, d), w_up, w_down)


def _inproj_body(h_ref, nw_ref, wb_ref, wf_ref, wqkt_ref, wst_ref,
                 pb_ref, pf_ref, qt_ref, kt_ref, st_ref, *, q_scale):
    x = _rmsnorm(h_ref[...], nw_ref[...]).astype(BF16)
    gw = GROUP_WIDTH
    for c in range(wb_ref.shape[1] // gw):
        acc = jnp.dot(x, wb_ref[:, c * gw:(c + 1) * gw], preferred_element_type=F32)
        if c == 0:
            acc = acc * q_scale
        pb_ref[:, c * gw:(c + 1) * gw] = acc.astype(BF16)
    for c in range(wf_ref.shape[1] // gw):
        pf_ref[:, c * gw:(c + 1) * gw] = jnp.dot(x, wf_ref[:, c * gw:(c + 1) * gw],
                                                preferred_element_type=F32)
    nt = (((1,), (1,)), ((), ()))
    qk = lax.dot_general(wqkt_ref[...], x, nt, preferred_element_type=F32)
    qt_ref[...] = (qk[0:gw] * q_scale).astype(BF16)
    kt_ref[...] = qk[gw:2 * gw].astype(BF16)
    st_ref[...] = lax.dot_general(wst_ref[...], x, nt, preferred_element_type=F32)


def _inproj(h, norm_w, wb, wf, wqkt, wst):
    n, d = h.shape
    nb, nf = wb.shape[1], wf.shape[1]
    gw = GROUP_WIDTH
    lanes_out = pl.BlockSpec((gw, TM_FFN), lambda i: (0, i))
    return pl.pallas_call(
        functools.partial(_inproj_body, q_scale=HEAD_DIM ** -0.5 * LOG2E),
        grid=(n // TM_FFN,),
        in_specs=[
            pl.BlockSpec((TM_FFN, d), lambda i: (i, 0)),
            _resident((1, d)),
            _resident((d, nb)),
            _resident((d, nf)),
            _resident((2 * gw, d)),
            _resident((2 * N_HEADS, d)),
        ],
        out_specs=[
            pl.BlockSpec((TM_FFN, nb), lambda i: (i, 0)),
            pl.BlockSpec((TM_FFN, nf), lambda i: (i, 0)),
            lanes_out, lanes_out,
            pl.BlockSpec((2 * N_HEADS, TM_FFN), lambda i: (0, i)),
        ],
        out_shape=[
            jax.ShapeDtypeStruct((n, nb), BF16),
            jax.ShapeDtypeStruct((n, nf), F32),
            jax.ShapeDtypeStruct((gw, n), BF16),
            jax.ShapeDtypeStruct((gw, n), BF16),
            jax.ShapeDtypeStruct((2 * N_HEADS, n), F32),
        ],
        compiler_params=_cparams(("parallel",), 48),
        name="inproj",
    )(h, norm_w.reshape(1, d), wb, wf, wqkt, wst)


def _gates_body(st_ref, bias_ref, alog_ref, g_ref, dt_ref, carry_ref):
    @pl.when(pl.program_id(1) == 0)
    def _():
        carry_ref[...] = jnp.zeros_like(carry_ref)

    x = st_ref[...] + bias_ref[...]
    tl = x.shape[1]
    row = lax.broadcasted_iota(jnp.int32, x.shape, 0)
    lane = lax.broadcasted_iota(jnp.int32, x.shape, 1)
    lane_in_chunk = lane % C_SSD
    is_fox = row < N_HEADS
    dt = jax.nn.softplus(x)
    val = jnp.where(is_fox, jax.nn.log_sigmoid(x), dt * (-jnp.exp(alog_ref[...])))
    full, local = val, val
    shift = 1
    while shift < tl:
        full = full + jnp.where(lane >= shift, pltpu.roll(full, shift, 1), 0.0)
        if shift < C_SSD:
            local = local + jnp.where(lane_in_chunk >= shift, pltpu.roll(local, shift, 1), 0.0)
        shift *= 2
    cum = full + carry_ref[...]
    carry_ref[...] = jnp.sum(jnp.where(lane == tl - 1, cum, 0.0), axis=1, keepdims=True)
    g_ref[...] = jnp.where(is_fox, cum * LOG2E, local)
    dt_ref[...] = dt


def _gates(st, bias, alog, batch):
    rows, n = st.shape
    s = n // batch
    nt = s // TL_GATES
    spec = pl.BlockSpec((rows, TL_GATES), lambda b, t: (0, b * nt + t))
    col = pl.BlockSpec((rows, 1), lambda b, t: (0, 0))
    return pl.pallas_call(
        _gates_body,
        grid=(batch, nt),
        in_specs=[spec, col, col],
        out_specs=[spec] * 2,
        out_shape=[jax.ShapeDtypeStruct((rows, n), F32)] * 2,
        scratch_shapes=[pltpu.VMEM((rows, 1), F32)],
        compiler_params=_cparams(("parallel", "arbitrary"), 16),
        name="gates",
    )(st, bias, alog)


def _rope_body(inv_ref, cos_ref, sin_ref):
    rows = cos_ref.shape[0]
    pos = (lax.broadcasted_iota(jnp.int32, cos_ref.shape, 0) + pl.program_id(0) * rows).astype(F32)
    lane = lax.broadcasted_iota(jnp.int32, cos_ref.shape, 1)
    ang = pos * inv_ref[...]
    cos_ref[...] = jnp.cos(ang)
    sin = jnp.sin(ang)
    sin_ref[...] = jnp.where(lane % HEAD_DIM < HEAD_DIM // 2, -sin, sin)


def _rope_tables(s):
    half = HEAD_DIM // 2
    inv_freq = ROPE_BASE ** (-jnp.arange(half, dtype=F32) / half)
    inv = jnp.tile(inv_freq, V7X_LANES // half).reshape(1, V7X_LANES)
    rows = 1024
    return pl.pallas_call(
        _rope_body,
        grid=(s // rows,),
        in_specs=[pl.BlockSpec((1, V7X_LANES), lambda i: (0, 0))],
        out_specs=[pl.BlockSpec((rows, V7X_LANES), lambda i: (i, 0))] * 2,
        out_shape=[jax.ShapeDtypeStruct((s, V7X_LANES), F32)] * 2,
        compiler_params=_cparams(("parallel",), 16),
        name="rope_tables",
    )(inv)


FOX_NEG = -1e30


FOX_SKIP_LOG2 = 160.0
FOX_FAST_LOG2 = 80.0
FOX_BOUND_SLACK = 1.01


def _fox_plan_body(qt_ref, kt_ref, c_ref, plan_ref, *, tq):
    gw, s = qt_ref.shape
    nblk = s // tq
    lanes = plan_ref.shape[-1]

    def head_norm2(x_ref):
        x = x_ref[...].astype(F32)
        x = x * x
        return jnp.concatenate([jnp.sum(x[h * HEAD_DIM:(h + 1) * HEAD_DIM], axis=0, keepdims=True)
                                for h in range(N_HEADS)], axis=0)

    qn2 = head_norm2(qt_ref)
    kmax2 = jnp.max(head_norm2(kt_ref), axis=1, keepdims=True)
    c = c_ref[0:N_HEADS, :]
    lane = lax.broadcasted_iota(jnp.int32, (N_HEADS, lanes), 1)
    qmax2 = jnp.zeros((N_HEADS, lanes), F32)
    cmax = jnp.zeros((N_HEADS, lanes), F32)
    cmin = jnp.full((N_HEADS, lanes), -jnp.inf, F32)
    for j in range(nblk):
        blk = slice(j * tq, (j + 1) * tq)
        qmax2 = jnp.where(lane == j, jnp.max(qn2[:, blk], axis=1, keepdims=True), qmax2)
        cmax = jnp.where(lane == j, jnp.max(c[:, blk], axis=1, keepdims=True), cmax)
        cmin = jnp.where(lane == j, jnp.min(c[:, blk], axis=1, keepdims=True), cmin)
    shift = 1
    while shift < nblk:
        cmin = jnp.minimum(cmin, jnp.where(lane >= shift, pltpu.roll(cmin, shift, 1), jnp.inf))
        shift *= 2
    spread = 2.0 * FOX_BOUND_SLACK * jnp.sqrt(qmax2 * kmax2)
    limit = cmax + spread + FOX_SKIP_LOG2
    skip = jnp.zeros((N_HEADS, lanes), F32)
    for i in range(nblk):
        lim_i = jnp.sum(jnp.where(lane == i, limit, 0.0), axis=1, keepdims=True)
        n_i = jnp.sum(jnp.where(cmin > lim_i, 1.0, 0.0), axis=1, keepdims=True)
        skip = jnp.where(lane == i, n_i, skip)
    fast = jnp.where(spread <= FOX_FAST_LOG2, 1.0, 0.0)
    rows = []
    for p in range(N_HEADS // 2):
        rows.append(jnp.minimum(skip[2 * p:2 * p + 1], skip[2 * p + 1:2 * p + 2]))
    for p in range(N_HEADS // 2):
        rows.append(jnp.minimum(fast[2 * p:2 * p + 1], fast[2 * p + 1:2 * p + 2]))
    rows.append(jnp.zeros((V7X_SUBLANES - len(rows), lanes), F32))
    plan_ref[...] = jnp.concatenate(rows, axis=0).astype(jnp.int32)


def _fox_plan(qt, kt, g_row, batch):
    gw, n = qt.shape
    s = n // batch
    lanes = V7X_LANES
    assert s // TQ_FOX <= lanes
    seq = pl.BlockSpec((gw, s), lambda b: (0, b))
    return pl.pallas_call(
        functools.partial(_fox_plan_body, tq=TQ_FOX),
        grid=(batch,),
        in_specs=[seq, seq, pl.BlockSpec((g_row.shape[0], s), lambda b: (0, b))],
        out_specs=pl.BlockSpec((None, V7X_SUBLANES, lanes), lambda b: (b, 0, 0)),
        out_shape=jax.ShapeDtypeStruct((batch, V7X_SUBLANES, lanes), jnp.int32),
        compiler_params=_cparams(("parallel",), 56),
        name="fox_plan",
    )(qt, kt, g_row)


def _fox_body(skip_ref, fast_ref, q_ref, kt_ref, v_ref, c_ref, o_ref,
              ref_ref, l_ref, acc_ref, m_ref, ls_ref):
    b, pair, i = pl.program_id(0), pl.program_id(1), pl.program_id(2)
    tq, w = q_ref.shape
    tk = 2 * tq
    idx = (b * pl.num_programs(1) + pair) * pl.num_programs(2) + i
    skip = skip_ref[idx]
    q = q_ref[...]
    first = lax.broadcasted_iota(jnp.int32, (tq, w), 1) < HEAD_DIM
    zero = jnp.zeros_like(q)
    qs = jnp.concatenate([jnp.where(first, q, zero), jnp.where(first, zero, q)], axis=0)

    def scores(start, size):
        s = jnp.dot(qs, kt_ref[:, pl.ds(start, size)], preferred_element_type=F32)
        return [s[hh * tq:(hh + 1) * tq] - c_ref[hh:hh + 1, pl.ds(start, size)] for hh in range(2)]

    def weighted_values(start, size, p):
        stacked = jnp.concatenate(p, axis=0).astype(BF16)
        return jnp.dot(stacked, v_ref[pl.ds(start, size), :], preferred_element_type=F32)

    def earlier_blocks(block):
        def body(j, carry):
            block(pl.multiple_of(j * tk, tk), tk)
            return carry

        lax.fori_loop(skip // 2, i // 2, body, 0)

        @pl.when(jnp.logical_and(i % 2 == 1, skip < i))
        def _():
            block(pl.multiple_of((i - 1) * tq, tq), tq)

    diag_start = pl.multiple_of(i * tq, tq)
    qry = lax.broadcasted_iota(jnp.int32, (tq, tq), 0)
    key = lax.broadcasted_iota(jnp.int32, (tq, tq), 1)

    @pl.when(fast_ref[idx] == 1)
    def _():
        p = []
        for hh, s in enumerate(scores(diag_start, tq)):
            ref = jnp.sum(jnp.where(key == qry, s, 0.0), axis=1, keepdims=True)
            ref_ref[hh] = jnp.broadcast_to(ref, (tq, w))
            p.append(jnp.exp2(jnp.where(key <= qry, s - ref, FOX_NEG)))
            l_ref[hh] = sum(p[hh][:, u * w:(u + 1) * w] for u in range(tq // w))
        acc_ref[...] = weighted_values(diag_start, tq, p)

        def block(start, size):
            p = []
            for hh, s in enumerate(scores(start, size)):
                ref = ref_ref[hh]
                tiles = [jnp.exp2(s[:, u * w:(u + 1) * w] - ref) for u in range(size // w)]
                l_ref[hh] = l_ref[hh] + sum(tiles)
                p.append(jnp.concatenate(tiles, axis=1))
            acc_ref[...] = acc_ref[...] + weighted_values(start, size, p)

        earlier_blocks(block)

    @pl.when(fast_ref[idx] != 1)
    def _():
        m_ref[...] = jnp.full(m_ref.shape, FOX_NEG, F32)
        ls_ref[...] = jnp.zeros_like(ls_ref)
        acc_ref[...] = jnp.zeros_like(acc_ref)

        def update(start, size, s_heads):
            p, alpha = [], []
            for hh, s in enumerate(s_heads):
                m_old = m_ref[hh]
                m_new = jnp.maximum(m_old, jnp.max(s, axis=1, keepdims=True))
                alpha.append(jnp.exp2(m_old - m_new))
                p.append(jnp.exp2(s - m_new))
                ls_ref[hh] = alpha[hh] * ls_ref[hh] + jnp.sum(p[hh], axis=1, keepdims=True)
                m_ref[hh] = m_new
            acc_ref[...] = (jnp.concatenate(alpha, axis=0) * acc_ref[...]
                            + weighted_values(start, size, p))

        earlier_blocks(lambda start, size: update(start, size, scores(start, size)))
        update(diag_start, tq, [jnp.where(key <= qry, s, FOX_NEG) for s in scores(diag_start, tq)])
        lane0 = lax.broadcasted_iota(jnp.int32, (tq, w), 1) == 0
        for hh in range(2):
            l_ref[hh] = jnp.where(lane0, ls_ref[hh], 0.0)

    l = [jnp.sum(l_ref[hh], axis=1, keepdims=True) for hh in range(2)]
    acc = acc_ref[...]
    o_ref[...] = jnp.where(first, acc[0:tq] / l[0], acc[tq:2 * tq] / l[1]).astype(o_ref.dtype)


def _fox(pb, kt, c_pairs, plan, batch, cols):
    n = pb.shape[0]
    s = n // batch
    nq = s // TQ_FOX
    w = V7X_LANES
    pairs = GROUP_WIDTH // w
    skip = plan[:, 0:pairs, 0:nq].reshape(-1)
    fast = plan[:, pairs:2 * pairs, 0:nq].reshape(-1)
    fq, fv = cols["fq"] * pairs, cols["fv"] * pairs
    return pl.pallas_call(
        _fox_body,
        grid_spec=pltpu.PrefetchScalarGridSpec(
            num_scalar_prefetch=2,
            grid=(batch, pairs, nq),
            in_specs=[
                pl.BlockSpec((TQ_FOX, w), lambda b, p, i, *_: (b * nq + i, fq + p)),
                pl.BlockSpec((w, s), lambda b, p, i, *_: (p, b)),
                pl.BlockSpec((s, w), lambda b, p, i, *_: (b, fv + p)),
                pl.BlockSpec((None, 2, s), lambda b, p, i, *_: (p, 0, b)),
            ],
            out_specs=pl.BlockSpec((TQ_FOX, w), lambda b, p, i, *_: (b * nq + i, p)),
            scratch_shapes=[
                pltpu.VMEM((2, TQ_FOX, w), F32),
                pltpu.VMEM((2, TQ_FOX, w), F32),
                pltpu.VMEM((2 * TQ_FOX, w), F32),
                pltpu.VMEM((2, TQ_FOX, 1), F32),
                pltpu.VMEM((2, TQ_FOX, 1), F32),
            ],
        ),
        out_shape=jax.ShapeDtypeStruct((n, GROUP_WIDTH), BF16),
        compiler_params=_cparams(("parallel", "parallel", "arbitrary"), 48),
        name="fox",
    )(skip, fast, pb, kt, pb, c_pairs)


def _head_of_lane(shape, axis):
    return lax.broadcasted_iota(jnp.int32, shape, axis) // HEAD_DIM


def _same_head(shape):
    return _head_of_lane(shape, 0) == _head_of_lane(shape, 1)


def _per_head(values, lane_head):
    out = values[N_HEADS - 1]
    for h in range(N_HEADS - 2, -1, -1):
        out = jnp.where(lane_head == h, values[h], out)
    return out


def _head_rmsnorm(o, w, ones_bd):
    ms = jnp.dot((o * o).astype(BF16), ones_bd, preferred_element_type=F32) * (1.0 / HEAD_DIM)
    return o * lax.rsqrt(ms + RMS_EPS) * w


def _ret_body(rq_ref, rk_ref, rv_ref, rg_ref, cos_ref, sin_ref, nw_ref, o_ref,
              state_ref, dmat_ref, tab_ref, bd_ref):
    c = C_RET
    gw = GROUP_WIDTH
    log_gamma = [math.log1p(-(2.0 ** (-5.0 - h))) for h in range(N_HEADS)]

    @pl.when(pl.program_id(1) == 0)
    def _():
        state_ref[...] = jnp.zeros_like(state_ref)
        r = lax.broadcasted_iota(jnp.int32, (c, c), 0)
        s = lax.broadcasted_iota(jnp.int32, (c, c), 1)
        dist = (r - s).astype(F32)
        for h in range(N_HEADS):
            dmat_ref[h] = jnp.where(s <= r, jnp.exp(dist * log_gamma[h]), 0.0)
        lg = _per_head(log_gamma, _head_of_lane((c, gw), 1))
        t = lax.broadcasted_iota(jnp.int32, (c, gw), 0).astype(F32)
        tab_ref[0] = jnp.exp((t + 1.0) * lg)
        tab_ref[1] = jnp.exp((c - 1.0 - t) * lg)
        tab_ref[2] = jnp.exp(c * lg)
        bd_ref[...] = jnp.where(_same_head((gw, gw)), 1.0, 0.0).astype(BF16)

    lane = lax.broadcasted_iota(jnp.int32, (c, gw), 1)
    lane_head = lane // HEAD_DIM
    first_half = lane % HEAD_DIM < HEAD_DIM // 2

    for ci in range(rq_ref.shape[0] // c):
        rows = slice(ci * c, (ci + 1) * c)
        cosx = jnp.concatenate([cos_ref[rows, :]] * (gw // V7X_LANES), axis=1)
        sinx = jnp.concatenate([sin_ref[rows, :]] * (gw // V7X_LANES), axis=1)

        def rotary(x):
            swapped = jnp.where(first_half, pltpu.roll(x, gw - HEAD_DIM // 2, 1),
                                pltpu.roll(x, HEAD_DIM // 2, 1))
            return x * cosx + swapped * sinx

        q = rotary(rq_ref[rows, :])
        k = rotary(rk_ref[rows, :]) * (HEAD_DIM ** -0.5)
        v = rv_ref[rows, :]
        qb, kb = q.astype(BF16), k.astype(BF16)
        zero = jnp.zeros_like(qb)
        o = jnp.zeros((c, gw), F32)
        for h in range(N_HEADS):
            sel = lane_head == h
            sc = lax.dot_general(jnp.where(sel, qb, zero), kb, (((1,), (1,)), ((), ())),
                                 preferred_element_type=F32)
            p = (sc * dmat_ref[h]).astype(BF16)
            o = o + jnp.where(sel, jnp.dot(p, v, preferred_element_type=F32), 0.0)
        st = state_ref[...]
        o = o + lax.dot_general(qb, st.astype(BF16), (((1,), (1,)), ((), ())),
                                preferred_element_type=F32) * tab_ref[0]
        kd = (k * tab_ref[1]).astype(BF16)
        vt = v.astype(F32).T.astype(BF16)
        upd = jnp.dot(vt, kd, preferred_element_type=F32)
        state_ref[...] = st * tab_ref[2, 0:1, :] + jnp.where(_same_head((gw, gw)), upd, 0.0)
        y = _head_rmsnorm(o, nw_ref[...], bd_ref[...]) * _silu(rg_ref[rows, :])
        o_ref[rows, :] = y.astype(o_ref.dtype)


def _retention(pf, pb, cos_t, sin_t, norm_w, batch, cols):
    n = pf.shape[0]
    s = n // batch
    nt = s // T_MIX
    gw = GROUP_WIDTH

    def tok(col):
        return pl.BlockSpec((T_MIX, gw), lambda b, t: (b * nt + t, col))

    tab = pl.BlockSpec((T_MIX, V7X_LANES), lambda b, t: (t, 0))
    return pl.pallas_call(
        _ret_body,
        grid=(batch, nt),
        in_specs=[tok(cols["rq"]), tok(cols["rk"]), tok(cols["rv"]), tok(cols["rg"]), tab, tab,
                  pl.BlockSpec((1, gw), lambda b, t: (0, 0))],
        out_specs=pl.BlockSpec((T_MIX, gw), lambda b, t: (b * nt + t, 0)),
        out_shape=jax.ShapeDtypeStruct((n, gw), BF16),
        scratch_shapes=[
            pltpu.VMEM((gw, gw), F32),
            pltpu.VMEM((N_HEADS, C_RET, C_RET), F32),
            pltpu.VMEM((3, C_RET, gw), F32),
            pltpu.VMEM((gw, gw), BF16),
        ],
        compiler_params=_cparams(("parallel", "arbitrary"), 24),
        name="retention",
    )(pf, pf, pb, pf, cos_t, sin_t, norm_w.reshape(1, gw))


def _ssd_body(xbc_ref, z_ref, gcol_ref, grow_ref, cw_ref, cb_ref, d_ref, nw_ref, o_ref,
              xpad_ref, state_ref):
    tile = xbc_ref.shape[0]
    c = C_SSD
    gw = GROUP_WIDTH
    pad = V7X_SUBLANES

    @pl.when(pl.program_id(1) == 0)
    def _():
        state_ref[...] = jnp.zeros_like(state_ref)
        xpad_ref[0:pad, :] = jnp.zeros((pad, xpad_ref.shape[1]), F32)

    xpad_ref[pad:pad + tile, :] = xbc_ref[...]
    conv = cb_ref[...]
    for j in range(CONV_WIDTH):
        off = pad - (CONV_WIDTH - 1) + j
        conv = conv + cw_ref[j:j + 1, :] * xpad_ref[off:off + tile, :]
    xpad_ref[0:pad, :] = xpad_ref[tile:tile + pad, :]
    xbc_all = _silu(conv)

    lane_head = _head_of_lane((c, gw), 1)
    r = lax.broadcasted_iota(jnp.int32, (c, c), 0)
    s = lax.broadcasted_iota(jnp.int32, (c, c), 1)
    for ci in range(tile // c):
        rows = slice(ci * c, (ci + 1) * c)
        xbc = xbc_all[rows]
        xs = xbc[:, 0:gw]
        bm = xbc[:, gw:gw + SSD_STATE]
        cm = xbc[:, gw + SSD_STATE:gw + 2 * SSD_STATE]
        gcol = gcol_ref[rows, :]
        dtx = _per_head([gcol[:, h:h + 1] for h in range(N_HEADS)], lane_head)
        bx = _per_head([gcol[:, N_HEADS + h:N_HEADS + h + 1] for h in range(N_HEADS)], lane_head)
        blast = bx[c - 1:c, :]
        v = xs * dtx
        vb = v.astype(BF16)

        sc = lax.dot_general(cm.astype(BF16), bm.astype(BF16), (((1,), (1,)), ((), ())),
                             preferred_element_type=F32)
        o = jnp.zeros((c, gw), F32)
        for h in range(N_HEADS):
            bcol = gcol[:, N_HEADS + h:N_HEADS + h + 1]
            brow = grow_ref[N_HEADS + h:N_HEADS + h + 1, rows]
            decay = jnp.where(s <= r, jnp.exp(jnp.minimum(bcol - brow, 0.0)), 0.0)
            p = (sc * decay).astype(BF16)
            o = o + jnp.where(lane_head == h, jnp.dot(p, vb, preferred_element_type=F32), 0.0)
        st = state_ref[...]
        o = o + jnp.dot(cm.astype(BF16), st.astype(BF16), preferred_element_type=F32) * jnp.exp(bx)
        vd = (v * jnp.exp(blast - bx)).astype(BF16)
        state_ref[...] = (st * jnp.exp(blast)
                          + jnp.dot(bm.T.astype(BF16), vd, preferred_element_type=F32))
        o = o + d_ref[...] * xs
        y = _rmsnorm(o * _silu(z_ref[rows, :]), nw_ref[...])
        o_ref[rows, :] = y.astype(o_ref.dtype)


def _ssd(pf, gcol, grow, conv_w, conv_b, d_x, norm_w, batch, cols):
    n = pf.shape[0]
    s = n // batch
    nt = s // T_MIX
    gw = GROUP_WIDTH
    xbc_w = gw + 2 * SSD_STATE
    const = lambda shape: pl.BlockSpec(shape, lambda b, t: (0, 0))
    return pl.pallas_call(
        _ssd_body,
        grid=(batch, nt),
        in_specs=[
            pl.BlockSpec((T_MIX, xbc_w), lambda b, t: (b * nt + t, cols["sxbc"])),
            pl.BlockSpec((T_MIX, gw), lambda b, t: (b * nt + t, cols["sz"])),
            pl.BlockSpec((T_MIX, 2 * N_HEADS), lambda b, t: (b * nt + t, 0)),
            pl.BlockSpec((2 * N_HEADS, T_MIX), lambda b, t: (0, b * nt + t)),
            const((CONV_WIDTH, xbc_w)), const((1, xbc_w)), const((1, gw)), const((1, gw)),
        ],
        out_specs=pl.BlockSpec((T_MIX, gw), lambda b, t: (b * nt + t, 0)),
        out_shape=jax.ShapeDtypeStruct((n, gw), BF16),
        scratch_shapes=[
            pltpu.VMEM((T_MIX + 2 * V7X_SUBLANES, xbc_w), F32),
            pltpu.VMEM((SSD_STATE, gw), F32),
        ],
        compiler_params=_cparams(("parallel", "arbitrary"), 24),
        name="ssd",
    )(pf, pf, gcol, grow, conv_w, conv_b.reshape(1, xbc_w), d_x, norm_w.reshape(1, gw))


def _hgrn_w_rows():
    c, sub = C_HGRN, V7X_SUBLANES
    return sum(c if sub * a < c // 2 else c // 2 for a in range(c // sub))


def _hgrn_body(hq_ref, hf_ref, hi_ref, hg_ref, lbw_ref, nw_ref, o_ref,
               state_ref, pad_ref, sh_ref, w_ref, bd_ref, *, layer):
    c, sub = C_HGRN, V7X_SUBLANES
    half = c // 2
    gw = GROUP_WIDTH

    @pl.when(pl.program_id(1) == 0)
    def _():
        state_ref[...] = jnp.zeros_like(state_ref)
        pad_ref[:, 0:c, :] = jnp.zeros((3, c, gw), F32)
        bd_ref[...] = jnp.where(_same_head((gw, gw)), 1.0, 0.0).astype(BF16)

    lbw = lbw_ref[...]
    e = jnp.exp(lbw - jnp.max(lbw, axis=0, keepdims=True))
    sm = e / jnp.sum(e, axis=0, keepdims=True)
    cs = sm[0:1, :]
    first = cs
    for d in range(1, layer + 1):
        cs = cs + sm[d:d + 1, :]
    lb = cs - first
    log_lb, log_1mlb = jnp.log(jnp.maximum(lb, 0.0)), jnp.log1p(-lb)
    bd = bd_ref[...]
    row = lax.broadcasted_iota(jnp.int32, (c, gw), 0)

    for ci in range(hq_ref.shape[0] // c):
        rows = slice(ci * c, (ci + 1) * c)
        log_f = jnp.logaddexp(log_lb, log_1mlb + jax.nn.log_sigmoid(hf_ref[rows, :]))
        k = 1.0 - jnp.exp(log_f)
        b = log_f
        shift = 1
        while shift < c:
            b = b + jnp.where(row >= shift, pltpu.roll(b, shift, 0), 0.0)
            shift *= 2
        b2 = b * LOG2E
        q = hq_ref[rows, :]
        v = hi_ref[rows, :].astype(F32)
        pad_ref[0, c:2 * c, :] = k
        pad_ref[1, c:2 * c, :] = b2
        pad_ref[2, c:2 * c, :] = v

        o_top = jnp.zeros((half, gw), F32)
        o_bot = jnp.zeros((half, gw), F32)
        for r in range(sub):
            if r == 0:
                src = pad_ref
            else:
                for x in range(3):
                    sh_ref[x] = pltpu.roll(pad_ref[x], r, 0)
                src = sh_ref
            spans = []
            pos = 0
            for a in range(c // sub):
                lo = c - sub * a
                r0 = 0 if sub * a < half else half
                ks = src[0, lo + r0:lo + c, :]
                bs = src[1, lo + r0:lo + c, :]
                w = q[r0:] * ks * jnp.exp2(b2[r0:] - bs)
                w_ref[pos:pos + c - r0, :] = w.astype(BF16)
                spans.append((pos, r0, lo))
                pos += c - r0
            seg = jnp.dot(w_ref[...], bd, preferred_element_type=F32)
            for pos, r0, lo in spans:
                term = seg[pos:pos + c - r0] * src[2, lo + r0:lo + c, :]
                if r0 == 0:
                    o_top = o_top + term[:half]
                    o_bot = o_bot + term[half:]
                else:
                    o_bot = o_bot + term
        o = jnp.concatenate([o_top, o_bot], axis=0)
        st = state_ref[...]
        o = o + lax.dot_general((q * jnp.exp2(b2)).astype(BF16), st.astype(BF16),
                                (((1,), (1,)), ((), ())), preferred_element_type=F32)
        blast = b2[c - 1:c, :]
        kd = (k * jnp.exp2(blast - b2)).astype(BF16)
        upd = jnp.dot(v.T.astype(BF16), kd, preferred_element_type=F32)
        state_ref[...] = st * jnp.exp2(blast) + jnp.where(_same_head((gw, gw)), upd, 0.0)
        y = _head_rmsnorm(o, nw_ref[...], bd) * _silu(hg_ref[rows, :])
        o_ref[rows, :] = y.astype(o_ref.dtype)


def _hgrn(pf, pb, lower_bounds, norm_w, layer, batch, cols):
    n = pf.shape[0]
    s = n // batch
    nt = s // T_HGRN
    gw = GROUP_WIDTH
    depth = lower_bounds.shape[0]

    def tok(col):
        return pl.BlockSpec((T_HGRN, gw), lambda b, t: (b * nt + t, col))

    return pl.pallas_call(
        functools.partial(_hgrn_body, layer=layer),
        grid=(batch, nt),
        in_specs=[tok(cols["hq"]), tok(cols["hf"]), tok(cols["hi"]), tok(cols["hg"]),
                  pl.BlockSpec((depth, gw), lambda b, t: (0, 0)),
                  pl.BlockSpec((1, gw), lambda b, t: (0, 0))],
        out_specs=pl.BlockSpec((T_HGRN, gw), lambda b, t: (b * nt + t, 0)),
        out_shape=jax.ShapeDtypeStruct((n, gw), BF16),
        scratch_shapes=[
            pltpu.VMEM((gw, gw), F32),
            pltpu.VMEM((3, 2 * C_HGRN, gw), F32),
            pltpu.VMEM((3, 2 * C_HGRN, gw), F32),
            pltpu.VMEM((_hgrn_w_rows(), gw), BF16),
            pltpu.VMEM((gw, gw), BF16),
        ],
        compiler_params=_cparams(("parallel", "arbitrary"), 24),
        name="hgrn2",
    )(pf, pf, pb, pf, lower_bounds, norm_w.reshape(1, gw))


def _post_body(h_ref, y0_ref, y1_ref, y2_ref, y3_ref, p_ref, wo_ref, fnw_ref, wup_ref, wdn_ref,
               pnw_ref, wg_ref, wp_ref, fw_ref, o_ref, act_ref, *, final):
    gw = GROUP_WIDTH
    h = h_ref[...]
    for m, y_ref in enumerate((y0_ref, y1_ref, y2_ref, y3_ref)):
        h = h + jnp.dot(y_ref[...], wo_ref[m * gw:(m + 1) * gw, :], preferred_element_type=F32)
    h = _ffn_math(h, fnw_ref, wup_ref, wdn_ref, act_ref)
    x = _rmsnorm(h, pnw_ref[...]).astype(BF16)
    gate = jax.nn.sigmoid(jnp.dot(x, wg_ref[...], preferred_element_type=F32))
    emb = jnp.dot(p_ref[...].astype(BF16), wp_ref[...], preferred_element_type=F32)
    h = h + gate * emb
    if final:
        h = _rmsnorm(h, fw_ref[...])
    o_ref[...] = h


def _post(h, ys, p, w_out, ffn_norm, w_up, w_down, ple_norm, w_gate, w_proj, final_w, final):
    n, d = h.shape
    gw = GROUP_WIDTH
    pd = p.shape[1]
    d_ff = w_down.shape[0]
    tok = pl.BlockSpec((TM_FFN, d), lambda i: (i, 0))
    ytok = pl.BlockSpec((TM_FFN, gw), lambda i: (i, 0))
    return pl.pallas_call(
        functools.partial(_post_body, final=final),
        grid=(n // TM_FFN,),
        in_specs=[tok, ytok, ytok, ytok, ytok, pl.BlockSpec((TM_FFN, pd), lambda i: (i, 0)),
                  _resident(w_out.shape), _resident((1, d)), _resident((d, 2 * d_ff)),
                  _resident((d_ff, d)), _resident((1, d)), _resident((d, d)), _resident((pd, d)),
                  _resident((1, d))],
        out_specs=tok,
        out_shape=jax.ShapeDtypeStruct((n, d), F32),
        scratch_shapes=[pltpu.VMEM((TM_FFN, d_ff), BF16)],
        compiler_params=_cparams(("parallel",), 56),
        name="post",
    )(h, *ys, p, w_out, ffn_norm.reshape(1, d), w_up, w_down, ple_norm.reshape(1, d),
      w_gate, w_proj, final_w.reshape(1, d))


def _split_w_in(w_in):
    gw = GROUP_WIDTH
    xbc_w = gw + 2 * SSD_STATE
    names = ["fq", "fk", "fv", "ff", "rq", "rk", "rv", "rg", "sz", "sxbc", "sdt", "hq", "hf", "hi", "hg"]
    sizes = [gw, gw, gw, N_HEADS, gw, gw, gw, gw, gw, xbc_w, N_HEADS, gw, gw, gw, gw]
    offs = dict(zip(names, np.concatenate([[0], np.cumsum(sizes)[:-1]]).tolist()))
    width = dict(zip(names, sizes))

    def gather(group):
        return jnp.concatenate([w_in[:, offs[k]:offs[k] + width[k]] for k in group], axis=1)

    b_group = ["fq", "fv", "rv", "hi"]
    f_group = ["rq", "rk", "rg", "sz", "sxbc", "hq", "hf", "hg"]
    cols = {}
    for group in (b_group, f_group):
        pos = 0
        for k in group:
            cols[k] = pos // (xbc_w if k == "sxbc" else gw)
            pos += width[k]
    wb = gather(b_group).astype(BF16)
    wf = gather(f_group).astype(BF16)
    wqkt = gather(["fq", "fk"]).T.astype(BF16)
    wst = gather(["ff", "sdt"]).T.astype(BF16)
    return wb, wf, wqkt, wst, cols


def kernel(x, p, ffn1_norm, ffn1_w_up, ffn1_w_down, mix_norm, w_in, fox_f_bias, ret_norm, conv_w, conv_b, dt_bias, a_log, ssd_d, ssd_norm, hgrn_lower_bounds, hgrn_norm, w_out, ffn2_norm, ffn2_w_up, ffn2_w_down, ple_norm, ple_w_gate, ple_w_proj, final_norm):
    batch, s, d = x.shape
    depth = p.shape[0]
    n = batch * s
    h = x.reshape(n, d)
    cos_t, sin_t = _rope_tables(s)
    zeros4 = jnp.zeros((N_HEADS,), F32)
    for i in range(depth):
        h = _ffn(h, ffn1_norm[i], ffn1_w_up[i].astype(BF16), ffn1_w_down[i].astype(BF16))

        wb, wf, wqkt, wst, cols = _split_w_in(w_in[i])
        pb, pf, qt, kt, st = _inproj(h, mix_norm[i], wb, wf, wqkt, wst)
        bias = jnp.concatenate([fox_f_bias[i], dt_bias[i]]).reshape(2 * N_HEADS, 1)
        alog = jnp.concatenate([zeros4, a_log[i]]).reshape(2 * N_HEADS, 1)
        g_row, dt_row = _gates(st, bias, alog, batch)
        c_pairs = g_row[:N_HEADS].reshape(N_HEADS // 2, 2, n)
        gcol = jnp.concatenate([dt_row[N_HEADS:], g_row[N_HEADS:]], axis=0).T
        d_x = jnp.repeat(ssd_d[i], HEAD_DIM).reshape(1, GROUP_WIDTH)

        y_fox = _fox(pb, kt, c_pairs, _fox_plan(qt, kt, g_row, batch), batch, cols)
        y_ret = _retention(pf, pb, cos_t, sin_t, ret_norm[i], batch, cols)
        y_ssd = _ssd(pf, gcol, g_row, conv_w[i], conv_b[i], d_x, ssd_norm[i], batch, cols)
        y_hg = _hgrn(pf, pb, hgrn_lower_bounds, hgrn_norm[i], i, batch, cols)
        h = _post(h, (y_fox, y_ret, y_ssd, y_hg), p[i].reshape(n, -1), w_out[i].astype(BF16),
                  ffn2_norm[i], ffn2_w_up[i].astype(BF16), ffn2_w_down[i].astype(BF16),
                  ple_norm[i], ple_w_gate[i].astype(BF16), ple_w_proj[i].astype(BF16),
                  final_norm, final=(i == depth - 1))
    return h.reshape(batch, s, d)
```

```python
import functools
import math

import jax
import jax.numpy as jnp
import numpy as np
from jax import lax
from jax.experimental import pallas as pl
from jax.experimental.pallas import tpu as pltpu

F32 = jnp.float32
BF16 = jnp.bfloat16

HEAD_DIM = 64
N_HEADS = 4
GROUP_WIDTH = N_HEADS * HEAD_DIM
SSD_STATE = 128
CONV_WIDTH = 4
RMS_EPS = 1e-6
ROPE_BASE = 10000.0
LOG2E = 1.4426950408889634

V7X_LANES = 128
V7X_SUBLANES = 8
V7X_VMEM_BYTES = 64 * 1024 * 1024
MIB = 1024 * 1024

TM_FFN = 512
FF_CHUNK = 256
TQ_FOX = 512
C_RET = 128
C_SSD = 128
C_HGRN = 32
T_MIX = 512
TL_GATES = 2048


def _cparams(semantics, vmem_mib):
    return pltpu.CompilerParams(dimension_semantics=semantics, vmem_limit_bytes=int(vmem_mib * MIB))


def _rmsnorm(x, w):
    return x * lax.rsqrt(jnp.mean(x * x, axis=-1, keepdims=True) + RMS_EPS) * w


def _silu(x):
    return x * jax.nn.sigmoid(x)


def _resident(shape):
    return pl.BlockSpec(shape, lambda *_: (0,) * len(shape), pipeline_mode=pl.Buffered(1))


def _ffn_math(h, nw_ref, wup_ref, wdn_ref, act_ref):
    d_ff = wdn_ref.shape[0]
    x = _rmsnorm(h, nw_ref[...]).astype(BF16)
    for c in range(d_ff // FF_CHUNK):
        lo = c * FF_CHUNK
        gate = jnp.dot(x, wup_ref[:, lo:lo + FF_CHUNK], preferred_element_type=F32)
        up = jnp.dot(x, wup_ref[:, d_ff + lo:d_ff + lo + FF_CHUNK], preferred_element_type=F32)
        act_ref[:, lo:lo + FF_CHUNK] = (_silu(gate) * up).astype(BF16)
    return h + 0.5 * jnp.dot(act_ref[...], wdn_ref[...], preferred_element_type=F32)


def _ffn_body(h_ref, nw_ref, wup_ref, wdn_ref, o_ref, act_ref):
    o_ref[...] = _ffn_math(h_ref[...], nw_ref, wup_ref, wdn_ref, act_ref)


def _ffn(h, norm_w, w_up, w_down):
    n, d = h.shape
    d_ff = w_down.shape[0]
    return pl.pallas_call(
        _ffn_body,
        grid=(n // TM_FFN,),
        in_specs=[
            pl.BlockSpec((TM_FFN, d), lambda i: (i, 0)),
            _resident((1, d)),
            _resident((d, 2 * d_ff)),
            _resident((d_ff, d)),
        ],
        out_specs=pl.BlockSpec((TM_FFN, d), lambda i: (i, 0)),
        out_shape=jax.ShapeDtypeStruct((n, d), F32),
        scratch_shapes=[pltpu.VMEM((TM_FFN, d_ff), BF16)],
        compiler_params=_cparams(("parallel",), 48),
        name="ffn",
    )(h, norm_w.reshape(1, d), w_up, w_down)


def _inproj_body(h_ref, nw_ref, wb_ref, wf_ref, wqkt_ref, wst_ref,
                 pb_ref, pf_ref, qt_ref, kt_ref, st_ref, *, q_scale):
    x = _rmsnorm(h_ref[...], nw_ref[...]).astype(BF16)
    gw = GROUP_WIDTH
    for c in range(wb_ref.shape[1] // gw):
        acc = jnp.dot(x, wb_ref[:, c * gw:(c + 1) * gw], preferred_element_type=F32)
        if c == 0:
            acc = acc * q_scale
        pb_ref[:, c * gw:(c + 1) * gw] = acc.astype(BF16)
    for c in range(wf_ref.shape[1] // gw):
        pf_ref[:, c * gw:(c + 1) * gw] = jnp.dot(x, wf_ref[:, c * gw:(c + 1) * gw],
                                                preferred_element_type=F32)
    nt = (((1,), (1,)), ((), ()))
    qk = lax.dot_general(wqkt_ref[...], x, nt, preferred_element_type=F32)
    qt_ref[...] = (qk[0:gw] * q_scale).astype(BF16)
    kt_ref[...] = qk[gw:2 * gw].astype(BF16)
    st_ref[...] = lax.dot_general(wst_ref[...], x, nt, preferred_element_type=F32)


def _inproj(h, norm_w, wb, wf, wqkt, wst):
    n, d = h.shape
    nb, nf = wb.shape[1], wf.shape[1]
    gw = GROUP_WIDTH
    lanes_out = pl.BlockSpec((gw, TM_FFN), lambda i: (0, i))
    return pl.pallas_call(
        functools.partial(_inproj_body, q_scale=HEAD_DIM ** -0.5 * LOG2E),
        grid=(n // TM_FFN,),
        in_specs=[
            pl.BlockSpec((TM_FFN, d), lambda i: (i, 0)),
            _resident((1, d)),
            _resident((d, nb)),
            _resident((d, nf)),
            _resident((2 * gw, d)),
            _resident((2 * N_HEADS, d)),
        ],
        out_specs=[
            pl.BlockSpec((TM_FFN, nb), lambda i: (i, 0)),
            pl.BlockSpec((TM_FFN, nf), lambda i: (i, 0)),
            lanes_out, lanes_out,
            pl.BlockSpec((2 * N_HEADS, TM_FFN), lambda i: (0, i)),
        ],
        out_shape=[
            jax.ShapeDtypeStruct((n, nb), BF16),
            jax.ShapeDtypeStruct((n, nf), F32),
            jax.ShapeDtypeStruct((gw, n), BF16),
            jax.ShapeDtypeStruct((gw, n), BF16),
            jax.ShapeDtypeStruct((2 * N_HEADS, n), F32),
        ],
        compiler_params=_cparams(("parallel",), 48),
        name="inproj",
    )(h, norm_w.reshape(1, d), wb, wf, wqkt, wst)


def _gates_body(st_ref, bias_ref, alog_ref, g_ref, dt_ref, carry_ref):
    @pl.when(pl.program_id(1) == 0)
    def _():
        carry_ref[...] = jnp.zeros_like(carry_ref)

    x = st_ref[...] + bias_ref[...]
    tl = x.shape[1]
    row = lax.broadcasted_iota(jnp.int32, x.shape, 0)
    lane = lax.broadcasted_iota(jnp.int32, x.shape, 1)
    lane_in_chunk = lane % C_SSD
    is_fox = row < N_HEADS
    dt = jax.nn.softplus(x)
    val = jnp.where(is_fox, jax.nn.log_sigmoid(x), dt * (-jnp.exp(alog_ref[...])))
    full, local = val, val
    shift = 1
    while shift < tl:
        full = full + jnp.where(lane >= shift, pltpu.roll(full, shift, 1), 0.0)
        if shift < C_SSD:
            local = local + jnp.where(lane_in_chunk >= shift, pltpu.roll(local, shift, 1), 0.0)
        shift *= 2
    cum = full + carry_ref[...]
    carry_ref[...] = jnp.sum(jnp.where(lane == tl - 1, cum, 0.0), axis=1, keepdims=True)
    g_ref[...] = jnp.where(is_fox, cum * LOG2E, local)
    dt_ref[...] = dt


def _gates(st, bias, alog, batch):
    rows, n = st.shape
    s = n // batch
    nt = s // TL_GATES
    spec = pl.BlockSpec((rows, TL_GATES), lambda b, t: (0, b * nt + t))
    col = pl.BlockSpec((rows, 1), lambda b, t: (0, 0))
    return pl.pallas_call(
        _gates_body,
        grid=(batch, nt),
        in_specs=[spec, col, col],
        out_specs=[spec] * 2,
        out_shape=[jax.ShapeDtypeStruct((rows, n), F32)] * 2,
        scratch_shapes=[pltpu.VMEM((rows, 1), F32)],
        compiler_params=_cparams(("parallel", "arbitrary"), 16),
        name="gates",
    )(st, bias, alog)


def _rope_body(inv_ref, cos_ref, sin_ref):
    rows = cos_ref.shape[0]
    pos = (lax.broadcasted_iota(jnp.int32, cos_ref.shape, 0) + pl.program_id(0) * rows).astype(F32)
    lane = lax.broadcasted_iota(jnp.int32, cos_ref.shape, 1)
    ang = pos * inv_ref[...]
    cos_ref[...] = jnp.cos(ang)
    sin = jnp.sin(ang)
    sin_ref[...] = jnp.where(lane % HEAD_DIM < HEAD_DIM // 2, -sin, sin)


def _rope_tables(s):
    half = HEAD_DIM // 2
    inv_freq = ROPE_BASE ** (-jnp.arange(half, dtype=F32) / half)
    inv = jnp.tile(inv_freq, V7X_LANES // half).reshape(1, V7X_LANES)
    rows = 1024
    return pl.pallas_call(
        _rope_body,
        grid=(s // rows,),
        in_specs=[pl.BlockSpec((1, V7X_LANES), lambda i: (0, 0))],
        out_specs=[pl.BlockSpec((rows, V7X_LANES), lambda i: (i, 0))] * 2,
        out_shape=[jax.ShapeDtypeStruct((s, V7X_LANES), F32)] * 2,
        compiler_params=_cparams(("parallel",), 16),
        name="rope_tables",
    )(inv)


FOX_NEG = -1e30


FOX_SKIP_LOG2 = 160.0
FOX_FAST_LOG2 = 80.0
FOX_BOUND_SLACK = 1.01


def _fox_plan_body(qt_ref, kt_ref, c_ref, plan_ref, *, tq):
    gw, s = qt_ref.shape
    nblk = s // tq
    lanes = plan_ref.shape[-1]

    def head_norm2(x_ref):
        x = x_ref[...].astype(F32)
        x = x * x
        return jnp.concatenate([jnp.sum(x[h * HEAD_DIM:(h + 1) * HEAD_DIM], axis=0, keepdims=True)
                                for h in range(N_HEADS)], axis=0)

    qn2 = head_norm2(qt_ref)
    kmax2 = jnp.max(head_norm2(kt_ref), axis=1, keepdims=True)
    c = c_ref[0:N_HEADS, :]
    lane = lax.broadcasted_iota(jnp.int32, (N_HEADS, lanes), 1)
    qmax2 = jnp.zeros((N_HEADS, lanes), F32)
    cmax = jnp.zeros((N_HEADS, lanes), F32)
    cmin = jnp.full((N_HEADS, lanes), -jnp.inf, F32)
    for j in range(nblk):
        blk = slice(j * tq, (j + 1) * tq)
        qmax2 = jnp.where(lane == j, jnp.max(qn2[:, blk], axis=1, keepdims=True), qmax2)
        cmax = jnp.where(lane == j, jnp.max(c[:, blk], axis=1, keepdims=True), cmax)
        cmin = jnp.where(lane == j, jnp.min(c[:, blk], axis=1, keepdims=True), cmin)
    shift = 1
    while shift < nblk:
        cmin = jnp.minimum(cmin, jnp.where(lane >= shift, pltpu.roll(cmin, shift, 1), jnp.inf))
        shift *= 2
    spread = 2.0 * FOX_BOUND_SLACK * jnp.sqrt(qmax2 * kmax2)
    limit = cmax + spread + FOX_SKIP_LOG2
    skip = jnp.zeros((N_HEADS, lanes), F32)
    for i in range(nblk):
        lim_i = jnp.sum(jnp.where(lane == i, limit, 0.0), axis=1, keepdims=True)
        n_i = jnp.sum(jnp.where(cmin > lim_i, 1.0, 0.0), axis=1, keepdims=True)
        skip = jnp.where(lane == i, n_i, skip)
    fast = jnp.where(spread <= FOX_FAST_LOG2, 1.0, 0.0)
    rows = []
    for p in range(N_HEADS // 2):
        rows.append(jnp.minimum(skip[2 * p:2 * p + 1], skip[2 * p + 1:2 * p + 2]))
    for p in range(N_HEADS // 2):
        rows.append(jnp.minimum(fast[2 * p:2 * p + 1], fast[2 * p + 1:2 * p + 2]))
    rows.append(jnp.zeros((V7X_SUBLANES - len(rows), lanes), F32))
    plan_ref[...] = jnp.concatenate(rows, axis=0).astype(jnp.int32)


def _fox_plan(qt, kt, g_row, batch):
    gw, n = qt.shape
    s = n // batch
    lanes = V7X_LANES
    assert s // TQ_FOX <= lanes
    seq = pl.BlockSpec((gw, s), lambda b: (0, b))
    return pl.pallas_call(
        functools.partial(_fox_plan_body, tq=TQ_FOX),
        grid=(batch,),
        in_specs=[seq, seq, pl.BlockSpec((g_row.shape[0], s), lambda b: (0, b))],
        out_specs=pl.BlockSpec((None, V7X_SUBLANES, lanes), lambda b: (b, 0, 0)),
        out_shape=jax.ShapeDtypeStruct((batch, V7X_SUBLANES, lanes), jnp.int32),
        compiler_params=_cparams(("parallel",), 56),
        name="fox_plan",
    )(qt, kt, g_row)


def _fox_body(skip_ref, fast_ref, q_ref, kt_ref, v_ref, c_ref, o_ref,
              ref_ref, l_ref, acc_ref, m_ref, ls_ref):
    b, pair, i = pl.program_id(0), pl.program_id(1), pl.program_id(2)
    tq, w = q_ref.shape
    tk = 2 * tq
    idx = (b * pl.num_programs(1) + pair) * pl.num_programs(2) + i
    skip = skip_ref[idx]
    q = q_ref[...]
    first = lax.broadcasted_iota(jnp.int32, (tq, w), 1) < HEAD_DIM
    zero = jnp.zeros_like(q)
    qs = jnp.concatenate([jnp.where(first, q, zero), jnp.where(first, zero, q)], axis=0)

    def scores(start, size):
        s = jnp.dot(qs, kt_ref[:, pl.ds(start, size)], preferred_element_type=F32)
        return [s[hh * tq:(hh + 1) * tq] - c_ref[hh:hh + 1, pl.ds(start, size)] for hh in range(2)]

    def weighted_values(start, size, p):
        stacked = jnp.concatenate(p, axis=0).astype(BF16)
        return jnp.dot(stacked, v_ref[pl.ds(start, size), :], preferred_element_type=F32)

    def earlier_blocks(block):
        def body(j, carry):
            block(pl.multiple_of(j * tk, tk), tk)
            return carry

        lax.fori_loop(skip // 2, i // 2, body, 0)

        @pl.when(jnp.logical_and(i % 2 == 1, skip < i))
        def _():
            block(pl.multiple_of((i - 1) * tq, tq), tq)

    diag_start = pl.multiple_of(i * tq, tq)
    qry = lax.broadcasted_iota(jnp.int32, (tq, tq), 0)
    key = lax.broadcasted_iota(jnp.int32, (tq, tq), 1)

    @pl.when(fast_ref[idx] == 1)
    def _():
        p = []
        for hh, s in enumerate(scores(diag_start, tq)):
            ref = jnp.sum(jnp.where(key == qry, s, 0.0), axis=1, keepdims=True)
            ref_ref[hh] = jnp.broadcast_to(ref, (tq, w))
            p.append(jnp.exp2(jnp.where(key <= qry, s - ref, FOX_NEG)))
            l_ref[hh] = sum(p[hh][:, u * w:(u + 1) * w] for u in range(tq // w))
        acc_ref[...] = weighted_values(diag_start, tq, p)

        def block(start, size):
            p = []
            for hh, s in enumerate(scores(start, size)):
                ref = ref_ref[hh]
                tiles = [jnp.exp2(s[:, u * w:(u + 1) * w] - ref) for u in range(size // w)]
                l_ref[hh] = l_ref[hh] + sum(tiles)
                p.append(jnp.concatenate(tiles, axis=1))
            acc_ref[...] = acc_ref[...] + weighted_values(start, size, p)

        earlier_blocks(block)

    @pl.when(fast_ref[idx] != 1)
    def _():
        m_ref[...] = jnp.full(m_ref.shape, FOX_NEG, F32)
        ls_ref[...] = jnp.zeros_like(ls_ref)
        acc_ref[...] = jnp.zeros_like(acc_ref)

        def update(start, size, s_heads):
            p, alpha = [], []
            for hh, s in enumerate(s_heads):
                m_old = m_ref[hh]
                m_new = jnp.maximum(m_old, jnp.max(s, axis=1, keepdims=True))
                alpha.append(jnp.exp2(m_old - m_new))
                p.append(jnp.exp2(s - m_new))
                ls_ref[hh] = alpha[hh] * ls_ref[hh] + jnp.sum(p[hh], axis=1, keepdims=True)
                m_ref[hh] = m_new
            acc_ref[...] = (jnp.concatenate(alpha, axis=0) * acc_ref[...]
                            + weighted_values(start, size, p))

        earlier_blocks(lambda start, size: update(start, size, scores(start, size)))
        update(diag_start, tq, [jnp.where(key <= qry, s, FOX_NEG) for s in scores(diag_start, tq)])
        lane0 = lax.broadcasted_iota(jnp.int32, (tq, w), 1) == 0
        for hh in range(2):
            l_ref[hh] = jnp.where(lane0, ls_ref[hh], 0.0)

    l = [jnp.sum(l_ref[hh], axis=1, keepdims=True) for hh in range(2)]
    acc = acc_ref[...]
    o_ref[...] = jnp.where(first, acc[0:tq] / l[0], acc[tq:2 * tq] / l[1]).astype(o_ref.dtype)


def _fox(pb, kt, c_pairs, plan, batch, cols):
    n = pb.shape[0]
    s = n // batch
    nq = s // TQ_FOX
    w = V7X_LANES
    pairs = GROUP_WIDTH // w
    skip = plan[:, 0:pairs, 0:nq].reshape(-1)
    fast = plan[:, pairs:2 * pairs, 0:nq].reshape(-1)
    fq, fv = cols["fq"] * pairs, cols["fv"] * pairs
    return pl.pallas_call(
        _fox_body,
        grid_spec=pltpu.PrefetchScalarGridSpec(
            num_scalar_prefetch=2,
            grid=(batch, pairs, nq),
            in_specs=[
                pl.BlockSpec((TQ_FOX, w), lambda b, p, i, *_: (b * nq + i, fq + p)),
                pl.BlockSpec((w, s), lambda b, p, i, *_: (p, b)),
                pl.BlockSpec((s, w), lambda b, p, i, *_: (b, fv + p)),
                pl.BlockSpec((None, 2, s), lambda b, p, i, *_: (p, 0, b)),
            ],
            out_specs=pl.BlockSpec((TQ_FOX, w), lambda b, p, i, *_: (b * nq + i, p)),
            scratch_shapes=[
                pltpu.VMEM((2, TQ_FOX, w), F32),
                pltpu.VMEM((2, TQ_FOX, w), F32),
                pltpu.VMEM((2 * TQ_FOX, w), F32),
                pltpu.VMEM((2, TQ_FOX, 1), F32),
                pltpu.VMEM((2, TQ_FOX, 1), F32),
            ],
        ),
        out_shape=jax.ShapeDtypeStruct((n, GROUP_WIDTH), BF16),
        compiler_params=_cparams(("parallel", "parallel", "arbitrary"), 48),
        name="fox",
    )(skip, fast, pb, kt, pb, c_pairs)


def _head_of_lane(shape, axis):
    return lax.broadcasted_iota(jnp.int32, shape, axis) // HEAD_DIM


def _same_head(shape):
    return _head_of_lane(shape, 0) == _head_of_lane(shape, 1)


def _per_head(values, lane_head):
    out = values[N_HEADS - 1]
    for h in range(N_HEADS - 2, -1, -1):
        out = jnp.where(lane_head == h, values[h], out)
    return out


def _head_rmsnorm(o, w, ones_bd):
    ms = jnp.dot((o * o).astype(BF16), ones_bd, preferred_element_type=F32) * (1.0 / HEAD_DIM)
    return o * lax.rsqrt(ms + RMS_EPS) * w


def _ret_body(rq_ref, rk_ref, rv_ref, rg_ref, cos_ref, sin_ref, nw_ref, o_ref,
              state_ref, dmat_ref, tab_ref, bd_ref):
    c = C_RET
    gw = GROUP_WIDTH
    log_gamma = [math.log1p(-(2.0 ** (-5.0 - h))) for h in range(N_HEADS)]

    @pl.when(pl.program_id(1) == 0)
    def _():
        state_ref[...] = jnp.zeros_like(state_ref)
        r = lax.broadcasted_iota(jnp.int32, (c, c), 0)
        s = lax.broadcasted_iota(jnp.int32, (c, c), 1)
        dist = (r - s).astype(F32)
        for h in range(N_HEADS):
            dmat_ref[h] = jnp.where(s <= r, jnp.exp(dist * log_gamma[h]), 0.0)
        lg = _per_head(log_gamma, _head_of_lane((c, gw), 1))
        t = lax.broadcasted_iota(jnp.int32, (c, gw), 0).astype(F32)
        tab_ref[0] = jnp.exp((t + 1.0) * lg)
        tab_ref[1] = jnp.exp((c - 1.0 - t) * lg)
        tab_ref[2] = jnp.exp(c * lg)
        bd_ref[...] = jnp.where(_same_head((gw, gw)), 1.0, 0.0).astype(BF16)

    lane = lax.broadcasted_iota(jnp.int32, (c, gw), 1)
    lane_head = lane // HEAD_DIM
    first_half = lane % HEAD_DIM < HEAD_DIM // 2

    for ci in range(rq_ref.shape[0] // c):
        rows = slice(ci * c, (ci + 1) * c)
        cosx = jnp.concatenate([cos_ref[rows, :]] * (gw // V7X_LANES), axis=1)
        sinx = jnp.concatenate([sin_ref[rows, :]] * (gw // V7X_LANES), axis=1)

        def rotary(x):
            swapped = jnp.where(first_half, pltpu.roll(x, gw - HEAD_DIM // 2, 1),
                                pltpu.roll(x, HEAD_DIM // 2, 1))
            return x * cosx + swapped * sinx

        q = rotary(rq_ref[rows, :])
        k = rotary(rk_ref[rows, :]) * (HEAD_DIM ** -0.5)
        v = rv_ref[rows, :]
        qb, kb = q.astype(BF16), k.astype(BF16)
        zero = jnp.zeros_like(qb)
        o = jnp.zeros((c, gw), F32)
        for h in range(N_HEADS):
            sel = lane_head == h
            sc = lax.dot_general(jnp.where(sel, qb, zero), kb, (((1,), (1,)), ((), ())),
                                 preferred_element_type=F32)
            p = (sc * dmat_ref[h]).astype(BF16)
            o = o + jnp.where(sel, jnp.dot(p, v, preferred_element_type=F32), 0.0)
        st = state_ref[...]
        o = o + lax.dot_general(qb, st.astype(BF16), (((1,), (1,)), ((), ())),
                                preferred_element_type=F32) * tab_ref[0]
        kd = (k * tab_ref[1]).astype(BF16)
        vt = v.astype(F32).T.astype(BF16)
        upd = jnp.dot(vt, kd, preferred_element_type=F32)
        state_ref[...] = st * tab_ref[2, 0:1, :] + jnp.where(_same_head((gw, gw)), upd, 0.0)
        y = _head_rmsnorm(o, nw_ref[...], bd_ref[...]) * _silu(rg_ref[rows, :])
        o_ref[rows, :] = y.astype(o_ref.dtype)


def _retention(pf, pb, cos_t, sin_t, norm_w, batch, cols):
    n = pf.shape[0]
    s = n // batch
    nt = s // T_MIX
    gw = GROUP_WIDTH

    def tok(col):
        return pl.BlockSpec((T_MIX, gw), lambda b, t: (b * nt + t, col))

    tab = pl.BlockSpec((T_MIX, V7X_LANES), lambda b, t: (t, 0))
    return pl.pallas_call(
        _ret_body,
        grid=(batch, nt),
        in_specs=[tok(cols["rq"]), tok(cols["rk"]), tok(cols["rv"]), tok(cols["rg"]), tab, tab,
                  pl.BlockSpec((1, gw), lambda b, t: (0, 0))],
        out_specs=pl.BlockSpec((T_MIX, gw), lambda b, t: (b * nt + t, 0)),
        out_shape=jax.ShapeDtypeStruct((n, gw), BF16),
        scratch_shapes=[
            pltpu.VMEM((gw, gw), F32),
            pltpu.VMEM((N_HEADS, C_RET, C_RET), F32),
            pltpu.VMEM((3, C_RET, gw), F32),
            pltpu.VMEM((gw, gw), BF16),
        ],
        compiler_params=_cparams(("parallel", "arbitrary"), 24),
        name="retention",
    )(pf, pf, pb, pf, cos_t, sin_t, norm_w.reshape(1, gw))


def _ssd_body(xbc_ref, z_ref, gcol_ref, grow_ref, cw_ref, cb_ref, d_ref, nw_ref, o_ref,
              xpad_ref, state_ref):
    tile = xbc_ref.shape[0]
    c = C_SSD
    gw = GROUP_WIDTH
    pad = V7X_SUBLANES

    @pl.when(pl.program_id(1) == 0)
    def _():
        state_ref[...] = jnp.zeros_like(state_ref)
        xpad_ref[0:pad, :] = jnp.zeros((pad, xpad_ref.shape[1]), F32)

    xpad_ref[pad:pad + tile, :] = xbc_ref[...]
    conv = cb_ref[...]
    for j in range(CONV_WIDTH):
        off = pad - (CONV_WIDTH - 1) + j
        conv = conv + cw_ref[j:j + 1, :] * xpad_ref[off:off + tile, :]
    xpad_ref[0:pad, :] = xpad_ref[tile:tile + pad, :]
    xbc_all = _silu(conv)

    lane_head = _head_of_lane((c, gw), 1)
    r = lax.broadcasted_iota(jnp.int32, (c, c), 0)
    s = lax.broadcasted_iota(jnp.int32, (c, c), 1)
    for ci in range(tile // c):
        rows = slice(ci * c, (ci + 1) * c)
        xbc = xbc_all[rows]
        xs = xbc[:, 0:gw]
        bm = xbc[:, gw:gw + SSD_STATE]
        cm = xbc[:, gw + SSD_STATE:gw + 2 * SSD_STATE]
        gcol = gcol_ref[rows, :]
        dtx = _per_head([gcol[:, h:h + 1] for h in range(N_HEADS)], lane_head)
        bx = _per_head([gcol[:, N_HEADS + h:N_HEADS + h + 1] for h in range(N_HEADS)], lane_head)
        blast = bx[c - 1:c, :]
        v = xs * dtx
        vb = v.astype(BF16)

        sc = lax.dot_general(cm.astype(BF16), bm.astype(BF16), (((1,), (1,)), ((), ())),
                             preferred_element_type=F32)
        o = jnp.zeros((c, gw), F32)
        for h in range(N_HEADS):
            bcol = gcol[:, N_HEADS + h:N_HEADS + h + 1]
            brow = grow_ref[N_HEADS + h:N_HEADS + h + 1, rows]
            decay = jnp.where(s <= r, jnp.exp(jnp.minimum(bcol - brow, 0.0)), 0.0)
            p = (sc * decay).astype(BF16)
            o = o + jnp.where(lane_head == h, jnp.dot(p, vb, preferred_element_type=F32), 0.0)
        st = state_ref[...]
        o = o + jnp.dot(cm.astype(BF16), st.astype(BF16), preferred_element_type=F32) * jnp.exp(bx)
        vd = (v * jnp.exp(blast - bx)).astype(BF16)
        state_ref[...] = (st * jnp.exp(blast)
                          + jnp.dot(bm.T.astype(BF16), vd, preferred_element_type=F32))
        o = o + d_ref[...] * xs
        y = _rmsnorm(o * _silu(z_ref[rows, :]), nw_ref[...])
        o_ref[rows, :] = y.astype(o_ref.dtype)


def _ssd(pf, gcol, grow, conv_w, conv_b, d_x, norm_w, batch, cols):
    n = pf.shape[0]
    s = n // batch
    nt = s // T_MIX
    gw = GROUP_WIDTH
    xbc_w = gw + 2 * SSD_STATE
    const = lambda shape: pl.BlockSpec(shape, lambda b, t: (0, 0))
    return pl.pallas_call(
        _ssd_body,
        grid=(batch, nt),
        in_specs=[
            pl.BlockSpec((T_MIX, xbc_w), lambda b, t: (b * nt + t, cols["sxbc"])),
            pl.BlockSpec((T_MIX, gw), lambda b, t: (b * nt + t, cols["sz"])),
            pl.BlockSpec((T_MIX, 2 * N_HEADS), lambda b, t: (b * nt + t, 0)),
            pl.BlockSpec((2 * N_HEADS, T_MIX), lambda b, t: (0, b * nt + t)),
            const((CONV_WIDTH, xbc_w)), const((1, xbc_w)), const((1, gw)), const((1, gw)),
        ],
        out_specs=pl.BlockSpec((T_MIX, gw), lambda b, t: (b * nt + t, 0)),
        out_shape=jax.ShapeDtypeStruct((n, gw), BF16),
        scratch_shapes=[
            pltpu.VMEM((T_MIX + 2 * V7X_SUBLANES, xbc_w), F32),
            pltpu.VMEM((SSD_STATE, gw), F32),
        ],
        compiler_params=_cparams(("parallel", "arbitrary"), 24),
        name="ssd",
    )(pf, pf, gcol, grow, conv_w, conv_b.reshape(1, xbc_w), d_x, norm_w.reshape(1, gw))


def _hgrn_body(hq_ref, hf_ref, hi_ref, hg_ref, lbw_ref, nw_ref, o_ref,
               state_ref, arr_ref, acc_ref, stage_ref, inter_ref, dec_ref, bd_ref, *, layer):
    c, nch = C_HGRN, V7X_SUBLANES
    tile = c * nch
    gw = GROUP_WIDTH
    w = V7X_LANES
    tiles = range(gw // w)
    pitch = stage_ref.shape[1] // nch
    k_, b_, v_, q_ = range(4)

    @pl.when(pl.program_id(1) == 0)
    def _():
        state_ref[...] = jnp.zeros_like(state_ref)
        arr_ref[:, :, 0:nch, :] = jnp.zeros((4, gw // w, nch, w), F32)
        bd_ref[...] = jnp.where(_same_head((gw, gw)), 1.0, 0.0).astype(BF16)

    def interleaved(ref):
        for u in tiles:
            for ci in range(nch):
                stage_ref[u, ci * pitch:ci * pitch + c, :] = ref[ci * c:(ci + 1) * c, u * w:(u + 1) * w]
        return jnp.concatenate(
            [jnp.concatenate([stage_ref[u, pl.ds(t, nch, stride=pitch), :] for t in range(c)], axis=0)
             for u in tiles], axis=1)

    def put(slot, value):
        for u in tiles:
            arr_ref[slot, u, nch:nch + tile, :] = value[:, u * w:(u + 1) * w]

    def get(slot, start, rows):
        return jnp.concatenate([arr_ref[slot, u, start:start + rows, :] for u in tiles], axis=1)

    lbw = lbw_ref[...]
    e = jnp.exp(lbw - jnp.max(lbw, axis=0, keepdims=True))
    sm = e / jnp.sum(e, axis=0, keepdims=True)
    cs = sm[0:1, :]
    first = cs
    for d in range(1, layer + 1):
        cs = cs + sm[d:d + 1, :]
    lb = cs - first
    log_f = jnp.logaddexp(jnp.log(jnp.maximum(lb, 0.0)),
                          jnp.log1p(-lb) + jax.nn.log_sigmoid(interleaved(hf_ref)))
    k = 1.0 - jnp.exp(log_f)
    b = log_f
    shift = 1
    while shift < c:
        rows = shift * nch
        b = b + jnp.concatenate([jnp.zeros((rows, gw), F32), b[:tile - rows]], axis=0)
        shift *= 2
    b2 = b * LOG2E
    q = interleaved(hq_ref)
    put(k_, k)
    put(b_, b2)
    put(v_, interleaved(hi_ref))
    put(q_, q)
    bd = bd_ref[...]

    blast = b2[tile - nch:tile]
    qe = q * jnp.exp2(b2)
    kd = k * jnp.exp2(jnp.concatenate([blast] * c, axis=0) - b2)
    for u in tiles:
        dec_ref[0, u] = qe[:, u * w:(u + 1) * w]
        dec_ref[1, u] = kd[:, u * w:(u + 1) * w]
    decay = jnp.exp2(blast)

    def dec_chunk(slot, ci):
        return jnp.concatenate([dec_ref[slot, u, pl.ds(ci, c, stride=nch), :] for u in tiles], axis=1)

    def state_step(ci):
        rows = slice(ci * c, (ci + 1) * c)
        st = state_ref[...]
        inter_ref[rows, :] = lax.dot_general(dec_chunk(0, ci).astype(BF16), st.astype(BF16),
                                             (((1,), (1,)), ((), ())), preferred_element_type=F32)
        upd = jnp.dot(hi_ref[rows, :].T.astype(BF16), dec_chunk(1, ci).astype(BF16),
                      preferred_element_type=F32)
        state_ref[...] = st * decay[ci:ci + 1] + jnp.where(_same_head((gw, gw)), upd, 0.0)

    steps_apart = (c // 2) // nch
    for m in range(c // 2):
        if m % steps_apart == 0:
            state_step(m // steps_apart)
        lo = nch * (1 + 2 * m)
        rows = tile + nch - lo
        qs = get(q_, lo, rows)
        bq = get(b_, lo, rows)
        starts = [lo - nch * d for d in (2 * m, 2 * m + 1)]
        wts = [qs * get(k_, s0, rows) * jnp.exp2(bq - get(b_, s0, rows)) for s0 in starts]
        seg = jnp.dot(jnp.concatenate(wts, axis=0).astype(BF16), bd, preferred_element_type=F32)
        term = sum(seg[j * rows:(j + 1) * rows] * get(v_, s0, rows) for j, s0 in enumerate(starts))
        for u in tiles:
            piece = term[:, u * w:(u + 1) * w]
            if m == 0:
                acc_ref[u] = piece
            else:
                acc_ref[u, lo - nch:tile, :] = acc_ref[u, lo - nch:tile, :] + piece

    intra = jnp.concatenate(
        [jnp.concatenate([acc_ref[u, pl.ds(ci, c, stride=nch), :] for u in tiles], axis=1)
         for ci in range(nch)], axis=0)
    y = _head_rmsnorm(intra + inter_ref[...], nw_ref[...], bd) * _silu(hg_ref[...])
    o_ref[...] = y.astype(o_ref.dtype)


def _hgrn(pf, lower_bounds, norm_w, layer, batch, cols):
    n = pf.shape[0]
    s = n // batch
    nch = V7X_SUBLANES
    tile = C_HGRN * nch
    nt = s // tile
    gw = GROUP_WIDTH
    lane_tiles = gw // V7X_LANES
    depth = lower_bounds.shape[0]

    def tok(col):
        return pl.BlockSpec((tile, gw), lambda b, t: (b * nt + t, col))

    return pl.pallas_call(
        functools.partial(_hgrn_body, layer=layer),
        grid=(batch, nt),
        in_specs=[tok(cols["hq"]), tok(cols["hf"]), tok(cols["hi"]), tok(cols["hg"]),
                  pl.BlockSpec((depth, gw), lambda b, t: (0, 0)),
                  pl.BlockSpec((1, gw), lambda b, t: (0, 0))],
        out_specs=pl.BlockSpec((tile, gw), lambda b, t: (b * nt + t, 0)),
        out_shape=jax.ShapeDtypeStruct((n, gw), BF16),
        scratch_shapes=[
            pltpu.VMEM((gw, gw), F32),
            pltpu.VMEM((4, lane_tiles, nch + tile, V7X_LANES), F32),
            pltpu.VMEM((lane_tiles, tile, V7X_LANES), F32),
            pltpu.VMEM((lane_tiles, nch * (C_HGRN + nch), V7X_LANES), F32),
            pltpu.VMEM((tile, gw), F32),
            pltpu.VMEM((2, lane_tiles, tile, V7X_LANES), F32),
            pltpu.VMEM((gw, gw), BF16),
        ],
        compiler_params=_cparams(("parallel", "arbitrary"), 24),
        name="hgrn2",
    )(pf, pf, pf, pf, lower_bounds, norm_w.reshape(1, gw))


def _post_body(h_ref, y0_ref, y1_ref, y2_ref, y3_ref, p_ref, wo_ref, fnw_ref, wup_ref, wdn_ref,
               pnw_ref, wg_ref, wp_ref, fw_ref, o_ref, act_ref, *, final):
    gw = GROUP_WIDTH
    h = h_ref[...]
    for m, y_ref in enumerate((y0_ref, y1_ref, y2_ref, y3_ref)):
        h = h + jnp.dot(y_ref[...], wo_ref[m * gw:(m + 1) * gw, :], preferred_element_type=F32)
    h = _ffn_math(h, fnw_ref, wup_ref, wdn_ref, act_ref)
    x = _rmsnorm(h, pnw_ref[...]).astype(BF16)
    gate = jax.nn.sigmoid(jnp.dot(x, wg_ref[...], preferred_element_type=F32))
    emb = jnp.dot(p_ref[...].astype(BF16), wp_ref[...], preferred_element_type=F32)
    h = h + gate * emb
    if final:
        h = _rmsnorm(h, fw_ref[...])
    o_ref[...] = h


def _post(h, ys, p, w_out, ffn_norm, w_up, w_down, ple_norm, w_gate, w_proj, final_w, final):
    n, d = h.shape
    gw = GROUP_WIDTH
    pd = p.shape[1]
    d_ff = w_down.shape[0]
    tok = pl.BlockSpec((TM_FFN, d), lambda i: (i, 0))
    ytok = pl.BlockSpec((TM_FFN, gw), lambda i: (i, 0))
    return pl.pallas_call(
        functools.partial(_post_body, final=final),
        grid=(n // TM_FFN,),
        in_specs=[tok, ytok, ytok, ytok, ytok, pl.BlockSpec((TM_FFN, pd), lambda i: (i, 0)),
                  _resident(w_out.shape), _resident((1, d)), _resident((d, 2 * d_ff)),
                  _resident((d_ff, d)), _resident((1, d)), _resident((d, d)), _resident((pd, d)),
                  _resident((1, d))],
        out_specs=tok,
        out_shape=jax.ShapeDtypeStruct((n, d), F32),
        scratch_shapes=[pltpu.VMEM((TM_FFN, d_ff), BF16)],
        compiler_params=_cparams(("parallel",), 56),
        name="post",
    )(h, *ys, p, w_out, ffn_norm.reshape(1, d), w_up, w_down, ple_norm.reshape(1, d),
      w_gate, w_proj, final_w.reshape(1, d))


def _split_w_in(w_in):
    gw = GROUP_WIDTH
    xbc_w = gw + 2 * SSD_STATE
    names = ["fq", "fk", "fv", "ff", "rq", "rk", "rv", "rg", "sz", "sxbc", "sdt", "hq", "hf", "hi", "hg"]
    sizes = [gw, gw, gw, N_HEADS, gw, gw, gw, gw, gw, xbc_w, N_HEADS, gw, gw, gw, gw]
    offs = dict(zip(names, np.concatenate([[0], np.cumsum(sizes)[:-1]]).tolist()))
    width = dict(zip(names, sizes))

    def gather(group):
        return jnp.concatenate([w_in[:, offs[k]:offs[k] + width[k]] for k in group], axis=1)

    b_group = ["fq", "fv", "rv"]
    f_group = ["rq", "rk", "rg", "sz", "sxbc", "hq", "hf", "hi", "hg"]
    cols = {}
    for group in (b_group, f_group):
        pos = 0
        for k in group:
            cols[k] = pos // (xbc_w if k == "sxbc" else gw)
            pos += width[k]
    wb = gather(b_group).astype(BF16)
    wf = gather(f_group).astype(BF16)
    wqkt = gather(["fq", "fk"]).T.astype(BF16)
    wst = gather(["ff", "sdt"]).T.astype(BF16)
    return wb, wf, wqkt, wst, cols


def kernel(x, p, ffn1_norm, ffn1_w_up, ffn1_w_down, mix_norm, w_in, fox_f_bias, ret_norm, conv_w, conv_b, dt_bias, a_log, ssd_d, ssd_norm, hgrn_lower_bounds, hgrn_norm, w_out, ffn2_norm, ffn2_w_up, ffn2_w_down, ple_norm, ple_w_gate, ple_w_proj, final_norm):
    batch, s, d = x.shape
    depth = p.shape[0]
    n = batch * s
    h = x.reshape(n, d)
    cos_t, sin_t = _rope_tables(s)
    zeros4 = jnp.zeros((N_HEADS,), F32)
    for i in range(depth):
        h = _ffn(h, ffn1_norm[i], ffn1_w_up[i].astype(BF16), ffn1_w_down[i].astype(BF16))

        wb, wf, wqkt, wst, cols = _split_w_in(w_in[i])
        pb, pf, qt, kt, st = _inproj(h, mix_norm[i], wb, wf, wqkt, wst)
        bias = jnp.concatenate([fox_f_bias[i], dt_bias[i]]).reshape(2 * N_HEADS, 1)
        alog = jnp.concatenate([zeros4, a_log[i]]).reshape(2 * N_HEADS, 1)
        g_row, dt_row = _gates(st, bias, alog, batch)
        c_pairs = g_row[:N_HEADS].reshape(N_HEADS // 2, 2, n)
        gcol = jnp.concatenate([dt_row[N_HEADS:], g_row[N_HEADS:]], axis=0).T
        d_x = jnp.repeat(ssd_d[i], HEAD_DIM).reshape(1, GROUP_WIDTH)

        y_fox = _fox(pb, kt, c_pairs, _fox_plan(qt, kt, g_row, batch), batch, cols)
        y_ret = _retention(pf, pb, cos_t, sin_t, ret_norm[i], batch, cols)
        y_ssd = _ssd(pf, gcol, g_row, conv_w[i], conv_b[i], d_x, ssd_norm[i], batch, cols)
        y_hg = _hgrn(pf, hgrn_lower_bounds, hgrn_norm[i], i, batch, cols)
        h = _post(h, (y_fox, y_ret, y_ssd, y_hg), p[i].reshape(n, -1), w_out[i].astype(BF16),
                  ffn2_norm[i], ffn2_w_up[i].astype(BF16), ffn2_w_down[i].astype(BF16),
                  ple_norm[i], ple_w_gate[i].astype(BF16), ple_w_proj[i].astype(BF16),
                  final_norm, final=(i == depth - 1))
    return h.reshape(batch, s, d)
```

```python
import functools
import math

import jax
import jax.numpy as jnp
import numpy as np
from jax import lax
from jax.experimental import pallas as pl
from jax.experimental.pallas import tpu as pltpu

F32 = jnp.float32
BF16 = jnp.bfloat16

HEAD_DIM = 64
N_HEADS = 4
GROUP_WIDTH = N_HEADS * HEAD_DIM
SSD_STATE = 128
CONV_WIDTH = 4
RMS_EPS = 1e-6
ROPE_BASE = 10000.0
LOG2E = 1.4426950408889634

V7X_LANES = 128
V7X_SUBLANES = 8
V7X_VMEM_BYTES = 64 * 1024 * 1024
MIB = 1024 * 1024

TM_FFN = 1024
FF_CHUNK = 256
TQ_FOX = 512
C_RET = 128
C_SSD = 128
C_HGRN = 32
T_MIX = 1024
TL_GATES = 2048


def _cparams(semantics, vmem_mib):
    return pltpu.CompilerParams(dimension_semantics=semantics, vmem_limit_bytes=int(vmem_mib * MIB))


def _rmsnorm(x, w):
    return x * lax.rsqrt(jnp.mean(x * x, axis=-1, keepdims=True) + RMS_EPS) * w


def _silu(x):
    return x * jax.nn.sigmoid(x)


def _resident(shape):
    return pl.BlockSpec(shape, lambda *_: (0,) * len(shape), pipeline_mode=pl.Buffered(1))


def _ffn_math(h, nw_ref, wup_ref, wdn_ref, act_ref):
    d_ff = wdn_ref.shape[0]
    x = _rmsnorm(h, nw_ref[...]).astype(BF16)
    for c in range(d_ff // FF_CHUNK):
        lo = c * FF_CHUNK
        gate = jnp.dot(x, wup_ref[:, lo:lo + FF_CHUNK], preferred_element_type=F32)
        up = jnp.dot(x, wup_ref[:, d_ff + lo:d_ff + lo + FF_CHUNK], preferred_element_type=F32)
        act_ref[:, lo:lo + FF_CHUNK] = (_silu(gate) * up).astype(BF16)
    return h + 0.5 * jnp.dot(act_ref[...], wdn_ref[...], preferred_element_type=F32)


def _ffn_body(h_ref, nw_ref, wup_ref, wdn_ref, o_ref, act_ref):
    o_ref[...] = _ffn_math(h_ref[...], nw_ref, wup_ref, wdn_ref, act_ref)


def _ffn(h, norm_w, w_up, w_down):
    n, d = h.shape
    d_ff = w_down.shape[0]
    return pl.pallas_call(
        _ffn_body,
        grid=(n // TM_FFN,),
        in_specs=[
            pl.BlockSpec((TM_FFN, d), lambda i: (i, 0)),
            _resident((1, d)),
            _resident((d, 2 * d_ff)),
            _resident((d_ff, d)),
        ],
        out_specs=pl.BlockSpec((TM_FFN, d), lambda i: (i, 0)),
        out_shape=jax.ShapeDtypeStruct((n, d), F32),
        scratch_shapes=[pltpu.VMEM((TM_FFN, d_ff), BF16)],
        compiler_params=_cparams(("parallel",), 56),
        name="ffn",
    )(h, norm_w.reshape(1, d), w_up, w_down)


def _inproj_body(h_ref, nw_ref, wb_ref, wf_ref, wqkt_ref, wst_ref,
                 pb_ref, pf_ref, qt_ref, kt_ref, st_ref, *, q_scale):
    x = _rmsnorm(h_ref[...], nw_ref[...]).astype(BF16)
    gw = GROUP_WIDTH
    for c in range(wb_ref.shape[1] // gw):
        pb_ref[:, c * gw:(c + 1) * gw] = jnp.dot(x, wb_ref[:, c * gw:(c + 1) * gw],
                                                preferred_element_type=F32).astype(BF16)
    for c in range(wf_ref.shape[1] // gw):
        pf_ref[:, c * gw:(c + 1) * gw] = jnp.dot(x, wf_ref[:, c * gw:(c + 1) * gw],
                                                preferred_element_type=F32)
    nt = (((1,), (1,)), ((), ()))
    qk = lax.dot_general(wqkt_ref[...], x, nt, preferred_element_type=F32)
    qt_ref[...] = (qk[0:gw] * q_scale).astype(BF16)
    kt_ref[...] = qk[gw:2 * gw].astype(BF16)
    st_ref[...] = lax.dot_general(wst_ref[...], x, nt, preferred_element_type=F32)


def _inproj(h, norm_w, wb, wf, wqkt, wst):
    n, d = h.shape
    nb, nf = wb.shape[1], wf.shape[1]
    gw = GROUP_WIDTH
    lanes_out = pl.BlockSpec((gw, TM_FFN), lambda i: (0, i))
    return pl.pallas_call(
        functools.partial(_inproj_body, q_scale=HEAD_DIM ** -0.5 * LOG2E),
        grid=(n // TM_FFN,),
        in_specs=[
            pl.BlockSpec((TM_FFN, d), lambda i: (i, 0)),
            _resident((1, d)),
            _resident((d, nb)),
            _resident((d, nf)),
            _resident((2 * gw, d)),
            _resident((2 * N_HEADS, d)),
        ],
        out_specs=[
            pl.BlockSpec((TM_FFN, nb), lambda i: (i, 0)),
            pl.BlockSpec((TM_FFN, nf), lambda i: (i, 0)),
            lanes_out, lanes_out,
            pl.BlockSpec((2 * N_HEADS, TM_FFN), lambda i: (0, i)),
        ],
        out_shape=[
            jax.ShapeDtypeStruct((n, nb), BF16),
            jax.ShapeDtypeStruct((n, nf), F32),
            jax.ShapeDtypeStruct((gw, n), BF16),
            jax.ShapeDtypeStruct((gw, n), BF16),
            jax.ShapeDtypeStruct((2 * N_HEADS, n), F32),
        ],
        compiler_params=_cparams(("parallel",), 56),
        name="inproj",
    )(h, norm_w.reshape(1, d), wb, wf, wqkt, wst)


def _gates_body(st_ref, bias_ref, alog_ref, g_ref, dt_ref, carry_ref):
    @pl.when(pl.program_id(1) == 0)
    def _():
        carry_ref[...] = jnp.zeros_like(carry_ref)

    x = st_ref[...] + bias_ref[...]
    tl = x.shape[1]
    row = lax.broadcasted_iota(jnp.int32, x.shape, 0)
    lane = lax.broadcasted_iota(jnp.int32, x.shape, 1)
    lane_in_chunk = lane % C_SSD
    is_fox = row < N_HEADS
    dt = jax.nn.softplus(x)
    val = jnp.where(is_fox, jax.nn.log_sigmoid(x), dt * (-jnp.exp(alog_ref[...])))
    full, local = val, val
    shift = 1
    while shift < tl:
        full = full + jnp.where(lane >= shift, pltpu.roll(full, shift, 1), 0.0)
        if shift < C_SSD:
            local = local + jnp.where(lane_in_chunk >= shift, pltpu.roll(local, shift, 1), 0.0)
        shift *= 2
    cum = full + carry_ref[...]
    carry_ref[...] = jnp.sum(jnp.where(lane == tl - 1, cum, 0.0), axis=1, keepdims=True)
    g_ref[...] = jnp.where(is_fox, cum * LOG2E, local)
    dt_ref[...] = dt


def _gates(st, bias, alog, batch):
    rows, n = st.shape
    s = n // batch
    nt = s // TL_GATES
    spec = pl.BlockSpec((rows, TL_GATES), lambda b, t: (0, b * nt + t))
    col = pl.BlockSpec((rows, 1), lambda b, t: (0, 0))
    return pl.pallas_call(
        _gates_body,
        grid=(batch, nt),
        in_specs=[spec, col, col],
        out_specs=[spec] * 2,
        out_shape=[jax.ShapeDtypeStruct((rows, n), F32)] * 2,
        scratch_shapes=[pltpu.VMEM((rows, 1), F32)],
        compiler_params=_cparams(("parallel", "arbitrary"), 16),
        name="gates",
    )(st, bias, alog)


def _rope_body(inv_ref, cos_ref, sin_ref):
    rows = cos_ref.shape[0]
    pos = (lax.broadcasted_iota(jnp.int32, cos_ref.shape, 0) + pl.program_id(0) * rows).astype(F32)
    lane = lax.broadcasted_iota(jnp.int32, cos_ref.shape, 1)
    ang = pos * inv_ref[...]
    cos_ref[...] = jnp.cos(ang)
    sin = jnp.sin(ang)
    sin_ref[...] = jnp.where(lane % HEAD_DIM < HEAD_DIM // 2, -sin, sin)


def _rope_tables(s):
    half = HEAD_DIM // 2
    inv_freq = ROPE_BASE ** (-jnp.arange(half, dtype=F32) / half)
    inv = jnp.tile(inv_freq, V7X_LANES // half).reshape(1, V7X_LANES)
    rows = 1024
    return pl.pallas_call(
        _rope_body,
        grid=(s // rows,),
        in_specs=[pl.BlockSpec((1, V7X_LANES), lambda i: (0, 0))],
        out_specs=[pl.BlockSpec((rows, V7X_LANES), lambda i: (i, 0))] * 2,
        out_shape=[jax.ShapeDtypeStruct((s, V7X_LANES), F32)] * 2,
        compiler_params=_cparams(("parallel",), 16),
        name="rope_tables",
    )(inv)


FOX_NEG = -1e30


FOX_SKIP_LOG2 = 160.0
FOX_FAST_LOG2 = 80.0
FOX_BOUND_SLACK = 1.01


def _fox_plan_body(qt_ref, kt_ref, c_ref, plan_ref, *, tq):
    gw, s = qt_ref.shape
    nblk = s // tq
    lanes = plan_ref.shape[-1]

    def head_norm2(x_ref):
        x = x_ref[...].astype(F32)
        x = x * x
        return jnp.concatenate([jnp.sum(x[h * HEAD_DIM:(h + 1) * HEAD_DIM], axis=0, keepdims=True)
                                for h in range(N_HEADS)], axis=0)

    qn2 = head_norm2(qt_ref)
    kmax2 = jnp.max(head_norm2(kt_ref), axis=1, keepdims=True)
    c = c_ref[0:N_HEADS, :]
    lane = lax.broadcasted_iota(jnp.int32, (N_HEADS, lanes), 1)
    qmax2 = jnp.zeros((N_HEADS, lanes), F32)
    cmax = jnp.zeros((N_HEADS, lanes), F32)
    cmin = jnp.full((N_HEADS, lanes), -jnp.inf, F32)
    for j in range(nblk):
        blk = slice(j * tq, (j + 1) * tq)
        qmax2 = jnp.where(lane == j, jnp.max(qn2[:, blk], axis=1, keepdims=True), qmax2)
        cmax = jnp.where(lane == j, jnp.max(c[:, blk], axis=1, keepdims=True), cmax)
        cmin = jnp.where(lane == j, jnp.min(c[:, blk], axis=1, keepdims=True), cmin)
    shift = 1
    while shift < nblk:
        cmin = jnp.minimum(cmin, jnp.where(lane >= shift, pltpu.roll(cmin, shift, 1), jnp.inf))
        shift *= 2
    spread = 2.0 * FOX_BOUND_SLACK * jnp.sqrt(qmax2 * kmax2)
    limit = cmax + spread + FOX_SKIP_LOG2
    skip = jnp.zeros((N_HEADS, lanes), F32)
    for i in range(nblk):
        lim_i = jnp.sum(jnp.where(lane == i, limit, 0.0), axis=1, keepdims=True)
        n_i = jnp.sum(jnp.where(cmin > lim_i, 1.0, 0.0), axis=1, keepdims=True)
        skip = jnp.where(lane == i, n_i, skip)
    fast = jnp.where(spread <= FOX_FAST_LOG2, 1.0, 0.0)
    rows = []
    for p in range(N_HEADS // 2):
        rows.append(jnp.minimum(skip[2 * p:2 * p + 1], skip[2 * p + 1:2 * p + 2]))
    for p in range(N_HEADS // 2):
        rows.append(jnp.minimum(fast[2 * p:2 * p + 1], fast[2 * p + 1:2 * p + 2]))
    rows.append(jnp.zeros((V7X_SUBLANES - len(rows), lanes), F32))
    plan_ref[...] = jnp.concatenate(rows, axis=0).astype(jnp.int32)


def _fox_plan(qt, kt, g_row, batch):
    gw, n = qt.shape
    s = n // batch
    lanes = V7X_LANES
    assert s // TQ_FOX <= lanes
    seq = pl.BlockSpec((gw, s), lambda b: (0, b))
    return pl.pallas_call(
        functools.partial(_fox_plan_body, tq=TQ_FOX),
        grid=(batch,),
        in_specs=[seq, seq, pl.BlockSpec((g_row.shape[0], s), lambda b: (0, b))],
        out_specs=pl.BlockSpec((None, V7X_SUBLANES, lanes), lambda b: (b, 0, 0)),
        out_shape=jax.ShapeDtypeStruct((batch, V7X_SUBLANES, lanes), jnp.int32),
        compiler_params=_cparams(("parallel",), 56),
        name="fox_plan",
    )(qt, kt, g_row)


def _fox_body(skip_ref, fast_ref, qt_ref, kt_ref, v_ref, c_ref, o_ref,
              ref_ref, l_ref, acc_ref, m_ref, ls_ref):
    b, pair, i = pl.program_id(0), pl.program_id(1), pl.program_id(2)
    w, tq = qt_ref.shape
    tk = 2 * tq
    idx = (b * pl.num_programs(1) + pair) * pl.num_programs(2) + i
    skip = skip_ref[idx]
    q = qt_ref[...].astype(F32).T.astype(BF16)
    first = lax.broadcasted_iota(jnp.int32, (tq, w), 1) < HEAD_DIM
    zero = jnp.zeros_like(q)
    qs = jnp.concatenate([jnp.where(first, q, zero), jnp.where(first, zero, q)], axis=0)

    def scores(start, size):
        s = jnp.dot(qs, kt_ref[:, pl.ds(start, size)], preferred_element_type=F32)
        return [s[hh * tq:(hh + 1) * tq] - c_ref[hh:hh + 1, pl.ds(start, size)] for hh in range(2)]

    def weighted_values(start, size, p):
        stacked = jnp.concatenate(p, axis=0).astype(BF16)
        return jnp.dot(stacked, v_ref[pl.ds(start, size), :], preferred_element_type=F32)

    def earlier_blocks(block):
        def body(j, carry):
            block(pl.multiple_of(j * tk, tk), tk)
            return carry

        lax.fori_loop(skip // 2, i // 2, body, 0)

        @pl.when(jnp.logical_and(i % 2 == 1, skip < i))
        def _():
            block(pl.multiple_of((i - 1) * tq, tq), tq)

    diag_start = pl.multiple_of(i * tq, tq)
    qry = lax.broadcasted_iota(jnp.int32, (tq, tq), 0)
    key = lax.broadcasted_iota(jnp.int32, (tq, tq), 1)

    @pl.when(fast_ref[idx] == 1)
    def _():
        p = []
        for hh, s in enumerate(scores(diag_start, tq)):
            ref = jnp.sum(jnp.where(key == qry, s, 0.0), axis=1, keepdims=True)
            ref_ref[hh] = jnp.broadcast_to(ref, (tq, w))
            p.append(jnp.exp2(jnp.where(key <= qry, s - ref, FOX_NEG)))
            l_ref[hh] = sum(p[hh][:, u * w:(u + 1) * w] for u in range(tq // w))
        acc_ref[...] = weighted_values(diag_start, tq, p)

        def block(start, size):
            p = []
            for hh, s in enumerate(scores(start, size)):
                ref = ref_ref[hh]
                tiles = [jnp.exp2(s[:, u * w:(u + 1) * w] - ref) for u in range(size // w)]
                l_ref[hh] = l_ref[hh] + sum(tiles)
                p.append(jnp.concatenate(tiles, axis=1))
            acc_ref[...] = acc_ref[...] + weighted_values(start, size, p)

        earlier_blocks(block)

    @pl.when(fast_ref[idx] != 1)
    def _():
        m_ref[...] = jnp.full(m_ref.shape, FOX_NEG, F32)
        ls_ref[...] = jnp.zeros_like(ls_ref)
        acc_ref[...] = jnp.zeros_like(acc_ref)

        def update(start, size, s_heads):
            p, alpha = [], []
            for hh, s in enumerate(s_heads):
                m_old = m_ref[hh]
                m_new = jnp.maximum(m_old, jnp.max(s, axis=1, keepdims=True))
                alpha.append(jnp.exp2(m_old - m_new))
                p.append(jnp.exp2(s - m_new))
                ls_ref[hh] = alpha[hh] * ls_ref[hh] + jnp.sum(p[hh], axis=1, keepdims=True)
                m_ref[hh] = m_new
            acc_ref[...] = (jnp.concatenate(alpha, axis=0) * acc_ref[...]
                            + weighted_values(start, size, p))

        earlier_blocks(lambda start, size: update(start, size, scores(start, size)))
        update(diag_start, tq, [jnp.where(key <= qry, s, FOX_NEG) for s in scores(diag_start, tq)])
        lane0 = lax.broadcasted_iota(jnp.int32, (tq, w), 1) == 0
        for hh in range(2):
            l_ref[hh] = jnp.where(lane0, ls_ref[hh], 0.0)

    l = [jnp.sum(l_ref[hh], axis=1, keepdims=True) for hh in range(2)]
    acc = acc_ref[...]
    o_ref[...] = jnp.where(first, acc[0:tq] / l[0], acc[tq:2 * tq] / l[1]).astype(o_ref.dtype)


def _fox(qt, kt, pb, c_pairs, plan, batch, cols):
    n = pb.shape[0]
    s = n // batch
    nq = s // TQ_FOX
    w = V7X_LANES
    pairs = GROUP_WIDTH // w
    skip = plan[:, 0:pairs, 0:nq].reshape(-1)
    fast = plan[:, pairs:2 * pairs, 0:nq].reshape(-1)
    fv = cols["fv"] * pairs
    return pl.pallas_call(
        _fox_body,
        grid_spec=pltpu.PrefetchScalarGridSpec(
            num_scalar_prefetch=2,
            grid=(batch, pairs, nq),
            in_specs=[
                pl.BlockSpec((w, TQ_FOX), lambda b, p, i, *_: (p, b * nq + i)),
                pl.BlockSpec((w, s), lambda b, p, i, *_: (p, b)),
                pl.BlockSpec((s, w), lambda b, p, i, *_: (b, fv + p)),
                pl.BlockSpec((None, 2, s), lambda b, p, i, *_: (p, 0, b)),
            ],
            out_specs=pl.BlockSpec((TQ_FOX, w), lambda b, p, i, *_: (b * nq + i, p)),
            scratch_shapes=[
                pltpu.VMEM((2, TQ_FOX, w), F32),
                pltpu.VMEM((2, TQ_FOX, w), F32),
                pltpu.VMEM((2 * TQ_FOX, w), F32),
                pltpu.VMEM((2, TQ_FOX, 1), F32),
                pltpu.VMEM((2, TQ_FOX, 1), F32),
            ],
        ),
        out_shape=jax.ShapeDtypeStruct((n, GROUP_WIDTH), BF16),
        compiler_params=_cparams(("parallel", "parallel", "arbitrary"), 48),
        name="fox",
    )(skip, fast, qt, kt, pb, c_pairs)


def _head_of_lane(shape, axis):
    return lax.broadcasted_iota(jnp.int32, shape, axis) // HEAD_DIM


def _same_head(shape):
    return _head_of_lane(shape, 0) == _head_of_lane(shape, 1)


def _per_head(values, lane_head):
    out = values[N_HEADS - 1]
    for h in range(N_HEADS - 2, -1, -1):
        out = jnp.where(lane_head == h, values[h], out)
    return out


def _head_rmsnorm(o, w, ones_bd):
    ms = jnp.dot((o * o).astype(BF16), ones_bd, preferred_element_type=F32) * (1.0 / HEAD_DIM)
    return o * lax.rsqrt(ms + RMS_EPS) * w


def _ret_body(rq_ref, rk_ref, rv_ref, rg_ref, cos_ref, sin_ref, nw_ref, o_ref,
              state_ref, dmat_ref, tab_ref, bd_ref):
    c = C_RET
    gw = GROUP_WIDTH
    log_gamma = [math.log1p(-(2.0 ** (-5.0 - h))) for h in range(N_HEADS)]

    @pl.when(pl.program_id(1) == 0)
    def _():
        state_ref[...] = jnp.zeros_like(state_ref)
        r = lax.broadcasted_iota(jnp.int32, (c, c), 0)
        s = lax.broadcasted_iota(jnp.int32, (c, c), 1)
        dist = (r - s).astype(F32)
        for h in range(N_HEADS):
            dmat_ref[h] = jnp.where(s <= r, jnp.exp(dist * log_gamma[h]), 0.0)
        lg = _per_head(log_gamma, _head_of_lane((c, gw), 1))
        t = lax.broadcasted_iota(jnp.int32, (c, gw), 0).astype(F32)
        tab_ref[0] = jnp.exp((t + 1.0) * lg)
        tab_ref[1] = jnp.exp((c - 1.0 - t) * lg)
        tab_ref[2] = jnp.exp(c * lg)
        bd_ref[...] = jnp.where(_same_head((gw, gw)), 1.0, 0.0).astype(BF16)

    lane = lax.broadcasted_iota(jnp.int32, (c, gw), 1)
    lane_head = lane // HEAD_DIM
    first_half = lane % HEAD_DIM < HEAD_DIM // 2

    for ci in range(rq_ref.shape[0] // c):
        rows = slice(ci * c, (ci + 1) * c)
        cosx = jnp.concatenate([cos_ref[rows, :]] * (gw // V7X_LANES), axis=1)
        sinx = jnp.concatenate([sin_ref[rows, :]] * (gw // V7X_LANES), axis=1)

        def rotary(x):
            swapped = jnp.where(first_half, pltpu.roll(x, gw - HEAD_DIM // 2, 1),
                                pltpu.roll(x, HEAD_DIM // 2, 1))
            return x * cosx + swapped * sinx

        q = rotary(rq_ref[rows, :])
        k = rotary(rk_ref[rows, :]) * (HEAD_DIM ** -0.5)
        v = rv_ref[rows, :]
        qb, kb = q.astype(BF16), k.astype(BF16)
        zero = jnp.zeros_like(qb)
        o = jnp.zeros((c, gw), F32)
        for h in range(N_HEADS):
            sel = lane_head == h
            sc = lax.dot_general(jnp.where(sel, qb, zero), kb, (((1,), (1,)), ((), ())),
                                 preferred_element_type=F32)
            p = (sc * dmat_ref[h]).astype(BF16)
            o = o + jnp.where(sel, jnp.dot(p, v, preferred_element_type=F32), 0.0)
        st = state_ref[...]
        o = o + lax.dot_general(qb, st.astype(BF16), (((1,), (1,)), ((), ())),
                                preferred_element_type=F32) * tab_ref[0]
        kd = (k * tab_ref[1]).astype(BF16)
        vt = v.astype(F32).T.astype(BF16)
        upd = jnp.dot(vt, kd, preferred_element_type=F32)
        state_ref[...] = st * tab_ref[2, 0:1, :] + jnp.where(_same_head((gw, gw)), upd, 0.0)
        y = _head_rmsnorm(o, nw_ref[...], bd_ref[...]) * _silu(rg_ref[rows, :])
        o_ref[rows, :] = y.astype(o_ref.dtype)


def _retention(pf, pb, cos_t, sin_t, norm_w, batch, cols):
    n = pf.shape[0]
    s = n // batch
    nt = s // T_MIX
    gw = GROUP_WIDTH

    def tok(col):
        return pl.BlockSpec((T_MIX, gw), lambda b, t: (b * nt + t, col))

    tab = pl.BlockSpec((T_MIX, V7X_LANES), lambda b, t: (t, 0))
    return pl.pallas_call(
        _ret_body,
        grid=(batch, nt),
        in_specs=[tok(cols["rq"]), tok(cols["rk"]), tok(cols["rv"]), tok(cols["rg"]), tab, tab,
                  pl.BlockSpec((1, gw), lambda b, t: (0, 0))],
        out_specs=pl.BlockSpec((T_MIX, gw), lambda b, t: (b * nt + t, 0)),
        out_shape=jax.ShapeDtypeStruct((n, gw), BF16),
        scratch_shapes=[
            pltpu.VMEM((gw, gw), F32),
            pltpu.VMEM((N_HEADS, C_RET, C_RET), F32),
            pltpu.VMEM((3, C_RET, gw), F32),
            pltpu.VMEM((gw, gw), BF16),
        ],
        compiler_params=_cparams(("parallel", "arbitrary"), 24),
        name="retention",
    )(pf, pf, pb, pf, cos_t, sin_t, norm_w.reshape(1, gw))


def _ssd_body(xbc_ref, z_ref, gcol_ref, grow_ref, cw_ref, cb_ref, d_ref, nw_ref, o_ref,
              stage_ref, act_ref, tail_ref, state_ref):
    tile = xbc_ref.shape[0]
    c = C_SSD
    gw = GROUP_WIDTH
    w = V7X_LANES
    nsub = V7X_SUBLANES
    span = tile // nsub
    pitch = stage_ref.shape[1] // nsub
    tiles = range(xbc_ref.shape[1] // w)
    taps = CONV_WIDTH - 1

    @pl.when(pl.program_id(1) == 0)
    def _():
        state_ref[...] = jnp.zeros_like(state_ref)
        tail_ref[...] = jnp.zeros_like(tail_ref)

    sub_id = lax.broadcasted_iota(jnp.int32, (nsub, w), 0)
    for u in tiles:
        for si in range(nsub):
            stage_ref[u, si * pitch:si * pitch + span, :] = xbc_ref[si * span:(si + 1) * span,
                                                                    u * w:(u + 1) * w]
        x = jnp.concatenate([stage_ref[u, pl.ds(t, nsub, stride=pitch), :] for t in range(span)],
                            axis=0)
        halo = []
        for j in range(taps):
            grp = x[(span - taps + j) * nsub:(span - taps + j + 1) * nsub]
            prev = tail_ref[u, j:j + 1, :]
            halo.append(jnp.where(sub_id == 0, prev, pltpu.roll(grp, 1, 0)))
            tail_ref[u, j:j + 1, :] = grp[nsub - 1:nsub]
        ext = jnp.concatenate(halo + [x], axis=0)
        conv = cb_ref[:, u * w:(u + 1) * w]
        for j in range(CONV_WIDTH):
            conv = conv + cw_ref[j:j + 1, u * w:(u + 1) * w] * ext[j * nsub:j * nsub + tile]
        act_ref[u] = _silu(conv)

    def act_rows(ci):
        per_sub = span // c
        si, off = ci // per_sub, (ci % per_sub) * c
        return jnp.concatenate(
            [act_ref[u, pl.ds(off * nsub + si, c, stride=nsub), :] for u in tiles], axis=1)

    lane_head = _head_of_lane((c, gw), 1)
    r = lax.broadcasted_iota(jnp.int32, (c, c), 0)
    s = lax.broadcasted_iota(jnp.int32, (c, c), 1)
    for ci in range(tile // c):
        rows = slice(ci * c, (ci + 1) * c)
        xbc = act_rows(ci)
        xs = xbc[:, 0:gw]
        bm = xbc[:, gw:gw + SSD_STATE]
        cm = xbc[:, gw + SSD_STATE:gw + 2 * SSD_STATE]
        gcol = gcol_ref[rows, :]
        dtx = _per_head([gcol[:, h:h + 1] for h in range(N_HEADS)], lane_head)
        bx = _per_head([gcol[:, N_HEADS + h:N_HEADS + h + 1] for h in range(N_HEADS)], lane_head)
        blast = bx[c - 1:c, :]
        v = xs * dtx
        vb = v.astype(BF16)

        sc = lax.dot_general(cm.astype(BF16), bm.astype(BF16), (((1,), (1,)), ((), ())),
                             preferred_element_type=F32)
        o = jnp.zeros((c, gw), F32)
        for h in range(N_HEADS):
            bcol = gcol[:, N_HEADS + h:N_HEADS + h + 1]
            brow = grow_ref[N_HEADS + h:N_HEADS + h + 1, rows]
            decay = jnp.where(s <= r, jnp.exp(jnp.minimum(bcol - brow, 0.0)), 0.0)
            p = (sc * decay).astype(BF16)
            o = o + jnp.where(lane_head == h, jnp.dot(p, vb, preferred_element_type=F32), 0.0)
        st = state_ref[...]
        o = o + jnp.dot(cm.astype(BF16), st.astype(BF16), preferred_element_type=F32) * jnp.exp(bx)
        vd = (v * jnp.exp(blast - bx)).astype(BF16)
        state_ref[...] = (st * jnp.exp(blast)
                          + jnp.dot(bm.T.astype(BF16), vd, preferred_element_type=F32))
        o = o + d_ref[...] * xs
        y = _rmsnorm(o * _silu(z_ref[rows, :]), nw_ref[...])
        o_ref[rows, :] = y.astype(o_ref.dtype)


def _ssd(pf, gcol, grow, conv_w, conv_b, d_x, norm_w, batch, cols):
    n = pf.shape[0]
    s = n // batch
    nt = s // T_MIX
    gw = GROUP_WIDTH
    xbc_w = gw + 2 * SSD_STATE
    const = lambda shape: pl.BlockSpec(shape, lambda b, t: (0, 0))
    return pl.pallas_call(
        _ssd_body,
        grid=(batch, nt),
        in_specs=[
            pl.BlockSpec((T_MIX, xbc_w), lambda b, t: (b * nt + t, cols["sxbc"])),
            pl.BlockSpec((T_MIX, gw), lambda b, t: (b * nt + t, cols["sz"])),
            pl.BlockSpec((T_MIX, 2 * N_HEADS), lambda b, t: (b * nt + t, 0)),
            pl.BlockSpec((2 * N_HEADS, T_MIX), lambda b, t: (0, b * nt + t)),
            const((CONV_WIDTH, xbc_w)), const((1, xbc_w)), const((1, gw)), const((1, gw)),
        ],
        out_specs=pl.BlockSpec((T_MIX, gw), lambda b, t: (b * nt + t, 0)),
        out_shape=jax.ShapeDtypeStruct((n, gw), BF16),
        scratch_shapes=[
            pltpu.VMEM((xbc_w // V7X_LANES, T_MIX + V7X_SUBLANES * V7X_SUBLANES, V7X_LANES), F32),
            pltpu.VMEM((xbc_w // V7X_LANES, T_MIX, V7X_LANES), F32),
            pltpu.VMEM((xbc_w // V7X_LANES, V7X_SUBLANES, V7X_LANES), F32),
            pltpu.VMEM((SSD_STATE, gw), F32),
        ],
        compiler_params=_cparams(("parallel", "arbitrary"), 24),
        name="ssd",
    )(pf, pf, gcol, grow, conv_w, conv_b.reshape(1, xbc_w), d_x, norm_w.reshape(1, gw))


def _hgrn_body(hq_ref, hf_ref, hi_ref, hg_ref, lbw_ref, nw_ref, o_ref,
               state_ref, arr_ref, acc_ref, stage_ref, inter_ref, dec_ref, bd_ref, *, layer):
    c, nch = C_HGRN, V7X_SUBLANES
    tile = c * nch
    gw = GROUP_WIDTH
    w = V7X_LANES
    tiles = range(gw // w)
    pitch = stage_ref.shape[1] // nch
    k_, b_, v_, q_ = range(4)

    @pl.when(pl.program_id(1) == 0)
    def _():
        state_ref[...] = jnp.zeros_like(state_ref)
        arr_ref[:, :, 0:nch, :] = jnp.zeros((4, gw // w, nch, w), F32)
        bd_ref[...] = jnp.where(_same_head((gw, gw)), 1.0, 0.0).astype(BF16)

    def interleaved(ref):
        for u in tiles:
            for ci in range(nch):
                stage_ref[u, ci * pitch:ci * pitch + c, :] = ref[ci * c:(ci + 1) * c, u * w:(u + 1) * w]
        return jnp.concatenate(
            [jnp.concatenate([stage_ref[u, pl.ds(t, nch, stride=pitch), :] for t in range(c)], axis=0)
             for u in tiles], axis=1)

    def put(slot, value):
        for u in tiles:
            arr_ref[slot, u, nch:nch + tile, :] = value[:, u * w:(u + 1) * w]

    def get(slot, start, rows):
        return jnp.concatenate([arr_ref[slot, u, start:start + rows, :] for u in tiles], axis=1)

    lbw = lbw_ref[...]
    e = jnp.exp(lbw - jnp.max(lbw, axis=0, keepdims=True))
    sm = e / jnp.sum(e, axis=0, keepdims=True)
    cs = sm[0:1, :]
    first = cs
    for d in range(1, layer + 1):
        cs = cs + sm[d:d + 1, :]
    lb = cs - first
    log_f = jnp.logaddexp(jnp.log(jnp.maximum(lb, 0.0)),
                          jnp.log1p(-lb) + jax.nn.log_sigmoid(interleaved(hf_ref)))
    k = 1.0 - jnp.exp(log_f)
    b = log_f
    shift = 1
    while shift < c:
        rows = shift * nch
        b = b + jnp.concatenate([jnp.zeros((rows, gw), F32), b[:tile - rows]], axis=0)
        shift *= 2
    b2 = b * LOG2E
    q = interleaved(hq_ref)
    put(k_, k)
    put(b_, b2)
    put(v_, interleaved(hi_ref))
    put(q_, q)
    bd = bd_ref[...]

    blast = b2[tile - nch:tile]
    qe = q * jnp.exp2(b2)
    kd = k * jnp.exp2(jnp.concatenate([blast] * c, axis=0) - b2)
    for u in tiles:
        dec_ref[0, u] = qe[:, u * w:(u + 1) * w]
        dec_ref[1, u] = kd[:, u * w:(u + 1) * w]
    decay = jnp.exp2(blast)

    def dec_chunk(slot, ci):
        return jnp.concatenate([dec_ref[slot, u, pl.ds(ci, c, stride=nch), :] for u in tiles], axis=1)

    def state_step(ci):
        rows = slice(ci * c, (ci + 1) * c)
        st = state_ref[...]
        inter_ref[rows, :] = lax.dot_general(dec_chunk(0, ci).astype(BF16), st.astype(BF16),
                                             (((1,), (1,)), ((), ())), preferred_element_type=F32)
        upd = jnp.dot(hi_ref[rows, :].T.astype(BF16), dec_chunk(1, ci).astype(BF16),
                      preferred_element_type=F32)
        state_ref[...] = st * decay[ci:ci + 1] + jnp.where(_same_head((gw, gw)), upd, 0.0)

    steps_apart = (c // 2) // nch
    for m in range(c // 2):
        if m % steps_apart == 0:
            state_step(m // steps_apart)
        lo = nch * (1 + 2 * m)
        rows = tile + nch - lo
        qs = get(q_, lo, rows)
        bq = get(b_, lo, rows)
        starts = [lo - nch * d for d in (2 * m, 2 * m + 1)]
        wts = [qs * get(k_, s0, rows) * jnp.exp2(bq - get(b_, s0, rows)) for s0 in starts]
        seg = jnp.dot(jnp.concatenate(wts, axis=0).astype(BF16), bd, preferred_element_type=F32)
        term = sum(seg[j * rows:(j + 1) * rows] * get(v_, s0, rows) for j, s0 in enumerate(starts))
        for u in tiles:
            piece = term[:, u * w:(u + 1) * w]
            if m == 0:
                acc_ref[u] = piece
            else:
                acc_ref[u, lo - nch:tile, :] = acc_ref[u, lo - nch:tile, :] + piece

    intra = jnp.concatenate(
        [jnp.concatenate([acc_ref[u, pl.ds(ci, c, stride=nch), :] for u in tiles], axis=1)
         for ci in range(nch)], axis=0)
    y = _head_rmsnorm(intra + inter_ref[...], nw_ref[...], bd) * _silu(hg_ref[...])
    o_ref[...] = y.astype(o_ref.dtype)


def _hgrn(pf, lower_bounds, norm_w, layer, batch, cols):
    n = pf.shape[0]
    s = n // batch
    nch = V7X_SUBLANES
    tile = C_HGRN * nch
    nt = s // tile
    gw = GROUP_WIDTH
    lane_tiles = gw // V7X_LANES
    depth = lower_bounds.shape[0]

    def tok(col):
        return pl.BlockSpec((tile, gw), lambda b, t: (b * nt + t, col))

    return pl.pallas_call(
        functools.partial(_hgrn_body, layer=layer),
        grid=(batch, nt),
        in_specs=[tok(cols["hq"]), tok(cols["hf"]), tok(cols["hi"]), tok(cols["hg"]),
                  pl.BlockSpec((depth, gw), lambda b, t: (0, 0)),
                  pl.BlockSpec((1, gw), lambda b, t: (0, 0))],
        out_specs=pl.BlockSpec((tile, gw), lambda b, t: (b * nt + t, 0)),
        out_shape=jax.ShapeDtypeStruct((n, gw), BF16),
        scratch_shapes=[
            pltpu.VMEM((gw, gw), F32),
            pltpu.VMEM((4, lane_tiles, nch + tile, V7X_LANES), F32),
            pltpu.VMEM((lane_tiles, tile, V7X_LANES), F32),
            pltpu.VMEM((lane_tiles, nch * (C_HGRN + nch), V7X_LANES), F32),
            pltpu.VMEM((tile, gw), F32),
            pltpu.VMEM((2, lane_tiles, tile, V7X_LANES), F32),
            pltpu.VMEM((gw, gw), BF16),
        ],
        compiler_params=_cparams(("parallel", "arbitrary"), 24),
        name="hgrn2",
    )(pf, pf, pf, pf, lower_bounds, norm_w.reshape(1, gw))


def _post_body(h_ref, y0_ref, y1_ref, y2_ref, y3_ref, p_ref, wo_ref, fnw_ref, wup_ref, wdn_ref,
               pnw_ref, wg_ref, wp_ref, fw_ref, o_ref, act_ref, *, final):
    gw = GROUP_WIDTH
    h = h_ref[...]
    for m, y_ref in enumerate((y0_ref, y1_ref, y2_ref, y3_ref)):
        h = h + jnp.dot(y_ref[...], wo_ref[m * gw:(m + 1) * gw, :], preferred_element_type=F32)
    h = _ffn_math(h, fnw_ref, wup_ref, wdn_ref, act_ref)
    x = _rmsnorm(h, pnw_ref[...]).astype(BF16)
    gate = jax.nn.sigmoid(jnp.dot(x, wg_ref[...], preferred_element_type=F32))
    emb = jnp.dot(p_ref[...].astype(BF16), wp_ref[...], preferred_element_type=F32)
    h = h + gate * emb
    if final:
        h = _rmsnorm(h, fw_ref[...])
    o_ref[...] = h


def _post(h, ys, p, w_out, ffn_norm, w_up, w_down, ple_norm, w_gate, w_proj, final_w, final):
    n, d = h.shape
    gw = GROUP_WIDTH
    pd = p.shape[1]
    d_ff = w_down.shape[0]
    tok = pl.BlockSpec((TM_FFN, d), lambda i: (i, 0))
    ytok = pl.BlockSpec((TM_FFN, gw), lambda i: (i, 0))
    return pl.pallas_call(
        functools.partial(_post_body, final=final),
        grid=(n // TM_FFN,),
        in_specs=[tok, ytok, ytok, ytok, ytok, pl.BlockSpec((TM_FFN, pd), lambda i: (i, 0)),
                  _resident(w_out.shape), _resident((1, d)), _resident((d, 2 * d_ff)),
                  _resident((d_ff, d)), _resident((1, d)), _resident((d, d)), _resident((pd, d)),
                  _resident((1, d))],
        out_specs=tok,
        out_shape=jax.ShapeDtypeStruct((n, d), F32),
        scratch_shapes=[pltpu.VMEM((TM_FFN, d_ff), BF16)],
        compiler_params=_cparams(("parallel",), 56),
        name="post",
    )(h, *ys, p, w_out, ffn_norm.reshape(1, d), w_up, w_down, ple_norm.reshape(1, d),
      w_gate, w_proj, final_w.reshape(1, d))


def _split_w_in(w_in):
    gw = GROUP_WIDTH
    xbc_w = gw + 2 * SSD_STATE
    names = ["fq", "fk", "fv", "ff", "rq", "rk", "rv", "rg", "sz", "sxbc", "sdt", "hq", "hf", "hi", "hg"]
    sizes = [gw, gw, gw, N_HEADS, gw, gw, gw, gw, gw, xbc_w, N_HEADS, gw, gw, gw, gw]
    offs = dict(zip(names, np.concatenate([[0], np.cumsum(sizes)[:-1]]).tolist()))
    width = dict(zip(names, sizes))

    def gather(group):
        return jnp.concatenate([w_in[:, offs[k]:offs[k] + width[k]] for k in group], axis=1)

    b_group = ["fv", "rv"]
    f_group = ["rq", "rk", "rg", "sz", "sxbc", "hq", "hf", "hi", "hg"]
    cols = {}
    for group in (b_group, f_group):
        pos = 0
        for k in group:
            cols[k] = pos // (xbc_w if k == "sxbc" else gw)
            pos += width[k]
    wb = gather(b_group).astype(BF16)
    wf = gather(f_group).astype(BF16)
    wqkt = gather(["fq", "fk"]).T.astype(BF16)
    wst = gather(["ff", "sdt"]).T.astype(BF16)
    return wb, wf, wqkt, wst, cols


def kernel(x, p, ffn1_norm, ffn1_w_up, ffn1_w_down, mix_norm, w_in, fox_f_bias, ret_norm, conv_w, conv_b, dt_bias, a_log, ssd_d, ssd_norm, hgrn_lower_bounds, hgrn_norm, w_out, ffn2_norm, ffn2_w_up, ffn2_w_down, ple_norm, ple_w_gate, ple_w_proj, final_norm):
    batch, s, d = x.shape
    depth = p.shape[0]
    n = batch * s
    h = x.reshape(n, d)
    cos_t, sin_t = _rope_tables(s)
    zeros4 = jnp.zeros((N_HEADS,), F32)
    for i in range(depth):
        h = _ffn(h, ffn1_norm[i], ffn1_w_up[i].astype(BF16), ffn1_w_down[i].astype(BF16))

        wb, wf, wqkt, wst, cols = _split_w_in(w_in[i])
        pb, pf, qt, kt, st = _inproj(h, mix_norm[i], wb, wf, wqkt, wst)
        bias = jnp.concatenate([fox_f_bias[i], dt_bias[i]]).reshape(2 * N_HEADS, 1)
        alog = jnp.concatenate([zeros4, a_log[i]]).reshape(2 * N_HEADS, 1)
        g_row, dt_row = _gates(st, bias, alog, batch)
        c_pairs = g_row[:N_HEADS].reshape(N_HEADS // 2, 2, n)
        gcol = jnp.concatenate([dt_row[N_HEADS:], g_row[N_HEADS:]], axis=0).T
        d_x = jnp.repeat(ssd_d[i], HEAD_DIM).reshape(1, GROUP_WIDTH)

        y_fox = _fox(qt, kt, pb, c_pairs, _fox_plan(qt, kt, g_row, batch), batch, cols)
        y_ret = _retention(pf, pb, cos_t, sin_t, ret_norm[i], batch, cols)
        y_ssd = _ssd(pf, gcol, g_row, conv_w[i], conv_b[i], d_x, ssd_norm[i], batch, cols)
        y_hg = _hgrn(pf, hgrn_lower_bounds, hgrn_norm[i], i, batch, cols)
        h = _post(h, (y_fox, y_ret, y_ssd, y_hg), p[i].reshape(n, -1), w_out[i].astype(BF16),
                  ffn2_norm[i], ffn2_w_up[i].astype(BF16), ffn2_w_down[i].astype(BF16),
                  ple_norm[i], ple_w_gate[i].astype(BF16), ple_w_proj[i].astype(BF16),
                  final_norm, final=(i == depth - 1))
    return h.reshape(batch, s, d)
```

```python
import functools
import math

import jax
import jax.numpy as jnp
import numpy as np
from jax import lax
from jax.experimental import pallas as pl
from jax.experimental.pallas import tpu as pltpu

F32 = jnp.float32
BF16 = jnp.bfloat16

HEAD_DIM = 64
N_HEADS = 4
GROUP_WIDTH = N_HEADS * HEAD_DIM
SSD_STATE = 128
CONV_WIDTH = 4
RMS_EPS = 1e-6
ROPE_BASE = 10000.0
LOG2E = 1.4426950408889634

V7X_LANES = 128
V7X_SUBLANES = 8
V7X_VMEM_BYTES = 64 * 1024 * 1024
MIB = 1024 * 1024

TM_FFN = 1024
FF_CHUNK = 256
TQ_FOX = 512
C_RET = 128
C_SSD = 128
C_HGRN = 32
T_MIX = 1024
TL_GATES = 2048


VMEM_DENSE = 56
VMEM_FOX = 48
VMEM_MIXER = 24
VMEM_SMALL = 16


def _cparams(semantics, vmem_mib):
    assert vmem_mib * MIB < V7X_VMEM_BYTES
    return pltpu.CompilerParams(dimension_semantics=semantics, vmem_limit_bytes=vmem_mib * MIB)


def _rmsnorm(x, w):
    return x * lax.rsqrt(jnp.mean(x * x, axis=-1, keepdims=True) + RMS_EPS) * w


def _silu(x):
    return x * jax.nn.sigmoid(x)


def _resident(shape):
    return pl.BlockSpec(shape, lambda *_: (0,) * len(shape), pipeline_mode=pl.Buffered(1))


def _ffn_math(h, nw_ref, wup_ref, wdn_ref, act_ref):
    d_ff = wdn_ref.shape[0]
    x = _rmsnorm(h, nw_ref[...]).astype(BF16)
    for c in range(d_ff // FF_CHUNK):
        lo = c * FF_CHUNK
        gate = jnp.dot(x, wup_ref[:, lo:lo + FF_CHUNK], preferred_element_type=F32)
        up = jnp.dot(x, wup_ref[:, d_ff + lo:d_ff + lo + FF_CHUNK], preferred_element_type=F32)
        act_ref[:, lo:lo + FF_CHUNK] = (_silu(gate) * up).astype(BF16)
    return h + 0.5 * jnp.dot(act_ref[...], wdn_ref[...], preferred_element_type=F32)


def _ffn_body(h_ref, nw_ref, wup_ref, wdn_ref, o_ref, act_ref):
    o_ref[...] = _ffn_math(h_ref[...], nw_ref, wup_ref, wdn_ref, act_ref)


def _ffn(h, norm_w, w_up, w_down):
    n, d = h.shape
    d_ff = w_down.shape[0]
    return pl.pallas_call(
        _ffn_body,
        grid=(n // TM_FFN,),
        in_specs=[
            pl.BlockSpec((TM_FFN, d), lambda i: (i, 0)),
            _resident((1, d)),
            _resident((d, 2 * d_ff)),
            _resident((d_ff, d)),
        ],
        out_specs=pl.BlockSpec((TM_FFN, d), lambda i: (i, 0)),
        out_shape=jax.ShapeDtypeStruct((n, d), F32),
        scratch_shapes=[pltpu.VMEM((TM_FFN, d_ff), BF16)],
        compiler_params=_cparams(("parallel",), VMEM_DENSE),
        name="ffn",
    )(h, norm_w.reshape(1, d), w_up, w_down)


def _inproj_body(h_ref, nw_ref, wb_ref, wf_ref, wqkt_ref, wst_ref,
                 pb_ref, pf_ref, qt_ref, kt_ref, st_ref, *, q_scale):
    x = _rmsnorm(h_ref[...], nw_ref[...]).astype(BF16)
    gw = GROUP_WIDTH
    for c in range(wb_ref.shape[1] // gw):
        pb_ref[:, c * gw:(c + 1) * gw] = jnp.dot(x, wb_ref[:, c * gw:(c + 1) * gw],
                                                preferred_element_type=F32).astype(BF16)
    for c in range(wf_ref.shape[1] // gw):
        pf_ref[:, c * gw:(c + 1) * gw] = jnp.dot(x, wf_ref[:, c * gw:(c + 1) * gw],
                                                preferred_element_type=F32)
    nt = (((1,), (1,)), ((), ()))
    qk = lax.dot_general(wqkt_ref[...], x, nt, preferred_element_type=F32)
    qt_ref[...] = (qk[0:gw] * q_scale).astype(BF16)
    kt_ref[...] = qk[gw:2 * gw].astype(BF16)
    st_ref[...] = lax.dot_general(wst_ref[...], x, nt, preferred_element_type=F32)


def _inproj(h, norm_w, wb, wf, wqkt, wst):
    n, d = h.shape
    nb, nf = wb.shape[1], wf.shape[1]
    gw = GROUP_WIDTH
    lanes_out = pl.BlockSpec((gw, TM_FFN), lambda i: (0, i))
    return pl.pallas_call(
        functools.partial(_inproj_body, q_scale=HEAD_DIM ** -0.5 * LOG2E),
        grid=(n // TM_FFN,),
        in_specs=[
            pl.BlockSpec((TM_FFN, d), lambda i: (i, 0)),
            _resident((1, d)),
            _resident((d, nb)),
            _resident((d, nf)),
            _resident((2 * gw, d)),
            _resident((2 * N_HEADS, d)),
        ],
        out_specs=[
            pl.BlockSpec((TM_FFN, nb), lambda i: (i, 0)),
            pl.BlockSpec((TM_FFN, nf), lambda i: (i, 0)),
            lanes_out, lanes_out,
            pl.BlockSpec((2 * N_HEADS, TM_FFN), lambda i: (0, i)),
        ],
        out_shape=[
            jax.ShapeDtypeStruct((n, nb), BF16),
            jax.ShapeDtypeStruct((n, nf), F32),
            jax.ShapeDtypeStruct((gw, n), BF16),
            jax.ShapeDtypeStruct((gw, n), BF16),
            jax.ShapeDtypeStruct((2 * N_HEADS, n), F32),
        ],
        compiler_params=_cparams(("parallel",), VMEM_DENSE),
        name="inproj",
    )(h, norm_w.reshape(1, d), wb, wf, wqkt, wst)


def _gates_body(st_ref, bias_ref, alog_ref, g_ref, dt_ref, carry_ref):
    @pl.when(pl.program_id(1) == 0)
    def _():
        carry_ref[...] = jnp.zeros_like(carry_ref)

    x = st_ref[...] + bias_ref[...]
    tl = x.shape[1]
    row = lax.broadcasted_iota(jnp.int32, x.shape, 0)
    lane = lax.broadcasted_iota(jnp.int32, x.shape, 1)
    lane_in_chunk = lane % C_SSD
    is_fox = row < N_HEADS
    dt = jax.nn.softplus(x)
    val = jnp.where(is_fox, jax.nn.log_sigmoid(x), dt * (-jnp.exp(alog_ref[...])))
    full, local = val, val
    shift = 1
    while shift < tl:
        full = full + jnp.where(lane >= shift, pltpu.roll(full, shift, 1), 0.0)
        if shift < C_SSD:
            local = local + jnp.where(lane_in_chunk >= shift, pltpu.roll(local, shift, 1), 0.0)
        shift *= 2
    cum = full + carry_ref[...]
    carry_ref[...] = jnp.sum(jnp.where(lane == tl - 1, cum, 0.0), axis=1, keepdims=True)
    g_ref[...] = jnp.where(is_fox, cum * LOG2E, local)
    dt_ref[...] = dt


def _gates(st, bias, alog, batch):
    rows, n = st.shape
    s = n // batch
    nt = s // TL_GATES
    spec = pl.BlockSpec((rows, TL_GATES), lambda b, t: (0, b * nt + t))
    col = pl.BlockSpec((rows, 1), lambda b, t: (0, 0))
    return pl.pallas_call(
        _gates_body,
        grid=(batch, nt),
        in_specs=[spec, col, col],
        out_specs=[spec] * 2,
        out_shape=[jax.ShapeDtypeStruct((rows, n), F32)] * 2,
        scratch_shapes=[pltpu.VMEM((rows, 1), F32)],
        compiler_params=_cparams(("parallel", "arbitrary"), VMEM_SMALL),
        name="gates",
    )(st, bias, alog)


def _rope_body(inv_ref, cos_ref, sin_ref):
    rows = cos_ref.shape[0]
    pos = (lax.broadcasted_iota(jnp.int32, cos_ref.shape, 0) + pl.program_id(0) * rows).astype(F32)
    lane = lax.broadcasted_iota(jnp.int32, cos_ref.shape, 1)
    ang = pos * inv_ref[...]
    cos_ref[...] = jnp.cos(ang)
    sin = jnp.sin(ang)
    sin_ref[...] = jnp.where(lane % HEAD_DIM < HEAD_DIM // 2, -sin, sin)


def _rope_tables(s):
    half = HEAD_DIM // 2
    inv_freq = ROPE_BASE ** (-jnp.arange(half, dtype=F32) / half)
    inv = jnp.tile(inv_freq, V7X_LANES // half).reshape(1, V7X_LANES)
    rows = 1024
    return pl.pallas_call(
        _rope_body,
        grid=(s // rows,),
        in_specs=[pl.BlockSpec((1, V7X_LANES), lambda i: (0, 0))],
        out_specs=[pl.BlockSpec((rows, V7X_LANES), lambda i: (i, 0))] * 2,
        out_shape=[jax.ShapeDtypeStruct((s, V7X_LANES), F32)] * 2,
        compiler_params=_cparams(("parallel",), VMEM_SMALL),
        name="rope_tables",
    )(inv)


FOX_NEG = -1e30


FOX_SKIP_LOG2 = 160.0
FOX_FAST_LOG2 = 80.0
FOX_BOUND_SLACK = 1.01


def _fox_plan_body(qt_ref, kt_ref, c_ref, plan_ref, *, tq):
    gw, s = qt_ref.shape
    nblk = s // tq
    lanes = plan_ref.shape[-1]

    def head_norm2(x_ref):
        x = x_ref[...].astype(F32)
        x = x * x
        return jnp.concatenate([jnp.sum(x[h * HEAD_DIM:(h + 1) * HEAD_DIM], axis=0, keepdims=True)
                                for h in range(N_HEADS)], axis=0)

    qn2 = head_norm2(qt_ref)
    kmax2 = jnp.max(head_norm2(kt_ref), axis=1, keepdims=True)
    c = c_ref[0:N_HEADS, :]
    lane = lax.broadcasted_iota(jnp.int32, (N_HEADS, lanes), 1)
    qmax2 = jnp.zeros((N_HEADS, lanes), F32)
    cmax = jnp.zeros((N_HEADS, lanes), F32)
    cmin = jnp.full((N_HEADS, lanes), -jnp.inf, F32)
    for j in range(nblk):
        blk = slice(j * tq, (j + 1) * tq)
        qmax2 = jnp.where(lane == j, jnp.max(qn2[:, blk], axis=1, keepdims=True), qmax2)
        cmax = jnp.where(lane == j, jnp.max(c[:, blk], axis=1, keepdims=True), cmax)
        cmin = jnp.where(lane == j, jnp.min(c[:, blk], axis=1, keepdims=True), cmin)
    shift = 1
    while shift < nblk:
        cmin = jnp.minimum(cmin, jnp.where(lane >= shift, pltpu.roll(cmin, shift, 1), jnp.inf))
        shift *= 2
    spread = 2.0 * FOX_BOUND_SLACK * jnp.sqrt(qmax2 * kmax2)
    limit = cmax + spread + FOX_SKIP_LOG2
    skip = jnp.zeros((N_HEADS, lanes), F32)
    for i in range(nblk):
        lim_i = jnp.sum(jnp.where(lane == i, limit, 0.0), axis=1, keepdims=True)
        n_i = jnp.sum(jnp.where(cmin > lim_i, 1.0, 0.0), axis=1, keepdims=True)
        skip = jnp.where(lane == i, n_i, skip)
    fast = jnp.where(spread <= FOX_FAST_LOG2, 1.0, 0.0)
    rows = []
    for p in range(N_HEADS // 2):
        rows.append(jnp.minimum(skip[2 * p:2 * p + 1], skip[2 * p + 1:2 * p + 2]))
    for p in range(N_HEADS // 2):
        rows.append(jnp.minimum(fast[2 * p:2 * p + 1], fast[2 * p + 1:2 * p + 2]))
    rows.append(jnp.zeros((V7X_SUBLANES - len(rows), lanes), F32))
    plan_ref[...] = jnp.concatenate(rows, axis=0).astype(jnp.int32)


def _fox_plan(qt, kt, g_row, batch):
    gw, n = qt.shape
    s = n // batch
    lanes = V7X_LANES
    assert s // TQ_FOX <= lanes
    seq = pl.BlockSpec((gw, s), lambda b: (0, b))
    return pl.pallas_call(
        functools.partial(_fox_plan_body, tq=TQ_FOX),
        grid=(batch,),
        in_specs=[seq, seq, pl.BlockSpec((g_row.shape[0], s), lambda b: (0, b))],
        out_specs=pl.BlockSpec((None, V7X_SUBLANES, lanes), lambda b: (b, 0, 0)),
        out_shape=jax.ShapeDtypeStruct((batch, V7X_SUBLANES, lanes), jnp.int32),
        compiler_params=_cparams(("parallel",), VMEM_DENSE),
        name="fox_plan",
    )(qt, kt, g_row)


def _fox_body(skip_ref, fast_ref, qt_ref, kt_ref, v_ref, c_ref, o_ref,
              ref_ref, l_ref, acc_ref, m_ref, ls_ref):
    b, pair, i = pl.program_id(0), pl.program_id(1), pl.program_id(2)
    w, tq = qt_ref.shape
    tk = 2 * tq
    idx = (b * pl.num_programs(1) + pair) * pl.num_programs(2) + i
    skip = skip_ref[idx]
    q = qt_ref[...].astype(F32).T.astype(BF16)
    first = lax.broadcasted_iota(jnp.int32, (tq, w), 1) < HEAD_DIM
    zero = jnp.zeros_like(q)
    qs = jnp.concatenate([jnp.where(first, q, zero), jnp.where(first, zero, q)], axis=0)

    def scores(start, size):
        s = jnp.dot(qs, kt_ref[:, pl.ds(start, size)], preferred_element_type=F32)
        return [s[hh * tq:(hh + 1) * tq] - c_ref[hh:hh + 1, pl.ds(start, size)] for hh in range(2)]

    def weighted_values(start, size, p):
        stacked = jnp.concatenate(p, axis=0).astype(BF16)
        return jnp.dot(stacked, v_ref[pl.ds(start, size), :], preferred_element_type=F32)

    def earlier_blocks(block):
        def body(j, carry):
            block(pl.multiple_of(j * tk, tk), tk)
            return carry

        lax.fori_loop(skip // 2, i // 2, body, 0)

        @pl.when(jnp.logical_and(i % 2 == 1, skip < i))
        def _():
            block(pl.multiple_of((i - 1) * tq, tq), tq)

    diag_start = pl.multiple_of(i * tq, tq)
    qry = lax.broadcasted_iota(jnp.int32, (tq, tq), 0)
    key = lax.broadcasted_iota(jnp.int32, (tq, tq), 1)

    @pl.when(fast_ref[idx] == 1)
    def _():
        p = []
        for hh, s in enumerate(scores(diag_start, tq)):
            ref = jnp.sum(jnp.where(key == qry, s, 0.0), axis=1, keepdims=True)
            ref_ref[hh] = jnp.broadcast_to(ref, (tq, w))
            p.append(jnp.exp2(jnp.where(key <= qry, s - ref, FOX_NEG)))
            l_ref[hh] = sum(p[hh][:, u * w:(u + 1) * w] for u in range(tq // w))
        acc_ref[...] = weighted_values(diag_start, tq, p)

        def block(start, size):
            p = []
            for hh, s in enumerate(scores(start, size)):
                ref = ref_ref[hh]
                tiles = [jnp.exp2(s[:, u * w:(u + 1) * w] - ref) for u in range(size // w)]
                l_ref[hh] = l_ref[hh] + sum(tiles)
                p.append(jnp.concatenate(tiles, axis=1))
            acc_ref[...] = acc_ref[...] + weighted_values(start, size, p)

        earlier_blocks(block)

    @pl.when(fast_ref[idx] != 1)
    def _():
        m_ref[...] = jnp.full(m_ref.shape, FOX_NEG, F32)
        ls_ref[...] = jnp.zeros_like(ls_ref)
        acc_ref[...] = jnp.zeros_like(acc_ref)

        def update(start, size, s_heads):
            p, alpha = [], []
            for hh, s in enumerate(s_heads):
                m_old = m_ref[hh]
                m_new = jnp.maximum(m_old, jnp.max(s, axis=1, keepdims=True))
                alpha.append(jnp.exp2(m_old - m_new))
                p.append(jnp.exp2(s - m_new))
                ls_ref[hh] = alpha[hh] * ls_ref[hh] + jnp.sum(p[hh], axis=1, keepdims=True)
                m_ref[hh] = m_new
            acc_ref[...] = (jnp.concatenate(alpha, axis=0) * acc_ref[...]
                            + weighted_values(start, size, p))

        earlier_blocks(lambda start, size: update(start, size, scores(start, size)))
        update(diag_start, tq, [jnp.where(key <= qry, s, FOX_NEG) for s in scores(diag_start, tq)])
        lane0 = lax.broadcasted_iota(jnp.int32, (tq, w), 1) == 0
        for hh in range(2):
            l_ref[hh] = jnp.where(lane0, ls_ref[hh], 0.0)

    l = [jnp.sum(l_ref[hh], axis=1, keepdims=True) for hh in range(2)]
    acc = acc_ref[...]
    o_ref[...] = jnp.where(first, acc[0:tq] / l[0], acc[tq:2 * tq] / l[1]).astype(o_ref.dtype)


def _fox(qt, kt, pb, c_pairs, plan, batch, cols):
    n = pb.shape[0]
    s = n // batch
    nq = s // TQ_FOX
    w = V7X_LANES
    pairs = GROUP_WIDTH // w
    skip = plan[:, 0:pairs, 0:nq].reshape(-1)
    fast = plan[:, pairs:2 * pairs, 0:nq].reshape(-1)
    fv = cols["fv"] * pairs
    return pl.pallas_call(
        _fox_body,
        grid_spec=pltpu.PrefetchScalarGridSpec(
            num_scalar_prefetch=2,
            grid=(batch, pairs, nq),
            in_specs=[
                pl.BlockSpec((w, TQ_FOX), lambda b, p, i, *_: (p, b * nq + i)),
                pl.BlockSpec((w, s), lambda b, p, i, *_: (p, b)),
                pl.BlockSpec((s, w), lambda b, p, i, *_: (b, fv + p)),
                pl.BlockSpec((None, 2, s), lambda b, p, i, *_: (p, 0, b)),
            ],
            out_specs=pl.BlockSpec((TQ_FOX, w), lambda b, p, i, *_: (b * nq + i, p)),
            scratch_shapes=[
                pltpu.VMEM((2, TQ_FOX, w), F32),
                pltpu.VMEM((2, TQ_FOX, w), F32),
                pltpu.VMEM((2 * TQ_FOX, w), F32),
                pltpu.VMEM((2, TQ_FOX, 1), F32),
                pltpu.VMEM((2, TQ_FOX, 1), F32),
            ],
        ),
        out_shape=jax.ShapeDtypeStruct((n, GROUP_WIDTH), BF16),
        compiler_params=_cparams(("parallel", "parallel", "arbitrary"), VMEM_FOX),
        name="fox",
    )(skip, fast, qt, kt, pb, c_pairs)


def _head_of_lane(shape, axis):
    return lax.broadcasted_iota(jnp.int32, shape, axis) // HEAD_DIM


def _same_head(shape):
    return _head_of_lane(shape, 0) == _head_of_lane(shape, 1)


def _per_head(values, lane_head):
    out = values[N_HEADS - 1]
    for h in range(N_HEADS - 2, -1, -1):
        out = jnp.where(lane_head == h, values[h], out)
    return out


def _head_rmsnorm(o, w, ones_bd):
    ms = jnp.dot((o * o).astype(BF16), ones_bd, preferred_element_type=F32) * (1.0 / HEAD_DIM)
    return o * lax.rsqrt(ms + RMS_EPS) * w


def _ret_body(rq_ref, rk_ref, rv_ref, rg_ref, cos_ref, sin_ref, nw_ref, o_ref,
              state_ref, dmat_ref, tab_ref, bd_ref):
    c = C_RET
    gw = GROUP_WIDTH
    log_gamma = [math.log1p(-(2.0 ** (-5.0 - h))) for h in range(N_HEADS)]

    @pl.when(pl.program_id(1) == 0)
    def _():
        state_ref[...] = jnp.zeros_like(state_ref)
        r = lax.broadcasted_iota(jnp.int32, (c, c), 0)
        s = lax.broadcasted_iota(jnp.int32, (c, c), 1)
        dist = (r - s).astype(F32)
        for h in range(N_HEADS):
            dmat_ref[h] = jnp.where(s <= r, jnp.exp(dist * log_gamma[h]), 0.0)
        lg = _per_head(log_gamma, _head_of_lane((c, gw), 1))
        t = lax.broadcasted_iota(jnp.int32, (c, gw), 0).astype(F32)
        tab_ref[0] = jnp.exp((t + 1.0) * lg)
        tab_ref[1] = jnp.exp((c - 1.0 - t) * lg)
        tab_ref[2] = jnp.exp(c * lg)
        bd_ref[...] = jnp.where(_same_head((gw, gw)), 1.0, 0.0).astype(BF16)

    lane = lax.broadcasted_iota(jnp.int32, (c, gw), 1)
    lane_head = lane // HEAD_DIM
    first_half = lane % HEAD_DIM < HEAD_DIM // 2

    for ci in range(rq_ref.shape[0] // c):
        rows = slice(ci * c, (ci + 1) * c)
        cosx = jnp.concatenate([cos_ref[rows, :]] * (gw // V7X_LANES), axis=1)
        sinx = jnp.concatenate([sin_ref[rows, :]] * (gw // V7X_LANES), axis=1)

        def rotary(x):
            swapped = jnp.where(first_half, pltpu.roll(x, gw - HEAD_DIM // 2, 1),
                                pltpu.roll(x, HEAD_DIM // 2, 1))
            return x * cosx + swapped * sinx

        q = rotary(rq_ref[rows, :])
        k = rotary(rk_ref[rows, :]) * (HEAD_DIM ** -0.5)
        v = rv_ref[rows, :]
        qb, kb = q.astype(BF16), k.astype(BF16)
        zero = jnp.zeros_like(qb)
        o = jnp.zeros((c, gw), F32)
        for h in range(N_HEADS):
            sel = lane_head == h
            sc = lax.dot_general(jnp.where(sel, qb, zero), kb, (((1,), (1,)), ((), ())),
                                 preferred_element_type=F32)
            p = (sc * dmat_ref[h]).astype(BF16)
            o = o + jnp.where(sel, jnp.dot(p, v, preferred_element_type=F32), 0.0)
        st = state_ref[...]
        o = o + lax.dot_general(qb, st.astype(BF16), (((1,), (1,)), ((), ())),
                                preferred_element_type=F32) * tab_ref[0]
        kd = (k * tab_ref[1]).astype(BF16)
        vt = v.astype(F32).T.astype(BF16)
        upd = jnp.dot(vt, kd, preferred_element_type=F32)
        state_ref[...] = st * tab_ref[2, 0:1, :] + jnp.where(_same_head((gw, gw)), upd, 0.0)
        y = _head_rmsnorm(o, nw_ref[...], bd_ref[...]) * _silu(rg_ref[rows, :])
        o_ref[rows, :] = y.astype(o_ref.dtype)


def _retention(pf, pb, cos_t, sin_t, norm_w, batch, cols):
    n = pf.shape[0]
    s = n // batch
    nt = s // T_MIX
    gw = GROUP_WIDTH

    def tok(col):
        return pl.BlockSpec((T_MIX, gw), lambda b, t: (b * nt + t, col))

    tab = pl.BlockSpec((T_MIX, V7X_LANES), lambda b, t: (t, 0))
    return pl.pallas_call(
        _ret_body,
        grid=(batch, nt),
        in_specs=[tok(cols["rq"]), tok(cols["rk"]), tok(cols["rv"]), tok(cols["rg"]), tab, tab,
                  pl.BlockSpec((1, gw), lambda b, t: (0, 0))],
        out_specs=pl.BlockSpec((T_MIX, gw), lambda b, t: (b * nt + t, 0)),
        out_shape=jax.ShapeDtypeStruct((n, gw), BF16),
        scratch_shapes=[
            pltpu.VMEM((gw, gw), F32),
            pltpu.VMEM((N_HEADS, C_RET, C_RET), F32),
            pltpu.VMEM((3, C_RET, gw), F32),
            pltpu.VMEM((gw, gw), BF16),
        ],
        compiler_params=_cparams(("parallel", "arbitrary"), VMEM_MIXER),
        name="retention",
    )(pf, pf, pb, pf, cos_t, sin_t, norm_w.reshape(1, gw))


def _ssd_body(xbc_ref, z_ref, gcol_ref, grow_ref, cw_ref, cb_ref, d_ref, nw_ref, o_ref,
              stage_ref, act_ref, tail_ref, state_ref):
    tile = xbc_ref.shape[0]
    c = C_SSD
    gw = GROUP_WIDTH
    w = V7X_LANES
    nsub = V7X_SUBLANES
    span = tile // nsub
    pitch = stage_ref.shape[1] // nsub
    tiles = range(xbc_ref.shape[1] // w)
    taps = CONV_WIDTH - 1

    @pl.when(pl.program_id(1) == 0)
    def _():
        state_ref[...] = jnp.zeros_like(state_ref)
        tail_ref[...] = jnp.zeros_like(tail_ref)

    sub_id = lax.broadcasted_iota(jnp.int32, (nsub, w), 0)
    for u in tiles:
        for si in range(nsub):
            stage_ref[u, si * pitch:si * pitch + span, :] = xbc_ref[si * span:(si + 1) * span,
                                                                    u * w:(u + 1) * w]
        x = jnp.concatenate([stage_ref[u, pl.ds(t, nsub, stride=pitch), :] for t in range(span)],
                            axis=0)
        halo = []
        for j in range(taps):
            grp = x[(span - taps + j) * nsub:(span - taps + j + 1) * nsub]
            prev = tail_ref[u, j:j + 1, :]
            halo.append(jnp.where(sub_id == 0, prev, pltpu.roll(grp, 1, 0)))
            tail_ref[u, j:j + 1, :] = grp[nsub - 1:nsub]
        ext = jnp.concatenate(halo + [x], axis=0)
        conv = cb_ref[:, u * w:(u + 1) * w]
        for j in range(CONV_WIDTH):
            conv = conv + cw_ref[j:j + 1, u * w:(u + 1) * w] * ext[j * nsub:j * nsub + tile]
        act_ref[u] = _silu(conv)

    def act_rows(ci):
        per_sub = span // c
        si, off = ci // per_sub, (ci % per_sub) * c
        return jnp.concatenate(
            [act_ref[u, pl.ds(off * nsub + si, c, stride=nsub), :] for u in tiles], axis=1)

    lane_head = _head_of_lane((c, gw), 1)
    r = lax.broadcasted_iota(jnp.int32, (c, c), 0)
    s = lax.broadcasted_iota(jnp.int32, (c, c), 1)
    for ci in range(tile // c):
        rows = slice(ci * c, (ci + 1) * c)
        xbc = act_rows(ci)
        xs = xbc[:, 0:gw]
        bm = xbc[:, gw:gw + SSD_STATE]
        cm = xbc[:, gw + SSD_STATE:gw + 2 * SSD_STATE]
        gcol = gcol_ref[rows, :]
        dtx = _per_head([gcol[:, h:h + 1] for h in range(N_HEADS)], lane_head)
        bx = _per_head([gcol[:, N_HEADS + h:N_HEADS + h + 1] for h in range(N_HEADS)], lane_head)
        blast = bx[c - 1:c, :]
        v = xs * dtx
        vb = v.astype(BF16)

        sc = lax.dot_general(cm.astype(BF16), bm.astype(BF16), (((1,), (1,)), ((), ())),
                             preferred_element_type=F32)
        o = jnp.zeros((c, gw), F32)
        for h in range(N_HEADS):
            bcol = gcol[:, N_HEADS + h:N_HEADS + h + 1]
            brow = grow_ref[N_HEADS + h:N_HEADS + h + 1, rows]
            decay = jnp.where(s <= r, jnp.exp(jnp.minimum(bcol - brow, 0.0)), 0.0)
            p = (sc * decay).astype(BF16)
            o = o + jnp.where(lane_head == h, jnp.dot(p, vb, preferred_element_type=F32), 0.0)
        st = state_ref[...]
        o = o + jnp.dot(cm.astype(BF16), st.astype(BF16), preferred_element_type=F32) * jnp.exp(bx)
        vd = (v * jnp.exp(blast - bx)).astype(BF16)
        state_ref[...] = (st * jnp.exp(blast)
                          + jnp.dot(bm.T.astype(BF16), vd, preferred_element_type=F32))
        o = o + d_ref[...] * xs
        y = _rmsnorm(o * _silu(z_ref[rows, :]), nw_ref[...])
        o_ref[rows, :] = y.astype(o_ref.dtype)


def _ssd(pf, gcol, grow, conv_w, conv_b, d_x, norm_w, batch, cols):
    n = pf.shape[0]
    s = n // batch
    nt = s // T_MIX
    gw = GROUP_WIDTH
    xbc_w = gw + 2 * SSD_STATE
    const = lambda shape: pl.BlockSpec(shape, lambda b, t: (0, 0))
    return pl.pallas_call(
        _ssd_body,
        grid=(batch, nt),
        in_specs=[
            pl.BlockSpec((T_MIX, xbc_w), lambda b, t: (b * nt + t, cols["sxbc"])),
            pl.BlockSpec((T_MIX, gw), lambda b, t: (b * nt + t, cols["sz"])),
            pl.BlockSpec((T_MIX, 2 * N_HEADS), lambda b, t: (b * nt + t, 0)),
            pl.BlockSpec((2 * N_HEADS, T_MIX), lambda b, t: (0, b * nt + t)),
            const((CONV_WIDTH, xbc_w)), const((1, xbc_w)), const((1, gw)), const((1, gw)),
        ],
        out_specs=pl.BlockSpec((T_MIX, gw), lambda b, t: (b * nt + t, 0)),
        out_shape=jax.ShapeDtypeStruct((n, gw), BF16),
        scratch_shapes=[
            pltpu.VMEM((xbc_w // V7X_LANES, T_MIX + V7X_SUBLANES * V7X_SUBLANES, V7X_LANES), F32),
            pltpu.VMEM((xbc_w // V7X_LANES, T_MIX, V7X_LANES), F32),
            pltpu.VMEM((xbc_w // V7X_LANES, V7X_SUBLANES, V7X_LANES), F32),
            pltpu.VMEM((SSD_STATE, gw), F32),
        ],
        compiler_params=_cparams(("parallel", "arbitrary"), VMEM_MIXER),
        name="ssd",
    )(pf, pf, gcol, grow, conv_w, conv_b.reshape(1, xbc_w), d_x, norm_w.reshape(1, gw))


def _hgrn_body(hq_ref, hf_ref, hi_ref, hg_ref, lbw_ref, nw_ref, o_ref,
               state_ref, arr_ref, acc_ref, stage_ref, inter_ref, dec_ref, bd_ref, *, layer):
    c, nch = C_HGRN, V7X_SUBLANES
    tile = c * nch
    gw = GROUP_WIDTH
    w = V7X_LANES
    tiles = range(gw // w)
    pitch = stage_ref.shape[1] // nch
    k_, b_, v_, q_ = range(4)

    @pl.when(pl.program_id(1) == 0)
    def _():
        state_ref[...] = jnp.zeros_like(state_ref)
        arr_ref[:, :, 0:nch, :] = jnp.zeros((4, gw // w, nch, w), F32)
        bd_ref[...] = jnp.where(_same_head((gw, gw)), 1.0, 0.0).astype(BF16)

    def interleaved(ref):
        for u in tiles:
            for ci in range(nch):
                stage_ref[u, ci * pitch:ci * pitch + c, :] = ref[ci * c:(ci + 1) * c, u * w:(u + 1) * w]
        return jnp.concatenate(
            [jnp.concatenate([stage_ref[u, pl.ds(t, nch, stride=pitch), :] for t in range(c)], axis=0)
             for u in tiles], axis=1)

    def put(slot, value):
        for u in tiles:
            arr_ref[slot, u, nch:nch + tile, :] = value[:, u * w:(u + 1) * w]

    def get(slot, start, rows):
        return jnp.concatenate([arr_ref[slot, u, start:start + rows, :] for u in tiles], axis=1)

    lbw = lbw_ref[...]
    e = jnp.exp(lbw - jnp.max(lbw, axis=0, keepdims=True))
    sm = e / jnp.sum(e, axis=0, keepdims=True)
    cs = sm[0:1, :]
    first = cs
    for d in range(1, layer + 1):
        cs = cs + sm[d:d + 1, :]
    lb = cs - first
    log_f = jnp.logaddexp(jnp.log(jnp.maximum(lb, 0.0)),
                          jnp.log1p(-lb) + jax.nn.log_sigmoid(interleaved(hf_ref)))
    k = 1.0 - jnp.exp(log_f)
    b = log_f
    shift = 1
    while shift < c:
        rows = shift * nch
        b = b + jnp.concatenate([jnp.zeros((rows, gw), F32), b[:tile - rows]], axis=0)
        shift *= 2
    b2 = b * LOG2E
    q = interleaved(hq_ref)
    put(k_, k)
    put(b_, b2)
    put(v_, interleaved(hi_ref))
    put(q_, q)
    bd = bd_ref[...]

    blast = b2[tile - nch:tile]
    qe = q * jnp.exp2(b2)
    kd = k * jnp.exp2(jnp.concatenate([blast] * c, axis=0) - b2)
    for u in tiles:
        dec_ref[0, u] = qe[:, u * w:(u + 1) * w]
        dec_ref[1, u] = kd[:, u * w:(u + 1) * w]
    decay = jnp.exp2(blast)

    def dec_chunk(slot, ci):
        return jnp.concatenate([dec_ref[slot, u, pl.ds(ci, c, stride=nch), :] for u in tiles], axis=1)

    def state_step(ci):
        rows = slice(ci * c, (ci + 1) * c)
        st = state_ref[...]
        inter_ref[rows, :] = lax.dot_general(dec_chunk(0, ci).astype(BF16), st.astype(BF16),
                                             (((1,), (1,)), ((), ())), preferred_element_type=F32)
        upd = jnp.dot(hi_ref[rows, :].T.astype(BF16), dec_chunk(1, ci).astype(BF16),
                      preferred_element_type=F32)
        state_ref[...] = st * decay[ci:ci + 1] + jnp.where(_same_head((gw, gw)), upd, 0.0)

    steps_apart = (c // 2) // nch
    for m in range(c // 2):
        if m % steps_apart == 0:
            state_step(m // steps_apart)
        lo = nch * (1 + 2 * m)
        rows = tile + nch - lo
        qs = get(q_, lo, rows)
        bq = get(b_, lo, rows)
        starts = [lo - nch * d for d in (2 * m, 2 * m + 1)]
        wts = [qs * get(k_, s0, rows) * jnp.exp2(bq - get(b_, s0, rows)) for s0 in starts]
        seg = jnp.dot(jnp.concatenate(wts, axis=0).astype(BF16), bd, preferred_element_type=F32)
        term = sum(seg[j * rows:(j + 1) * rows] * get(v_, s0, rows) for j, s0 in enumerate(starts))
        for u in tiles:
            piece = term[:, u * w:(u + 1) * w]
            if m == 0:
                acc_ref[u] = piece
            else:
                acc_ref[u, lo - nch:tile, :] = acc_ref[u, lo - nch:tile, :] + piece

    intra = jnp.concatenate(
        [jnp.concatenate([acc_ref[u, pl.ds(ci, c, stride=nch), :] for u in tiles], axis=1)
         for ci in range(nch)], axis=0)
    y = _head_rmsnorm(intra + inter_ref[...], nw_ref[...], bd) * _silu(hg_ref[...])
    o_ref[...] = y.astype(o_ref.dtype)


def _hgrn(pf, lower_bounds, norm_w, layer, batch, cols):
    n = pf.shape[0]
    s = n // batch
    nch = V7X_SUBLANES
    tile = C_HGRN * nch
    nt = s // tile
    gw = GROUP_WIDTH
    lane_tiles = gw // V7X_LANES
    depth = lower_bounds.shape[0]

    def tok(col):
        return pl.BlockSpec((tile, gw), lambda b, t: (b * nt + t, col))

    return pl.pallas_call(
        functools.partial(_hgrn_body, layer=layer),
        grid=(batch, nt),
        in_specs=[tok(cols["hq"]), tok(cols["hf"]), tok(cols["hi"]), tok(cols["hg"]),
                  pl.BlockSpec((depth, gw), lambda b, t: (0, 0)),
                  pl.BlockSpec((1, gw), lambda b, t: (0, 0))],
        out_specs=pl.BlockSpec((tile, gw), lambda b, t: (b * nt + t, 0)),
        out_shape=jax.ShapeDtypeStruct((n, gw), BF16),
        scratch_shapes=[
            pltpu.VMEM((gw, gw), F32),
            pltpu.VMEM((4, lane_tiles, nch + tile, V7X_LANES), F32),
            pltpu.VMEM((lane_tiles, tile, V7X_LANES), F32),
            pltpu.VMEM((lane_tiles, nch * (C_HGRN + nch), V7X_LANES), F32),
            pltpu.VMEM((tile, gw), F32),
            pltpu.VMEM((2, lane_tiles, tile, V7X_LANES), F32),
            pltpu.VMEM((gw, gw), BF16),
        ],
        compiler_params=_cparams(("parallel", "arbitrary"), VMEM_MIXER),
        name="hgrn2",
    )(pf, pf, pf, pf, lower_bounds, norm_w.reshape(1, gw))


def _post_body(h_ref, y0_ref, y1_ref, y2_ref, y3_ref, p_ref, wo_ref, fnw_ref, wup_ref, wdn_ref,
               pnw_ref, wg_ref, wp_ref, fw_ref, o_ref, act_ref, *, final):
    gw = GROUP_WIDTH
    h = h_ref[...]
    for m, y_ref in enumerate((y0_ref, y1_ref, y2_ref, y3_ref)):
        h = h + jnp.dot(y_ref[...], wo_ref[m * gw:(m + 1) * gw, :], preferred_element_type=F32)
    h = _ffn_math(h, fnw_ref, wup_ref, wdn_ref, act_ref)
    x = _rmsnorm(h, pnw_ref[...]).astype(BF16)
    gate = jax.nn.sigmoid(jnp.dot(x, wg_ref[...], preferred_element_type=F32))
    emb = jnp.dot(p_ref[...].astype(BF16), wp_ref[...], preferred_element_type=F32)
    h = h + gate * emb
    if final:
        h = _rmsnorm(h, fw_ref[...])
    o_ref[...] = h


def _post(h, ys, p, w_out, ffn_norm, w_up, w_down, ple_norm, w_gate, w_proj, final_w, final):
    n, d = h.shape
    gw = GROUP_WIDTH
    pd = p.shape[1]
    d_ff = w_down.shape[0]
    tok = pl.BlockSpec((TM_FFN, d), lambda i: (i, 0))
    ytok = pl.BlockSpec((TM_FFN, gw), lambda i: (i, 0))
    return pl.pallas_call(
        functools.partial(_post_body, final=final),
        grid=(n // TM_FFN,),
        in_specs=[tok, ytok, ytok, ytok, ytok, pl.BlockSpec((TM_FFN, pd), lambda i: (i, 0)),
                  _resident(w_out.shape), _resident((1, d)), _resident((d, 2 * d_ff)),
                  _resident((d_ff, d)), _resident((1, d)), _resident((d, d)), _resident((pd, d)),
                  _resident((1, d))],
        out_specs=tok,
        out_shape=jax.ShapeDtypeStruct((n, d), F32),
        scratch_shapes=[pltpu.VMEM((TM_FFN, d_ff), BF16)],
        compiler_params=_cparams(("parallel",), VMEM_DENSE),
        name="post",
    )(h, *ys, p, w_out, ffn_norm.reshape(1, d), w_up, w_down, ple_norm.reshape(1, d),
      w_gate, w_proj, final_w.reshape(1, d))


def _split_w_in(w_in):
    gw = GROUP_WIDTH
    xbc_w = gw + 2 * SSD_STATE
    names = ["fq", "fk", "fv", "ff", "rq", "rk", "rv", "rg", "sz", "sxbc", "sdt", "hq", "hf", "hi", "hg"]
    sizes = [gw, gw, gw, N_HEADS, gw, gw, gw, gw, gw, xbc_w, N_HEADS, gw, gw, gw, gw]
    offs = dict(zip(names, np.concatenate([[0], np.cumsum(sizes)[:-1]]).tolist()))
    width = dict(zip(names, sizes))
    w_in = w_in.astype(BF16)

    def gather(group):
        return jnp.concatenate([w_in[:, offs[k]:offs[k] + width[k]] for k in group], axis=1)

    b_group = ["fv", "rv"]
    f_group = ["rq", "rk", "rg", "sz", "sxbc", "hq", "hf", "hi", "hg"]
    cols = {}
    for group in (b_group, f_group):
        pos = 0
        for k in group:
            cols[k] = pos // (xbc_w if k == "sxbc" else gw)
            pos += width[k]
    wqkt = gather(["fq", "fk"]).T
    wst = gather(["ff", "sdt"]).T
    return gather(b_group), gather(f_group), wqkt, wst, cols


def kernel(x, p, ffn1_norm, ffn1_w_up, ffn1_w_down, mix_norm, w_in, fox_f_bias, ret_norm, conv_w, conv_b, dt_bias, a_log, ssd_d, ssd_norm, hgrn_lower_bounds, hgrn_norm, w_out, ffn2_norm, ffn2_w_up, ffn2_w_down, ple_norm, ple_w_gate, ple_w_proj, final_norm):
    batch, s, d = x.shape
    depth = p.shape[0]
    n = batch * s
    h = x.reshape(n, d)
    cos_t, sin_t = _rope_tables(s)
    zeros4 = jnp.zeros((N_HEADS,), F32)
    for i in range(depth):
        h = _ffn(h, ffn1_norm[i], ffn1_w_up[i].astype(BF16), ffn1_w_down[i].astype(BF16))

        wb, wf, wqkt, wst, cols = _split_w_in(w_in[i])
        pb, pf, qt, kt, st = _inproj(h, mix_norm[i], wb, wf, wqkt, wst)
        bias = jnp.concatenate([fox_f_bias[i], dt_bias[i]]).reshape(2 * N_HEADS, 1)
        alog = jnp.concatenate([zeros4, a_log[i]]).reshape(2 * N_HEADS, 1)
        g_row, dt_row = _gates(st, bias, alog, batch)
        c_pairs = g_row[:N_HEADS].reshape(N_HEADS // 2, 2, n)
        gcol = jnp.concatenate([dt_row[N_HEADS:], g_row[N_HEADS:]], axis=0).T
        d_x = jnp.repeat(ssd_d[i], HEAD_DIM).reshape(1, GROUP_WIDTH)

        y_fox = _fox(qt, kt, pb, c_pairs, _fox_plan(qt, kt, g_row, batch), batch, cols)
        y_ret = _retention(pf, pb, cos_t, sin_t, ret_norm[i], batch, cols)
        y_ssd = _ssd(pf, gcol, g_row, conv_w[i], conv_b[i], d_x, ssd_norm[i], batch, cols)
        y_hg = _hgrn(pf, hgrn_lower_bounds, hgrn_norm[i], i, batch, cols)
        h = _post(h, (y_fox, y_ret, y_ssd, y_hg), p[i].reshape(n, -1), w_out[i].astype(BF16),
                  ffn2_norm[i], ffn2_w_up[i].astype(BF16), ffn2_w_down[i].astype(BF16),
                  ple_norm[i], ple_w_gate[i].astype(BF16), ple_w_proj[i].astype(BF16),
                  final_norm, final=(i == depth - 1))
    return h.reshape(batch, s, d)
```

```python
import functools
import math

import jax
import jax.numpy as jnp
import numpy as np
from jax import lax
from jax.experimental import pallas as pl
from jax.experimental.pallas import tpu as pltpu

F32 = jnp.float32
BF16 = jnp.bfloat16

HEAD_DIM = 64
N_HEADS = 4
GROUP_WIDTH = N_HEADS * HEAD_DIM
SSD_STATE = 128
CONV_WIDTH = 4
RMS_EPS = 1e-6
ROPE_BASE = 10000.0
LOG2E = 1.4426950408889634

V7X_LANES = 128
V7X_SUBLANES = 8
V7X_VMEM_BYTES = 64 * 1024 * 1024
MIB = 1024 * 1024

TM_FFN = 1024
FF_CHUNK = 256
TQ_FOX = 512
C_RET = 128
C_SSD = 128
C_HGRN = 32
T_MIX = 1024
TL_GATES = 2048


VMEM_DENSE = 56
VMEM_FOX = 48
VMEM_MIXER = 24
VMEM_SMALL = 16


def _cparams(semantics, vmem_mib):
    assert vmem_mib * MIB < V7X_VMEM_BYTES
    return pltpu.CompilerParams(dimension_semantics=semantics, vmem_limit_bytes=vmem_mib * MIB)


def _rmsnorm(x, w):
    return x * lax.rsqrt(jnp.mean(x * x, axis=-1, keepdims=True) + RMS_EPS) * w


def _silu(x):
    return x * jax.nn.sigmoid(x)


def _resident(shape, layer=None):
    if layer is None:
        return pl.BlockSpec(shape, lambda *_: (0,) * len(shape), pipeline_mode=pl.Buffered(1))
    return pl.BlockSpec((None,) + tuple(shape), lambda *_: (layer,) + (0,) * len(shape),
                        pipeline_mode=pl.Buffered(1))


def _ffn_math(h, nw_ref, wup_ref, wdn_ref, act_ref):
    d_ff = wdn_ref.shape[0]
    x = _rmsnorm(h, nw_ref[...]).astype(BF16)
    for c in range(d_ff // FF_CHUNK):
        lo = c * FF_CHUNK
        gate = jnp.dot(x, wup_ref[:, lo:lo + FF_CHUNK], preferred_element_type=F32)
        up = jnp.dot(x, wup_ref[:, d_ff + lo:d_ff + lo + FF_CHUNK], preferred_element_type=F32)
        act_ref[:, lo:lo + FF_CHUNK] = (_silu(gate) * up).astype(BF16)
    return h + 0.5 * jnp.dot(act_ref[...], wdn_ref[...], preferred_element_type=F32)


def _ffn_body(h_ref, nw_ref, wup_ref, wdn_ref, o_ref, act_ref):
    o_ref[...] = _ffn_math(h_ref[...], nw_ref, wup_ref, wdn_ref, act_ref)


def _ffn(h, norm_w, w_up, w_down, layer):
    n, d = h.shape
    d_ff = w_down.shape[1]
    return pl.pallas_call(
        _ffn_body,
        grid=(n // TM_FFN,),
        in_specs=[
            pl.BlockSpec((TM_FFN, d), lambda i: (i, 0)),
            _resident((1, d)),
            _resident((d, 2 * d_ff), layer),
            _resident((d_ff, d), layer),
        ],
        out_specs=pl.BlockSpec((TM_FFN, d), lambda i: (i, 0)),
        out_shape=jax.ShapeDtypeStruct((n, d), F32),
        scratch_shapes=[pltpu.VMEM((TM_FFN, d_ff), BF16)],
        compiler_params=_cparams(("parallel",), VMEM_DENSE),
        name="ffn",
    )(h, norm_w.reshape(1, d), w_up, w_down)


def _inproj_body(h_ref, nw_ref, wb_ref, wf_ref, wqkt_ref, wst_ref,
                 pb_ref, pf_ref, qt_ref, kt_ref, st_ref, *, q_scale):
    x = _rmsnorm(h_ref[...], nw_ref[...]).astype(BF16)
    gw = GROUP_WIDTH
    for c in range(wb_ref.shape[1] // gw):
        pb_ref[:, c * gw:(c + 1) * gw] = jnp.dot(x, wb_ref[:, c * gw:(c + 1) * gw],
                                                preferred_element_type=F32).astype(BF16)
    for c in range(wf_ref.shape[1] // gw):
        pf_ref[:, c * gw:(c + 1) * gw] = jnp.dot(x, wf_ref[:, c * gw:(c + 1) * gw],
                                                preferred_element_type=F32)
    nt = (((1,), (1,)), ((), ()))
    qk = lax.dot_general(wqkt_ref[...], x, nt, preferred_element_type=F32)
    qt_ref[...] = (qk[0:gw] * q_scale).astype(BF16)
    kt_ref[...] = qk[gw:2 * gw].astype(BF16)
    st_ref[...] = lax.dot_general(wst_ref[...], x, nt, preferred_element_type=F32)


def _inproj(h, norm_w, wb, wf, wqkt, wst):
    n, d = h.shape
    nb, nf = wb.shape[1], wf.shape[1]
    gw = GROUP_WIDTH
    lanes_out = pl.BlockSpec((gw, TM_FFN), lambda i: (0, i))
    return pl.pallas_call(
        functools.partial(_inproj_body, q_scale=HEAD_DIM ** -0.5 * LOG2E),
        grid=(n // TM_FFN,),
        in_specs=[
            pl.BlockSpec((TM_FFN, d), lambda i: (i, 0)),
            _resident((1, d)),
            _resident((d, nb)),
            _resident((d, nf)),
            _resident((2 * gw, d)),
            _resident((2 * N_HEADS, d)),
        ],
        out_specs=[
            pl.BlockSpec((TM_FFN, nb), lambda i: (i, 0)),
            pl.BlockSpec((TM_FFN, nf), lambda i: (i, 0)),
            lanes_out, lanes_out,
            pl.BlockSpec((2 * N_HEADS, TM_FFN), lambda i: (0, i)),
        ],
        out_shape=[
            jax.ShapeDtypeStruct((n, nb), BF16),
            jax.ShapeDtypeStruct((n, nf), F32),
            jax.ShapeDtypeStruct((gw, n), BF16),
            jax.ShapeDtypeStruct((gw, n), BF16),
            jax.ShapeDtypeStruct((2 * N_HEADS, n), F32),
        ],
        compiler_params=_cparams(("parallel",), VMEM_DENSE),
        name="inproj",
    )(h, norm_w.reshape(1, d), wb, wf, wqkt, wst)


def _gates_body(st_ref, bias_ref, alog_ref, g_ref, dt_ref, carry_ref):
    @pl.when(pl.program_id(1) == 0)
    def _():
        carry_ref[...] = jnp.zeros_like(carry_ref)

    x = st_ref[...] + bias_ref[...]
    tl = x.shape[1]
    row = lax.broadcasted_iota(jnp.int32, x.shape, 0)
    lane = lax.broadcasted_iota(jnp.int32, x.shape, 1)
    lane_in_chunk = lane % C_SSD
    is_fox = row < N_HEADS
    dt = jax.nn.softplus(x)
    val = jnp.where(is_fox, jax.nn.log_sigmoid(x), dt * (-jnp.exp(alog_ref[...])))
    full, local = val, val
    shift = 1
    while shift < tl:
        full = full + jnp.where(lane >= shift, pltpu.roll(full, shift, 1), 0.0)
        if shift < C_SSD:
            local = local + jnp.where(lane_in_chunk >= shift, pltpu.roll(local, shift, 1), 0.0)
        shift *= 2
    cum = full + carry_ref[...]
    carry_ref[...] = jnp.sum(jnp.where(lane == tl - 1, cum, 0.0), axis=1, keepdims=True)
    g_ref[...] = jnp.where(is_fox, cum * LOG2E, local)
    dt_ref[...] = dt


def _gates(st, bias, alog, batch):
    rows, n = st.shape
    s = n // batch
    nt = s // TL_GATES
    spec = pl.BlockSpec((rows, TL_GATES), lambda b, t: (0, b * nt + t))
    col = pl.BlockSpec((rows, 1), lambda b, t: (0, 0))
    return pl.pallas_call(
        _gates_body,
        grid=(batch, nt),
        in_specs=[spec, col, col],
        out_specs=[spec] * 2,
        out_shape=[jax.ShapeDtypeStruct((rows, n), F32)] * 2,
        scratch_shapes=[pltpu.VMEM((rows, 1), F32)],
        compiler_params=_cparams(("parallel", "arbitrary"), VMEM_SMALL),
        name="gates",
    )(st, bias, alog)


def _rope_body(inv_ref, cos_ref, sin_ref):
    rows = cos_ref.shape[0]
    pos = (lax.broadcasted_iota(jnp.int32, cos_ref.shape, 0) + pl.program_id(0) * rows).astype(F32)
    lane = lax.broadcasted_iota(jnp.int32, cos_ref.shape, 1)
    ang = pos * inv_ref[...]
    cos_ref[...] = jnp.cos(ang)
    sin = jnp.sin(ang)
    sin_ref[...] = jnp.where(lane % HEAD_DIM < HEAD_DIM // 2, -sin, sin)


def _rope_tables(s):
    half = HEAD_DIM // 2
    inv_freq = ROPE_BASE ** (-jnp.arange(half, dtype=F32) / half)
    inv = jnp.tile(inv_freq, V7X_LANES // half).reshape(1, V7X_LANES)
    rows = 1024
    return pl.pallas_call(
        _rope_body,
        grid=(s // rows,),
        in_specs=[pl.BlockSpec((1, V7X_LANES), lambda i: (0, 0))],
        out_specs=[pl.BlockSpec((rows, V7X_LANES), lambda i: (i, 0))] * 2,
        out_shape=[jax.ShapeDtypeStruct((s, V7X_LANES), F32)] * 2,
        compiler_params=_cparams(("parallel",), VMEM_SMALL),
        name="rope_tables",
    )(inv)


FOX_NEG = -1e30


FOX_SKIP_LOG2 = 160.0
FOX_FAST_LOG2 = 80.0
FOX_BOUND_SLACK = 1.01


def _fox_plan_body(qt_ref, kt_ref, c_ref, plan_ref, *, tq):
    gw, s = qt_ref.shape
    nblk = s // tq
    lanes = plan_ref.shape[-1]

    def head_norm2(x_ref):
        x = x_ref[...].astype(F32)
        x = x * x
        return jnp.concatenate([jnp.sum(x[h * HEAD_DIM:(h + 1) * HEAD_DIM], axis=0, keepdims=True)
                                for h in range(N_HEADS)], axis=0)

    qn2 = head_norm2(qt_ref)
    kmax2 = jnp.max(head_norm2(kt_ref), axis=1, keepdims=True)
    c = c_ref[0:N_HEADS, :]
    lane = lax.broadcasted_iota(jnp.int32, (N_HEADS, lanes), 1)
    qmax2 = jnp.zeros((N_HEADS, lanes), F32)
    cmax = jnp.zeros((N_HEADS, lanes), F32)
    cmin = jnp.full((N_HEADS, lanes), -jnp.inf, F32)
    for j in range(nblk):
        blk = slice(j * tq, (j + 1) * tq)
        qmax2 = jnp.where(lane == j, jnp.max(qn2[:, blk], axis=1, keepdims=True), qmax2)
        cmax = jnp.where(lane == j, jnp.max(c[:, blk], axis=1, keepdims=True), cmax)
        cmin = jnp.where(lane == j, jnp.min(c[:, blk], axis=1, keepdims=True), cmin)
    shift = 1
    while shift < nblk:
        cmin = jnp.minimum(cmin, jnp.where(lane >= shift, pltpu.roll(cmin, shift, 1), jnp.inf))
        shift *= 2
    spread = 2.0 * FOX_BOUND_SLACK * jnp.sqrt(qmax2 * kmax2)
    limit = cmax + spread + FOX_SKIP_LOG2
    skip = jnp.zeros((N_HEADS, lanes), F32)
    for i in range(nblk):
        lim_i = jnp.sum(jnp.where(lane == i, limit, 0.0), axis=1, keepdims=True)
        n_i = jnp.sum(jnp.where(cmin > lim_i, 1.0, 0.0), axis=1, keepdims=True)
        skip = jnp.where(lane == i, n_i, skip)
    fast = jnp.where(spread <= FOX_FAST_LOG2, 1.0, 0.0)
    rows = []
    for p in range(N_HEADS // 2):
        rows.append(jnp.minimum(skip[2 * p:2 * p + 1], skip[2 * p + 1:2 * p + 2]))
    for p in range(N_HEADS // 2):
        rows.append(jnp.minimum(fast[2 * p:2 * p + 1], fast[2 * p + 1:2 * p + 2]))
    rows.append(jnp.zeros((V7X_SUBLANES - len(rows), lanes), F32))
    plan_ref[...] = jnp.concatenate(rows, axis=0).astype(jnp.int32)


def _fox_plan(qt, kt, g_row, batch):
    gw, n = qt.shape
    s = n // batch
    lanes = V7X_LANES
    assert s // TQ_FOX <= lanes
    seq = pl.BlockSpec((gw, s), lambda b: (0, b))
    return pl.pallas_call(
        functools.partial(_fox_plan_body, tq=TQ_FOX),
        grid=(batch,),
        in_specs=[seq, seq, pl.BlockSpec((g_row.shape[0], s), lambda b: (0, b))],
        out_specs=pl.BlockSpec((None, V7X_SUBLANES, lanes), lambda b: (b, 0, 0)),
        out_shape=jax.ShapeDtypeStruct((batch, V7X_SUBLANES, lanes), jnp.int32),
        compiler_params=_cparams(("parallel",), VMEM_DENSE),
        name="fox_plan",
    )(qt, kt, g_row)


def _fox_body(skip_ref, fast_ref, qt_ref, kt_ref, v_ref, c_ref, o_ref,
              ref_ref, l_ref, acc_ref, m_ref, ls_ref):
    b, pair, i = pl.program_id(0), pl.program_id(1), pl.program_id(2)
    w, tq = qt_ref.shape
    tk = 2 * tq
    idx = (b * pl.num_programs(1) + pair) * pl.num_programs(2) + i
    skip = skip_ref[idx]
    q = qt_ref[...].astype(F32).T.astype(BF16)
    first = lax.broadcasted_iota(jnp.int32, (tq, w), 1) < HEAD_DIM
    zero = jnp.zeros_like(q)
    qs = jnp.concatenate([jnp.where(first, q, zero), jnp.where(first, zero, q)], axis=0)

    def scores(start, size):
        s = jnp.dot(qs, kt_ref[:, pl.ds(start, size)], preferred_element_type=F32)
        return [s[hh * tq:(hh + 1) * tq] - c_ref[hh:hh + 1, pl.ds(start, size)] for hh in range(2)]

    def weighted_values(start, size, p):
        stacked = jnp.concatenate(p, axis=0).astype(BF16)
        return jnp.dot(stacked, v_ref[pl.ds(start, size), :], preferred_element_type=F32)

    def earlier_blocks(block):
        def body(j, carry):
            block(pl.multiple_of(j * tk, tk), tk)
            return carry

        lax.fori_loop(skip // 2, i // 2, body, 0)

        @pl.when(jnp.logical_and(i % 2 == 1, skip < i))
        def _():
            block(pl.multiple_of((i - 1) * tq, tq), tq)

    diag_start = pl.multiple_of(i * tq, tq)
    qry = lax.broadcasted_iota(jnp.int32, (tq, tq), 0)
    key = lax.broadcasted_iota(jnp.int32, (tq, tq), 1)

    @pl.when(fast_ref[idx] == 1)
    def _():
        p = []
        for hh, s in enumerate(scores(diag_start, tq)):
            ref = jnp.sum(jnp.where(key == qry, s, 0.0), axis=1, keepdims=True)
            ref_ref[hh] = jnp.broadcast_to(ref, (tq, w))
            p.append(jnp.exp2(jnp.where(key <= qry, s - ref, FOX_NEG)))
            l_ref[hh] = sum(p[hh][:, u * w:(u + 1) * w] for u in range(tq // w))
        acc_ref[...] = weighted_values(diag_start, tq, p)

        def block(start, size):
            p = []
            for hh, s in enumerate(scores(start, size)):
                ref = ref_ref[hh]
                tiles = [jnp.exp2(s[:, u * w:(u + 1) * w] - ref) for u in range(size // w)]
                l_ref[hh] = l_ref[hh] + sum(tiles)
                p.append(jnp.concatenate(tiles, axis=1))
            acc_ref[...] = acc_ref[...] + weighted_values(start, size, p)

        earlier_blocks(block)

    @pl.when(fast_ref[idx] != 1)
    def _():
        m_ref[...] = jnp.full(m_ref.shape, FOX_NEG, F32)
        ls_ref[...] = jnp.zeros_like(ls_ref)
        acc_ref[...] = jnp.zeros_like(acc_ref)

        def update(start, size, s_heads):
            p, alpha = [], []
            for hh, s in enumerate(s_heads):
                m_old = m_ref[hh]
                m_new = jnp.maximum(m_old, jnp.max(s, axis=1, keepdims=True))
                alpha.append(jnp.exp2(m_old - m_new))
                p.append(jnp.exp2(s - m_new))
                ls_ref[hh] = alpha[hh] * ls_ref[hh] + jnp.sum(p[hh], axis=1, keepdims=True)
                m_ref[hh] = m_new
            acc_ref[...] = (jnp.concatenate(alpha, axis=0) * acc_ref[...]
                            + weighted_values(start, size, p))

        earlier_blocks(lambda start, size: update(start, size, scores(start, size)))
        update(diag_start, tq, [jnp.where(key <= qry, s, FOX_NEG) for s in scores(diag_start, tq)])
        lane0 = lax.broadcasted_iota(jnp.int32, (tq, w), 1) == 0
        for hh in range(2):
            l_ref[hh] = jnp.where(lane0, ls_ref[hh], 0.0)

    l = [jnp.sum(l_ref[hh], axis=1, keepdims=True) for hh in range(2)]
    acc = acc_ref[...]
    o_ref[...] = jnp.where(first, acc[0:tq] / l[0], acc[tq:2 * tq] / l[1]).astype(o_ref.dtype)


def _fox(qt, kt, pb, c_pairs, plan, batch, cols):
    n = pb.shape[0]
    s = n // batch
    nq = s // TQ_FOX
    w = V7X_LANES
    pairs = GROUP_WIDTH // w
    skip = plan[:, 0:pairs, 0:nq].reshape(-1)
    fast = plan[:, pairs:2 * pairs, 0:nq].reshape(-1)
    fv = cols["fv"] * pairs
    return pl.pallas_call(
        _fox_body,
        grid_spec=pltpu.PrefetchScalarGridSpec(
            num_scalar_prefetch=2,
            grid=(batch, pairs, nq),
            in_specs=[
                pl.BlockSpec((w, TQ_FOX), lambda b, p, i, *_: (p, b * nq + i)),
                pl.BlockSpec((w, s), lambda b, p, i, *_: (p, b)),
                pl.BlockSpec((s, w), lambda b, p, i, *_: (b, fv + p)),
                pl.BlockSpec((None, 2, s), lambda b, p, i, *_: (p, 0, b)),
            ],
            out_specs=pl.BlockSpec((TQ_FOX, w), lambda b, p, i, *_: (b * nq + i, p)),
            scratch_shapes=[
                pltpu.VMEM((2, TQ_FOX, w), F32),
                pltpu.VMEM((2, TQ_FOX, w), F32),
                pltpu.VMEM((2 * TQ_FOX, w), F32),
                pltpu.VMEM((2, TQ_FOX, 1), F32),
                pltpu.VMEM((2, TQ_FOX, 1), F32),
            ],
        ),
        out_shape=jax.ShapeDtypeStruct((n, GROUP_WIDTH), BF16),
        compiler_params=_cparams(("parallel", "parallel", "arbitrary"), VMEM_FOX),
        name="fox",
    )(skip, fast, qt, kt, pb, c_pairs)


def _head_of_lane(shape, axis):
    return lax.broadcasted_iota(jnp.int32, shape, axis) // HEAD_DIM


def _same_head(shape):
    return _head_of_lane(shape, 0) == _head_of_lane(shape, 1)


def _per_head(values, lane_head):
    out = values[N_HEADS - 1]
    for h in range(N_HEADS - 2, -1, -1):
        out = jnp.where(lane_head == h, values[h], out)
    return out


def _head_rmsnorm(o, w, ones_bd):
    ms = jnp.dot((o * o).astype(BF16), ones_bd, preferred_element_type=F32) * (1.0 / HEAD_DIM)
    return o * lax.rsqrt(ms + RMS_EPS) * w


def _ret_body(rq_ref, rk_ref, rv_ref, rg_ref, cos_ref, sin_ref, nw_ref, o_ref,
              state_ref, dmat_ref, tab_ref, bd_ref):
    c = C_RET
    gw = GROUP_WIDTH
    log_gamma = [math.log1p(-(2.0 ** (-5.0 - h))) for h in range(N_HEADS)]

    @pl.when(pl.program_id(1) == 0)
    def _():
        state_ref[...] = jnp.zeros_like(state_ref)
        r = lax.broadcasted_iota(jnp.int32, (c, c), 0)
        s = lax.broadcasted_iota(jnp.int32, (c, c), 1)
        dist = (r - s).astype(F32)
        for h in range(N_HEADS):
            dmat_ref[h] = jnp.where(s <= r, jnp.exp(dist * log_gamma[h]), 0.0)
        lg = _per_head(log_gamma, _head_of_lane((c, gw), 1))
        t = lax.broadcasted_iota(jnp.int32, (c, gw), 0).astype(F32)
        tab_ref[0] = jnp.exp((t + 1.0) * lg)
        tab_ref[1] = jnp.exp((c - 1.0 - t) * lg)
        tab_ref[2] = jnp.exp(c * lg)
        bd_ref[...] = jnp.where(_same_head((gw, gw)), 1.0, 0.0).astype(BF16)

    lane = lax.broadcasted_iota(jnp.int32, (c, gw), 1)
    lane_head = lane // HEAD_DIM
    first_half = lane % HEAD_DIM < HEAD_DIM // 2

    for ci in range(rq_ref.shape[0] // c):
        rows = slice(ci * c, (ci + 1) * c)
        cosx = jnp.concatenate([cos_ref[rows, :]] * (gw // V7X_LANES), axis=1)
        sinx = jnp.concatenate([sin_ref[rows, :]] * (gw // V7X_LANES), axis=1)

        def rotary(x):
            swapped = jnp.where(first_half, pltpu.roll(x, gw - HEAD_DIM // 2, 1),
                                pltpu.roll(x, HEAD_DIM // 2, 1))
            return x * cosx + swapped * sinx

        q = rotary(rq_ref[rows, :])
        k = rotary(rk_ref[rows, :]) * (HEAD_DIM ** -0.5)
        v = rv_ref[rows, :]
        qb, kb = q.astype(BF16), k.astype(BF16)
        zero = jnp.zeros_like(qb)
        o = jnp.zeros((c, gw), F32)
        for h in range(N_HEADS):
            sel = lane_head == h
            sc = lax.dot_general(jnp.where(sel, qb, zero), kb, (((1,), (1,)), ((), ())),
                                 preferred_element_type=F32)
            p = (sc * dmat_ref[h]).astype(BF16)
            o = o + jnp.where(sel, jnp.dot(p, v, preferred_element_type=F32), 0.0)
        st = state_ref[...]
        o = o + lax.dot_general(qb, st.astype(BF16), (((1,), (1,)), ((), ())),
                                preferred_element_type=F32) * tab_ref[0]
        kd = (k * tab_ref[1]).astype(BF16)
        vt = v.astype(F32).T.astype(BF16)
        upd = jnp.dot(vt, kd, preferred_element_type=F32)
        state_ref[...] = st * tab_ref[2, 0:1, :] + jnp.where(_same_head((gw, gw)), upd, 0.0)
        y = _head_rmsnorm(o, nw_ref[...], bd_ref[...]) * _silu(rg_ref[rows, :])
        o_ref[rows, :] = y.astype(o_ref.dtype)


def _retention(pf, pb, cos_t, sin_t, norm_w, batch, cols):
    n = pf.shape[0]
    s = n // batch
    nt = s // T_MIX
    gw = GROUP_WIDTH

    def tok(col):
        return pl.BlockSpec((T_MIX, gw), lambda b, t: (b * nt + t, col))

    tab = pl.BlockSpec((T_MIX, V7X_LANES), lambda b, t: (t, 0))
    return pl.pallas_call(
        _ret_body,
        grid=(batch, nt),
        in_specs=[tok(cols["rq"]), tok(cols["rk"]), tok(cols["rv"]), tok(cols["rg"]), tab, tab,
                  pl.BlockSpec((1, gw), lambda b, t: (0, 0))],
        out_specs=pl.BlockSpec((T_MIX, gw), lambda b, t: (b * nt + t, 0)),
        out_shape=jax.ShapeDtypeStruct((n, gw), BF16),
        scratch_shapes=[
            pltpu.VMEM((gw, gw), F32),
            pltpu.VMEM((N_HEADS, C_RET, C_RET), F32),
            pltpu.VMEM((3, C_RET, gw), F32),
            pltpu.VMEM((gw, gw), BF16),
        ],
        compiler_params=_cparams(("parallel", "arbitrary"), VMEM_MIXER),
        name="retention",
    )(pf, pf, pb, pf, cos_t, sin_t, norm_w.reshape(1, gw))


def _ssd_body(xbc_ref, z_ref, gcol_ref, grow_ref, cw_ref, cb_ref, d_ref, nw_ref, o_ref,
              stage_ref, act_ref, tail_ref, state_ref):
    tile = xbc_ref.shape[0]
    c = C_SSD
    gw = GROUP_WIDTH
    w = V7X_LANES
    nsub = V7X_SUBLANES
    span = tile // nsub
    pitch = stage_ref.shape[1] // nsub
    tiles = range(xbc_ref.shape[1] // w)
    taps = CONV_WIDTH - 1

    @pl.when(pl.program_id(1) == 0)
    def _():
        state_ref[...] = jnp.zeros_like(state_ref)
        tail_ref[...] = jnp.zeros_like(tail_ref)

    sub_id = lax.broadcasted_iota(jnp.int32, (nsub, w), 0)
    for u in tiles:
        for si in range(nsub):
            stage_ref[u, si * pitch:si * pitch + span, :] = xbc_ref[si * span:(si + 1) * span,
                                                                    u * w:(u + 1) * w]
        x = jnp.concatenate([stage_ref[u, pl.ds(t, nsub, stride=pitch), :] for t in range(span)],
                            axis=0)
        halo = []
        for j in range(taps):
            grp = x[(span - taps + j) * nsub:(span - taps + j + 1) * nsub]
            prev = tail_ref[u, j:j + 1, :]
            halo.append(jnp.where(sub_id == 0, prev, pltpu.roll(grp, 1, 0)))
            tail_ref[u, j:j + 1, :] = grp[nsub - 1:nsub]
        ext = jnp.concatenate(halo + [x], axis=0)
        conv = cb_ref[:, u * w:(u + 1) * w]
        for j in range(CONV_WIDTH):
            conv = conv + cw_ref[j:j + 1, u * w:(u + 1) * w] * ext[j * nsub:j * nsub + tile]
        act_ref[u] = _silu(conv)

    def act_rows(ci):
        per_sub = span // c
        si, off = ci // per_sub, (ci % per_sub) * c
        return jnp.concatenate(
            [act_ref[u, pl.ds(off * nsub + si, c, stride=nsub), :] for u in tiles], axis=1)

    lane_head = _head_of_lane((c, gw), 1)
    r = lax.broadcasted_iota(jnp.int32, (c, c), 0)
    s = lax.broadcasted_iota(jnp.int32, (c, c), 1)
    for ci in range(tile // c):
        rows = slice(ci * c, (ci + 1) * c)
        xbc = act_rows(ci)
        xs = xbc[:, 0:gw]
        bm = xbc[:, gw:gw + SSD_STATE]
        cm = xbc[:, gw + SSD_STATE:gw + 2 * SSD_STATE]
        gcol = gcol_ref[rows, :]
        dtx = _per_head([gcol[:, h:h + 1] for h in range(N_HEADS)], lane_head)
        bx = _per_head([gcol[:, N_HEADS + h:N_HEADS + h + 1] for h in range(N_HEADS)], lane_head)
        blast = bx[c - 1:c, :]
        v = xs * dtx
        vb = v.astype(BF16)

        sc = lax.dot_general(cm.astype(BF16), bm.astype(BF16), (((1,), (1,)), ((), ())),
                             preferred_element_type=F32)
        o = jnp.zeros((c, gw), F32)
        for h in range(N_HEADS):
            bcol = gcol[:, N_HEADS + h:N_HEADS + h + 1]
            brow = grow_ref[N_HEADS + h:N_HEADS + h + 1, rows]
            decay = jnp.where(s <= r, jnp.exp(jnp.minimum(bcol - brow, 0.0)), 0.0)
            p = (sc * decay).astype(BF16)
            o = o + jnp.where(lane_head == h, jnp.dot(p, vb, preferred_element_type=F32), 0.0)
        st = state_ref[...]
        o = o + jnp.dot(cm.astype(BF16), st.astype(BF16), preferred_element_type=F32) * jnp.exp(bx)
        vd = (v * jnp.exp(blast - bx)).astype(BF16)
        state_ref[...] = (st * jnp.exp(blast)
                          + jnp.dot(bm.T.astype(BF16), vd, preferred_element_type=F32))
        o = o + d_ref[...] * xs
        y = _rmsnorm(o * _silu(z_ref[rows, :]), nw_ref[...])
        o_ref[rows, :] = y.astype(o_ref.dtype)


def _ssd(pf, gcol, grow, conv_w, conv_b, d_x, norm_w, batch, cols):
    n = pf.shape[0]
    s = n // batch
    nt = s // T_MIX
    gw = GROUP_WIDTH
    xbc_w = gw + 2 * SSD_STATE
    const = lambda shape: pl.BlockSpec(shape, lambda b, t: (0, 0))
    return pl.pallas_call(
        _ssd_body,
        grid=(batch, nt),
        in_specs=[
            pl.BlockSpec((T_MIX, xbc_w), lambda b, t: (b * nt + t, cols["sxbc"])),
            pl.BlockSpec((T_MIX, gw), lambda b, t: (b * nt + t, cols["sz"])),
            pl.BlockSpec((T_MIX, 2 * N_HEADS), lambda b, t: (b * nt + t, 0)),
            pl.BlockSpec((2 * N_HEADS, T_MIX), lambda b, t: (0, b * nt + t)),
            const((CONV_WIDTH, xbc_w)), const((1, xbc_w)), const((1, gw)), const((1, gw)),
        ],
        out_specs=pl.BlockSpec((T_MIX, gw), lambda b, t: (b * nt + t, 0)),
        out_shape=jax.ShapeDtypeStruct((n, gw), BF16),
        scratch_shapes=[
            pltpu.VMEM((xbc_w // V7X_LANES, T_MIX + V7X_SUBLANES * V7X_SUBLANES, V7X_LANES), F32),
            pltpu.VMEM((xbc_w // V7X_LANES, T_MIX, V7X_LANES), F32),
            pltpu.VMEM((xbc_w // V7X_LANES, V7X_SUBLANES, V7X_LANES), F32),
            pltpu.VMEM((SSD_STATE, gw), F32),
        ],
        compiler_params=_cparams(("parallel", "arbitrary"), VMEM_MIXER),
        name="ssd",
    )(pf, pf, gcol, grow, conv_w, conv_b.reshape(1, xbc_w), d_x, norm_w.reshape(1, gw))


def _hgrn_body(hq_ref, hf_ref, hi_ref, hg_ref, lbw_ref, nw_ref, o_ref,
               state_ref, arr_ref, acc_ref, stage_ref, inter_ref, dec_ref, bd_ref, *, layer):
    c, nch = C_HGRN, V7X_SUBLANES
    tile = c * nch
    gw = GROUP_WIDTH
    w = V7X_LANES
    tiles = range(gw // w)
    pitch = stage_ref.shape[1] // nch
    k_, b_, v_, q_ = range(4)

    @pl.when(pl.program_id(1) == 0)
    def _():
        state_ref[...] = jnp.zeros_like(state_ref)
        arr_ref[:, :, 0:nch, :] = jnp.zeros((4, gw // w, nch, w), F32)
        bd_ref[...] = jnp.where(_same_head((gw, gw)), 1.0, 0.0).astype(BF16)

    def interleaved(ref):
        for u in tiles:
            for ci in range(nch):
                stage_ref[u, ci * pitch:ci * pitch + c, :] = ref[ci * c:(ci + 1) * c, u * w:(u + 1) * w]
        return jnp.concatenate(
            [jnp.concatenate([stage_ref[u, pl.ds(t, nch, stride=pitch), :] for t in range(c)], axis=0)
             for u in tiles], axis=1)

    def put(slot, value):
        for u in tiles:
            arr_ref[slot, u, nch:nch + tile, :] = value[:, u * w:(u + 1) * w]

    def get(slot, start, rows):
        return jnp.concatenate([arr_ref[slot, u, start:start + rows, :] for u in tiles], axis=1)

    lbw = lbw_ref[...]
    e = jnp.exp(lbw - jnp.max(lbw, axis=0, keepdims=True))
    sm = e / jnp.sum(e, axis=0, keepdims=True)
    cs = sm[0:1, :]
    first = cs
    for d in range(1, layer + 1):
        cs = cs + sm[d:d + 1, :]
    lb = cs - first
    log_f = jnp.logaddexp(jnp.log(jnp.maximum(lb, 0.0)),
                          jnp.log1p(-lb) + jax.nn.log_sigmoid(interleaved(hf_ref)))
    k = 1.0 - jnp.exp(log_f)
    b = log_f
    shift = 1
    while shift < c:
        rows = shift * nch
        b = b + jnp.concatenate([jnp.zeros((rows, gw), F32), b[:tile - rows]], axis=0)
        shift *= 2
    b2 = b * LOG2E
    q = interleaved(hq_ref)
    put(k_, k)
    put(b_, b2)
    put(v_, interleaved(hi_ref))
    put(q_, q)
    bd = bd_ref[...]

    blast = b2[tile - nch:tile]
    qe = q * jnp.exp2(b2)
    kd = k * jnp.exp2(jnp.concatenate([blast] * c, axis=0) - b2)
    for u in tiles:
        dec_ref[0, u] = qe[:, u * w:(u + 1) * w]
        dec_ref[1, u] = kd[:, u * w:(u + 1) * w]
    decay = jnp.exp2(blast)

    def dec_chunk(slot, ci):
        return jnp.concatenate([dec_ref[slot, u, pl.ds(ci, c, stride=nch), :] for u in tiles], axis=1)

    def state_step(ci):
        rows = slice(ci * c, (ci + 1) * c)
        st = state_ref[...]
        inter_ref[rows, :] = lax.dot_general(dec_chunk(0, ci).astype(BF16), st.astype(BF16),
                                             (((1,), (1,)), ((), ())), preferred_element_type=F32)
        upd = jnp.dot(hi_ref[rows, :].T.astype(BF16), dec_chunk(1, ci).astype(BF16),
                      preferred_element_type=F32)
        state_ref[...] = st * decay[ci:ci + 1] + jnp.where(_same_head((gw, gw)), upd, 0.0)

    steps_apart = (c // 2) // nch
    for m in range(c // 2):
        if m % steps_apart == 0:
            state_step(m // steps_apart)
        lo = nch * (1 + 2 * m)
        rows = tile + nch - lo
        qs = get(q_, lo, rows)
        bq = get(b_, lo, rows)
        starts = [lo - nch * d for d in (2 * m, 2 * m + 1)]
        wts = [qs * get(k_, s0, rows) * jnp.exp2(bq - get(b_, s0, rows)) for s0 in starts]
        seg = jnp.dot(jnp.concatenate(wts, axis=0).astype(BF16), bd, preferred_element_type=F32)
        term = sum(seg[j * rows:(j + 1) * rows] * get(v_, s0, rows) for j, s0 in enumerate(starts))
        for u in tiles:
            piece = term[:, u * w:(u + 1) * w]
            if m == 0:
                acc_ref[u] = piece
            else:
                acc_ref[u, lo - nch:tile, :] = acc_ref[u, lo - nch:tile, :] + piece

    intra = jnp.concatenate(
        [jnp.concatenate([acc_ref[u, pl.ds(ci, c, stride=nch), :] for u in tiles], axis=1)
         for ci in range(nch)], axis=0)
    y = _head_rmsnorm(intra + inter_ref[...], nw_ref[...], bd) * _silu(hg_ref[...])
    o_ref[...] = y.astype(o_ref.dtype)


def _hgrn(pf, lower_bounds, norm_w, layer, batch, cols):
    n = pf.shape[0]
    s = n // batch
    nch = V7X_SUBLANES
    tile = C_HGRN * nch
    nt = s // tile
    gw = GROUP_WIDTH
    lane_tiles = gw // V7X_LANES
    depth = lower_bounds.shape[0]

    def tok(col):
        return pl.BlockSpec((tile, gw), lambda b, t: (b * nt + t, col))

    return pl.pallas_call(
        functools.partial(_hgrn_body, layer=layer),
        grid=(batch, nt),
        in_specs=[tok(cols["hq"]), tok(cols["hf"]), tok(cols["hi"]), tok(cols["hg"]),
                  pl.BlockSpec((depth, gw), lambda b, t: (0, 0)),
                  pl.BlockSpec((1, gw), lambda b, t: (0, 0))],
        out_specs=pl.BlockSpec((tile, gw), lambda b, t: (b * nt + t, 0)),
        out_shape=jax.ShapeDtypeStruct((n, gw), BF16),
        scratch_shapes=[
            pltpu.VMEM((gw, gw), F32),
            pltpu.VMEM((4, lane_tiles, nch + tile, V7X_LANES), F32),
            pltpu.VMEM((lane_tiles, tile, V7X_LANES), F32),
            pltpu.VMEM((lane_tiles, nch * (C_HGRN + nch), V7X_LANES), F32),
            pltpu.VMEM((tile, gw), F32),
            pltpu.VMEM((2, lane_tiles, tile, V7X_LANES), F32),
            pltpu.VMEM((gw, gw), BF16),
        ],
        compiler_params=_cparams(("parallel", "arbitrary"), VMEM_MIXER),
        name="hgrn2",
    )(pf, pf, pf, pf, lower_bounds, norm_w.reshape(1, gw))


def _post_body(h_ref, y0_ref, y1_ref, y2_ref, y3_ref, p_ref, wo_ref, fnw_ref, wup_ref, wdn_ref,
               pnw_ref, wg_ref, wp_ref, fw_ref, o_ref, act_ref, *, final):
    gw = GROUP_WIDTH
    h = h_ref[...]
    for m, y_ref in enumerate((y0_ref, y1_ref, y2_ref, y3_ref)):
        h = h + jnp.dot(y_ref[...], wo_ref[m * gw:(m + 1) * gw, :], preferred_element_type=F32)
    h = _ffn_math(h, fnw_ref, wup_ref, wdn_ref, act_ref)
    x = _rmsnorm(h, pnw_ref[...]).astype(BF16)
    gate = jax.nn.sigmoid(jnp.dot(x, wg_ref[...], preferred_element_type=F32))
    emb = jnp.dot(p_ref[...].astype(BF16), wp_ref[...], preferred_element_type=F32)
    h = h + gate * emb
    if final:
        h = _rmsnorm(h, fw_ref[...])
    o_ref[...] = h


def _post(h, ys, p, w_out, ffn_norm, w_up, w_down, ple_norm, w_gate, w_proj, final_w, layer, final):
    n, d = h.shape
    gw = GROUP_WIDTH
    pd = p.shape[-1]
    d_ff = w_down.shape[1]
    tok = pl.BlockSpec((TM_FFN, d), lambda i: (i, 0))
    ytok = pl.BlockSpec((TM_FFN, gw), lambda i: (i, 0))
    return pl.pallas_call(
        functools.partial(_post_body, final=final),
        grid=(n // TM_FFN,),
        in_specs=[tok, ytok, ytok, ytok, ytok,
                  pl.BlockSpec((None, TM_FFN, pd), lambda i: (layer, i, 0)),
                  _resident(w_out.shape[1:], layer), _resident((1, d)),
                  _resident((d, 2 * d_ff), layer), _resident((d_ff, d), layer), _resident((1, d)),
                  _resident((d, d), layer), _resident((pd, d), layer), _resident((1, d))],
        out_specs=tok,
        out_shape=jax.ShapeDtypeStruct((n, d), F32),
        scratch_shapes=[pltpu.VMEM((TM_FFN, d_ff), BF16)],
        compiler_params=_cparams(("parallel",), VMEM_DENSE),
        name="post",
    )(h, *ys, p, w_out, ffn_norm.reshape(1, d), w_up, w_down, ple_norm.reshape(1, d),
      w_gate, w_proj, final_w.reshape(1, d))


def _split_w_in(w_in):
    gw = GROUP_WIDTH
    xbc_w = gw + 2 * SSD_STATE
    names = ["fq", "fk", "fv", "ff", "rq", "rk", "rv", "rg", "sz", "sxbc", "sdt", "hq", "hf", "hi", "hg"]
    sizes = [gw, gw, gw, N_HEADS, gw, gw, gw, gw, gw, xbc_w, N_HEADS, gw, gw, gw, gw]
    offs = dict(zip(names, np.concatenate([[0], np.cumsum(sizes)[:-1]]).tolist()))
    width = dict(zip(names, sizes))
    w_in = w_in.astype(BF16)

    def gather(group):
        return jnp.concatenate([w_in[:, offs[k]:offs[k] + width[k]] for k in group], axis=1)

    b_group = ["fv", "rv"]
    f_group = ["rq", "rk", "rg", "sz", "sxbc", "hq", "hf", "hi", "hg"]
    cols = {}
    for group in (b_group, f_group):
        pos = 0
        for k in group:
            cols[k] = pos // (xbc_w if k == "sxbc" else gw)
            pos += width[k]
    wqkt = gather(["fq", "fk"]).T
    wst = gather(["ff", "sdt"]).T
    return gather(b_group), gather(f_group), wqkt, wst, cols


def kernel(x, p, ffn1_norm, ffn1_w_up, ffn1_w_down, mix_norm, w_in, fox_f_bias, ret_norm, conv_w, conv_b, dt_bias, a_log, ssd_d, ssd_norm, hgrn_lower_bounds, hgrn_norm, w_out, ffn2_norm, ffn2_w_up, ffn2_w_down, ple_norm, ple_w_gate, ple_w_proj, final_norm):
    batch, s, d = x.shape
    depth = p.shape[0]
    n = batch * s
    h = x.reshape(n, d)
    cos_t, sin_t = _rope_tables(s)
    zeros4 = jnp.zeros((N_HEADS,), F32)
    w_up1, w_dn1 = ffn1_w_up.astype(BF16), ffn1_w_down.astype(BF16)
    w_up2, w_dn2 = ffn2_w_up.astype(BF16), ffn2_w_down.astype(BF16)
    w_o, w_g, w_p = w_out.astype(BF16), ple_w_gate.astype(BF16), ple_w_proj.astype(BF16)
    p_all = p.reshape(depth, n, p.shape[-1])
    for i in range(depth):
        h = _ffn(h, ffn1_norm[i], w_up1, w_dn1, i)

        wb, wf, wqkt, wst, cols = _split_w_in(w_in[i])
        pb, pf, qt, kt, st = _inproj(h, mix_norm[i], wb, wf, wqkt, wst)
        bias = jnp.concatenate([fox_f_bias[i], dt_bias[i]]).reshape(2 * N_HEADS, 1)
        alog = jnp.concatenate([zeros4, a_log[i]]).reshape(2 * N_HEADS, 1)
        g_row, dt_row = _gates(st, bias, alog, batch)
        c_pairs = g_row[:N_HEADS].reshape(N_HEADS // 2, 2, n)
        gcol = jnp.concatenate([dt_row[N_HEADS:], g_row[N_HEADS:]], axis=0).T
        d_x = jnp.repeat(ssd_d[i], HEAD_DIM).reshape(1, GROUP_WIDTH)

        y_fox = _fox(qt, kt, pb, c_pairs, _fox_plan(qt, kt, g_row, batch), batch, cols)
        y_ret = _retention(pf, pb, cos_t, sin_t, ret_norm[i], batch, cols)
        y_ssd = _ssd(pf, gcol, g_row, conv_w[i], conv_b[i], d_x, ssd_norm[i], batch, cols)
        y_hg = _hgrn(pf, hgrn_lower_bounds, hgrn_norm[i], i, batch, cols)
        h = _post(h, (y_fox, y_ret, y_ssd, y_hg), p_all, w_o, ffn2_norm[i], w_up2, w_dn2,
                  ple_norm[i], w_g, w_p, final_norm, layer=i, final=(i == depth - 1))
    return h.reshape(batch, s, d)
```

```python
import functools
import math

import jax
import jax.numpy as jnp
import numpy as np
from jax import lax
from jax.experimental import pallas as pl
from jax.experimental.pallas import tpu as pltpu

F32 = jnp.float32
BF16 = jnp.bfloat16

HEAD_DIM = 64
N_HEADS = 4
GROUP_WIDTH = N_HEADS * HEAD_DIM
SSD_STATE = 128
CONV_WIDTH = 4
RMS_EPS = 1e-6
ROPE_BASE = 10000.0
LOG2E = 1.4426950408889634

V7X_LANES = 128
V7X_SUBLANES = 8
V7X_VMEM_BYTES = 64 * 1024 * 1024
MIB = 1024 * 1024

TM_FFN = 1024
FF_CHUNK = 256
TQ_FOX = 512
C_RET = 128
C_SSD = 128
C_HGRN = 32
T_MIX = 1024
TL_GATES = 2048


VMEM_DENSE = 56
VMEM_FOX = 48
VMEM_MIXER = 24
VMEM_SMALL = 16


def _cparams(semantics, vmem_mib):
    assert vmem_mib * MIB < V7X_VMEM_BYTES
    return pltpu.CompilerParams(dimension_semantics=semantics, vmem_limit_bytes=vmem_mib * MIB)


def _rmsnorm(x, w):
    return x * lax.rsqrt(jnp.mean(x * x, axis=-1, keepdims=True) + RMS_EPS) * w


def _silu(x):
    return x * jax.nn.sigmoid(x)


def _resident(shape, layer=None):
    if layer is None:
        return pl.BlockSpec(shape, lambda *_: (0,) * len(shape), pipeline_mode=pl.Buffered(1))
    return pl.BlockSpec((None,) + tuple(shape), lambda *_: (layer,) + (0,) * len(shape),
                        pipeline_mode=pl.Buffered(1))


def _ffn_math(h, nw_ref, wup_ref, wdn_ref, act_ref):
    d_ff = wdn_ref.shape[0]
    x = _rmsnorm(h, nw_ref[...]).astype(BF16)
    for c in range(d_ff // FF_CHUNK):
        lo = c * FF_CHUNK
        gate = jnp.dot(x, wup_ref[:, lo:lo + FF_CHUNK], preferred_element_type=F32)
        up = jnp.dot(x, wup_ref[:, d_ff + lo:d_ff + lo + FF_CHUNK], preferred_element_type=F32)
        act_ref[:, lo:lo + FF_CHUNK] = (_silu(gate) * up).astype(BF16)
    return h + 0.5 * jnp.dot(act_ref[...], wdn_ref[...], preferred_element_type=F32)


def _ffn_body(h_ref, nw_ref, wup_ref, wdn_ref, o_ref, act_ref):
    o_ref[...] = _ffn_math(h_ref[...], nw_ref, wup_ref, wdn_ref, act_ref)


def _ffn(h, norm_w, w_up, w_down, layer):
    n, d = h.shape
    d_ff = w_down.shape[1]
    return pl.pallas_call(
        _ffn_body,
        grid=(n // TM_FFN,),
        in_specs=[
            pl.BlockSpec((TM_FFN, d), lambda i: (i, 0)),
            _resident((1, d)),
            _resident((d, 2 * d_ff), layer),
            _resident((d_ff, d), layer),
        ],
        out_specs=pl.BlockSpec((TM_FFN, d), lambda i: (i, 0)),
        out_shape=jax.ShapeDtypeStruct((n, d), F32),
        scratch_shapes=[pltpu.VMEM((TM_FFN, d_ff), BF16)],
        compiler_params=_cparams(("parallel",), VMEM_DENSE),
        name="ffn",
    )(h, norm_w.reshape(1, d), w_up, w_down)


def _inproj_body(h_ref, nw_ref, wb_ref, wf_ref, wqkt_ref, wst_ref,
                 pb_ref, pf_ref, qt_ref, kt_ref, st_ref, *, q_scale):
    x = _rmsnorm(h_ref[...], nw_ref[...]).astype(BF16)
    gw = GROUP_WIDTH
    for c in range(wb_ref.shape[1] // gw):
        pb_ref[:, c * gw:(c + 1) * gw] = jnp.dot(x, wb_ref[:, c * gw:(c + 1) * gw],
                                                preferred_element_type=F32).astype(BF16)
    for c in range(wf_ref.shape[1] // gw):
        pf_ref[:, c * gw:(c + 1) * gw] = jnp.dot(x, wf_ref[:, c * gw:(c + 1) * gw],
                                                preferred_element_type=F32)
    nt = (((1,), (1,)), ((), ()))
    qk = lax.dot_general(wqkt_ref[...], x, nt, preferred_element_type=F32)
    qt_ref[...] = (qk[0:gw] * q_scale).astype(BF16)
    kt_ref[...] = qk[gw:2 * gw].astype(BF16)
    st_ref[...] = lax.dot_general(wst_ref[...], x, nt, preferred_element_type=F32)


def _inproj(h, norm_w, wb, wf, wqkt, wst):
    n, d = h.shape
    nb, nf = wb.shape[1], wf.shape[1]
    gw = GROUP_WIDTH
    lanes_out = pl.BlockSpec((gw, TM_FFN), lambda i: (0, i))
    return pl.pallas_call(
        functools.partial(_inproj_body, q_scale=HEAD_DIM ** -0.5 * LOG2E),
        grid=(n // TM_FFN,),
        in_specs=[
            pl.BlockSpec((TM_FFN, d), lambda i: (i, 0)),
            _resident((1, d)),
            _resident((d, nb)),
            _resident((d, nf)),
            _resident((2 * gw, d)),
            _resident((2 * N_HEADS, d)),
        ],
        out_specs=[
            pl.BlockSpec((TM_FFN, nb), lambda i: (i, 0)),
            pl.BlockSpec((TM_FFN, nf), lambda i: (i, 0)),
            lanes_out, lanes_out,
            pl.BlockSpec((2 * N_HEADS, TM_FFN), lambda i: (0, i)),
        ],
        out_shape=[
            jax.ShapeDtypeStruct((n, nb), BF16),
            jax.ShapeDtypeStruct((n, nf), F32),
            jax.ShapeDtypeStruct((gw, n), BF16),
            jax.ShapeDtypeStruct((gw, n), BF16),
            jax.ShapeDtypeStruct((2 * N_HEADS, n), F32),
        ],
        compiler_params=_cparams(("parallel",), VMEM_DENSE),
        name="inproj",
    )(h, norm_w.reshape(1, d), wb, wf, wqkt, wst)


def _gates_body(st_ref, bias_ref, alog_ref, g_ref, dt_ref, carry_ref):
    @pl.when(pl.program_id(1) == 0)
    def _():
        carry_ref[...] = jnp.zeros_like(carry_ref)

    x = st_ref[...] + bias_ref[...]
    tl = x.shape[1]
    row = lax.broadcasted_iota(jnp.int32, x.shape, 0)
    lane = lax.broadcasted_iota(jnp.int32, x.shape, 1)
    lane_in_chunk = lane % C_SSD
    is_fox = row < N_HEADS
    dt = jax.nn.softplus(x)
    val = jnp.where(is_fox, jax.nn.log_sigmoid(x), dt * (-jnp.exp(alog_ref[...])))
    full, local = val, val
    shift = 1
    while shift < tl:
        full = full + jnp.where(lane >= shift, pltpu.roll(full, shift, 1), 0.0)
        if shift < C_SSD:
            local = local + jnp.where(lane_in_chunk >= shift, pltpu.roll(local, shift, 1), 0.0)
        shift *= 2
    cum = full + carry_ref[...]
    carry_ref[...] = jnp.sum(jnp.where(lane == tl - 1, cum, 0.0), axis=1, keepdims=True)
    g_ref[...] = jnp.where(is_fox, cum * LOG2E, local)
    dt_ref[...] = dt


def _gates(st, bias, alog, batch):
    rows, n = st.shape
    s = n // batch
    nt = s // TL_GATES
    spec = pl.BlockSpec((rows, TL_GATES), lambda b, t: (0, b * nt + t))
    col = pl.BlockSpec((rows, 1), lambda b, t: (0, 0))
    return pl.pallas_call(
        _gates_body,
        grid=(batch, nt),
        in_specs=[spec, col, col],
        out_specs=[spec] * 2,
        out_shape=[jax.ShapeDtypeStruct((rows, n), F32)] * 2,
        scratch_shapes=[pltpu.VMEM((rows, 1), F32)],
        compiler_params=_cparams(("parallel", "arbitrary"), VMEM_SMALL),
        name="gates",
    )(st, bias, alog)


def _rope_body(inv_ref, cos_ref, sin_ref, base_ref):
    rows = cos_ref.shape[0]
    inv = inv_ref[...]

    @pl.when(pl.program_id(0) == 0)
    def _():
        ang = lax.broadcasted_iota(jnp.int32, cos_ref.shape, 0).astype(F32) * inv
        base_ref[0] = jnp.cos(ang)
        base_ref[1] = jnp.sin(ang)

    off = (pl.program_id(0) * rows).astype(F32) * inv
    cos_off, sin_off = jnp.cos(off), jnp.sin(off)
    cos_t, sin_t = base_ref[0], base_ref[1]
    lane = lax.broadcasted_iota(jnp.int32, cos_ref.shape, 1)
    cos_ref[...] = cos_t * cos_off - sin_t * sin_off
    sin = sin_t * cos_off + cos_t * sin_off
    sin_ref[...] = jnp.where(lane % HEAD_DIM < HEAD_DIM // 2, -sin, sin)


def _rope_tables(s):
    half = HEAD_DIM // 2
    inv_freq = ROPE_BASE ** (-jnp.arange(half, dtype=F32) / half)
    inv = jnp.tile(inv_freq, V7X_LANES // half).reshape(1, V7X_LANES)
    rows = 1024
    return pl.pallas_call(
        _rope_body,
        grid=(s // rows,),
        in_specs=[pl.BlockSpec((1, V7X_LANES), lambda i: (0, 0))],
        out_specs=[pl.BlockSpec((rows, V7X_LANES), lambda i: (i, 0))] * 2,
        out_shape=[jax.ShapeDtypeStruct((s, V7X_LANES), F32)] * 2,
        scratch_shapes=[pltpu.VMEM((2, rows, V7X_LANES), F32)],
        compiler_params=_cparams(("arbitrary",), VMEM_SMALL),
        name="rope_tables",
    )(inv)


FOX_NEG = -1e30


FOX_SKIP_LOG2 = 160.0
FOX_FAST_LOG2 = 80.0
FOX_BOUND_SLACK = 1.01


def _fox_plan_body(qt_ref, kt_ref, c_ref, plan_ref, *, tq):
    gw, s = qt_ref.shape
    nblk = s // tq
    lanes = plan_ref.shape[-1]

    def head_norm2(x_ref):
        x = x_ref[...].astype(F32)
        x = x * x
        return jnp.concatenate([jnp.sum(x[h * HEAD_DIM:(h + 1) * HEAD_DIM], axis=0, keepdims=True)
                                for h in range(N_HEADS)], axis=0)

    qn2 = head_norm2(qt_ref)
    kmax2 = jnp.max(head_norm2(kt_ref), axis=1, keepdims=True)
    c = c_ref[0:N_HEADS, :]
    lane = lax.broadcasted_iota(jnp.int32, (N_HEADS, lanes), 1)
    qmax2 = jnp.zeros((N_HEADS, lanes), F32)
    cmax = jnp.zeros((N_HEADS, lanes), F32)
    cmin = jnp.full((N_HEADS, lanes), -jnp.inf, F32)
    for j in range(nblk):
        blk = slice(j * tq, (j + 1) * tq)
        qmax2 = jnp.where(lane == j, jnp.max(qn2[:, blk], axis=1, keepdims=True), qmax2)
        cmax = jnp.where(lane == j, jnp.max(c[:, blk], axis=1, keepdims=True), cmax)
        cmin = jnp.where(lane == j, jnp.min(c[:, blk], axis=1, keepdims=True), cmin)
    shift = 1
    while shift < nblk:
        cmin = jnp.minimum(cmin, jnp.where(lane >= shift, pltpu.roll(cmin, shift, 1), jnp.inf))
        shift *= 2
    spread = 2.0 * FOX_BOUND_SLACK * jnp.sqrt(qmax2 * kmax2)
    limit = cmax + spread + FOX_SKIP_LOG2
    skip = jnp.zeros((N_HEADS, lanes), F32)
    for i in range(nblk):
        lim_i = jnp.sum(jnp.where(lane == i, limit, 0.0), axis=1, keepdims=True)
        n_i = jnp.sum(jnp.where(cmin > lim_i, 1.0, 0.0), axis=1, keepdims=True)
        skip = jnp.where(lane == i, n_i, skip)
    fast = jnp.where(spread <= FOX_FAST_LOG2, 1.0, 0.0)
    rows = []
    for p in range(N_HEADS // 2):
        rows.append(jnp.minimum(skip[2 * p:2 * p + 1], skip[2 * p + 1:2 * p + 2]))
    for p in range(N_HEADS // 2):
        rows.append(jnp.minimum(fast[2 * p:2 * p + 1], fast[2 * p + 1:2 * p + 2]))
    rows.append(jnp.zeros((V7X_SUBLANES - len(rows), lanes), F32))
    plan_ref[...] = jnp.concatenate(rows, axis=0).astype(jnp.int32)


def _fox_plan(qt, kt, g_row, batch):
    gw, n = qt.shape
    s = n // batch
    lanes = V7X_LANES
    assert s // TQ_FOX <= lanes
    seq = pl.BlockSpec((gw, s), lambda b: (0, b))
    return pl.pallas_call(
        functools.partial(_fox_plan_body, tq=TQ_FOX),
        grid=(batch,),
        in_specs=[seq, seq, pl.BlockSpec((g_row.shape[0], s), lambda b: (0, b))],
        out_specs=pl.BlockSpec((None, V7X_SUBLANES, lanes), lambda b: (b, 0, 0)),
        out_shape=jax.ShapeDtypeStruct((batch, V7X_SUBLANES, lanes), jnp.int32),
        compiler_params=_cparams(("parallel",), VMEM_DENSE),
        name="fox_plan",
    )(qt, kt, g_row)


def _fox_body(skip_ref, fast_ref, qt_ref, kt_ref, v_ref, c_ref, o_ref,
              ref_ref, l_ref, acc_ref, m_ref, ls_ref):
    b, pair, i = pl.program_id(0), pl.program_id(1), pl.program_id(2)
    w, tq = qt_ref.shape
    tk = 2 * tq
    idx = (b * pl.num_programs(1) + pair) * pl.num_programs(2) + i
    skip = skip_ref[idx]
    q = qt_ref[...].astype(F32).T.astype(BF16)
    first = lax.broadcasted_iota(jnp.int32, (tq, w), 1) < HEAD_DIM
    zero = jnp.zeros_like(q)
    qs = jnp.concatenate([jnp.where(first, q, zero), jnp.where(first, zero, q)], axis=0)

    def scores(start, size):
        s = jnp.dot(qs, kt_ref[:, pl.ds(start, size)], preferred_element_type=F32)
        return [s[hh * tq:(hh + 1) * tq] - c_ref[hh:hh + 1, pl.ds(start, size)] for hh in range(2)]

    def weighted_values(start, size, p):
        stacked = jnp.concatenate(p, axis=0).astype(BF16)
        return jnp.dot(stacked, v_ref[pl.ds(start, size), :], preferred_element_type=F32)

    def earlier_blocks(block):
        def body(j, carry):
            block(pl.multiple_of(j * tk, tk), tk)
            return carry

        lax.fori_loop(skip // 2, i // 2, body, 0)

        @pl.when(jnp.logical_and(i % 2 == 1, skip < i))
        def _():
            block(pl.multiple_of((i - 1) * tq, tq), tq)

    diag_start = pl.multiple_of(i * tq, tq)
    qry = lax.broadcasted_iota(jnp.int32, (tq, tq), 0)
    key = lax.broadcasted_iota(jnp.int32, (tq, tq), 1)

    @pl.when(fast_ref[idx] == 1)
    def _():
        p = []
        for hh, s in enumerate(scores(diag_start, tq)):
            ref = jnp.sum(jnp.where(key == qry, s, 0.0), axis=1, keepdims=True)
            ref_ref[hh] = jnp.broadcast_to(ref, (tq, w))
            p.append(jnp.exp2(jnp.where(key <= qry, s - ref, FOX_NEG)))
            l_ref[hh] = sum(p[hh][:, u * w:(u + 1) * w] for u in range(tq // w))
        acc_ref[...] = weighted_values(diag_start, tq, p)

        def block(start, size):
            p = []
            for hh, s in enumerate(scores(start, size)):
                ref = ref_ref[hh]
                tiles = [jnp.exp2(s[:, u * w:(u + 1) * w] - ref) for u in range(size // w)]
                l_ref[hh] = l_ref[hh] + sum(tiles)
                p.append(jnp.concatenate(tiles, axis=1))
            acc_ref[...] = acc_ref[...] + weighted_values(start, size, p)

        earlier_blocks(block)

    @pl.when(fast_ref[idx] != 1)
    def _():
        m_ref[...] = jnp.full(m_ref.shape, FOX_NEG, F32)
        ls_ref[...] = jnp.zeros_like(ls_ref)
        acc_ref[...] = jnp.zeros_like(acc_ref)

        def update(start, size, s_heads):
            p, alpha = [], []
            for hh, s in enumerate(s_heads):
                m_old = m_ref[hh]
                m_new = jnp.maximum(m_old, jnp.max(s, axis=1, keepdims=True))
                alpha.append(jnp.exp2(m_old - m_new))
                p.append(jnp.exp2(s - m_new))
                ls_ref[hh] = alpha[hh] * ls_ref[hh] + jnp.sum(p[hh], axis=1, keepdims=True)
                m_ref[hh] = m_new
            acc_ref[...] = (jnp.concatenate(alpha, axis=0) * acc_ref[...]
                            + weighted_values(start, size, p))

        earlier_blocks(lambda start, size: update(start, size, scores(start, size)))
        update(diag_start, tq, [jnp.where(key <= qry, s, FOX_NEG) for s in scores(diag_start, tq)])
        lane0 = lax.broadcasted_iota(jnp.int32, (tq, w), 1) == 0
        for hh in range(2):
            l_ref[hh] = jnp.where(lane0, ls_ref[hh], 0.0)

    l = [jnp.sum(l_ref[hh], axis=1, keepdims=True) for hh in range(2)]
    acc = acc_ref[...]
    o_ref[...] = jnp.where(first, acc[0:tq] / l[0], acc[tq:2 * tq] / l[1]).astype(o_ref.dtype)


def _fox(qt, kt, pb, c_pairs, plan, batch, cols):
    n = pb.shape[0]
    s = n // batch
    nq = s // TQ_FOX
    w = V7X_LANES
    pairs = GROUP_WIDTH // w
    skip = plan[:, 0:pairs, 0:nq].reshape(-1)
    fast = plan[:, pairs:2 * pairs, 0:nq].reshape(-1)
    fv = cols["fv"] * pairs
    return pl.pallas_call(
        _fox_body,
        grid_spec=pltpu.PrefetchScalarGridSpec(
            num_scalar_prefetch=2,
            grid=(batch, pairs, nq),
            in_specs=[
                pl.BlockSpec((w, TQ_FOX), lambda b, p, i, *_: (p, b * nq + i)),
                pl.BlockSpec((w, s), lambda b, p, i, *_: (p, b)),
                pl.BlockSpec((s, w), lambda b, p, i, *_: (b, fv + p)),
                pl.BlockSpec((None, 2, s), lambda b, p, i, *_: (p, 0, b)),
            ],
            out_specs=pl.BlockSpec((TQ_FOX, w), lambda b, p, i, *_: (b * nq + i, p)),
            scratch_shapes=[
                pltpu.VMEM((2, TQ_FOX, w), F32),
                pltpu.VMEM((2, TQ_FOX, w), F32),
                pltpu.VMEM((2 * TQ_FOX, w), F32),
                pltpu.VMEM((2, TQ_FOX, 1), F32),
                pltpu.VMEM((2, TQ_FOX, 1), F32),
            ],
        ),
        out_shape=jax.ShapeDtypeStruct((n, GROUP_WIDTH), BF16),
        compiler_params=_cparams(("parallel", "parallel", "arbitrary"), VMEM_FOX),
        name="fox",
    )(skip, fast, qt, kt, pb, c_pairs)


def _head_of_lane(shape, axis):
    return lax.broadcasted_iota(jnp.int32, shape, axis) // HEAD_DIM


def _same_head(shape):
    return _head_of_lane(shape, 0) == _head_of_lane(shape, 1)


def _per_head(values, lane_head):
    out = values[N_HEADS - 1]
    for h in range(N_HEADS - 2, -1, -1):
        out = jnp.where(lane_head == h, values[h], out)
    return out


def _head_rmsnorm(o, w, ones_bd):
    ms = jnp.dot((o * o).astype(BF16), ones_bd, preferred_element_type=F32) * (1.0 / HEAD_DIM)
    return o * lax.rsqrt(ms + RMS_EPS) * w


def _ret_body(rq_ref, rk_ref, rv_ref, rg_ref, cos_ref, sin_ref, nw_ref, o_ref,
              state_ref, dmat_ref, tab_ref, bd_ref):
    c = C_RET
    gw = GROUP_WIDTH
    log_gamma = [math.log1p(-(2.0 ** (-5.0 - h))) for h in range(N_HEADS)]

    @pl.when(pl.program_id(1) == 0)
    def _():
        state_ref[...] = jnp.zeros_like(state_ref)
        r = lax.broadcasted_iota(jnp.int32, (c, c), 0)
        s = lax.broadcasted_iota(jnp.int32, (c, c), 1)
        dist = (r - s).astype(F32)
        for h in range(N_HEADS):
            dmat_ref[h] = jnp.where(s <= r, jnp.exp(dist * log_gamma[h]), 0.0)
        lg = _per_head(log_gamma, _head_of_lane((c, gw), 1))
        t = lax.broadcasted_iota(jnp.int32, (c, gw), 0).astype(F32)
        tab_ref[0] = jnp.exp((t + 1.0) * lg)
        tab_ref[1] = jnp.exp((c - 1.0 - t) * lg)
        tab_ref[2] = jnp.exp(c * lg)
        bd_ref[...] = jnp.where(_same_head((gw, gw)), 1.0, 0.0).astype(BF16)

    lane = lax.broadcasted_iota(jnp.int32, (c, gw), 1)
    lane_head = lane // HEAD_DIM
    first_half = lane % HEAD_DIM < HEAD_DIM // 2

    for ci in range(rq_ref.shape[0] // c):
        rows = slice(ci * c, (ci + 1) * c)
        cosx = jnp.concatenate([cos_ref[rows, :]] * (gw // V7X_LANES), axis=1)
        sinx = jnp.concatenate([sin_ref[rows, :]] * (gw // V7X_LANES), axis=1)

        def rotary(x):
            swapped = jnp.where(first_half, pltpu.roll(x, gw - HEAD_DIM // 2, 1),
                                pltpu.roll(x, HEAD_DIM // 2, 1))
            return x * cosx + swapped * sinx

        q = rotary(rq_ref[rows, :])
        k = rotary(rk_ref[rows, :]) * (HEAD_DIM ** -0.5)
        v = rv_ref[rows, :]
        qb, kb = q.astype(BF16), k.astype(BF16)
        zero = jnp.zeros_like(qb)
        o = jnp.zeros((c, gw), F32)
        for h in range(N_HEADS):
            sel = lane_head == h
            sc = lax.dot_general(jnp.where(sel, qb, zero), kb, (((1,), (1,)), ((), ())),
                                 preferred_element_type=F32)
            p = (sc * dmat_ref[h]).astype(BF16)
            o = o + jnp.where(sel, jnp.dot(p, v, preferred_element_type=F32), 0.0)
        st = state_ref[...]
        o = o + lax.dot_general(qb, st.astype(BF16), (((1,), (1,)), ((), ())),
                                preferred_element_type=F32) * tab_ref[0]
        kd = (k * tab_ref[1]).astype(BF16)
        vt = v.astype(F32).T.astype(BF16)
        upd = jnp.dot(vt, kd, preferred_element_type=F32)
        state_ref[...] = st * tab_ref[2, 0:1, :] + jnp.where(_same_head((gw, gw)), upd, 0.0)
        y = _head_rmsnorm(o, nw_ref[...], bd_ref[...]) * _silu(rg_ref[rows, :])
        o_ref[rows, :] = y.astype(o_ref.dtype)


def _retention(pf, pb, cos_t, sin_t, norm_w, batch, cols):
    n = pf.shape[0]
    s = n // batch
    nt = s // T_MIX
    gw = GROUP_WIDTH

    def tok(col):
        return pl.BlockSpec((T_MIX, gw), lambda b, t: (b * nt + t, col))

    tab = pl.BlockSpec((T_MIX, V7X_LANES), lambda b, t: (t, 0))
    return pl.pallas_call(
        _ret_body,
        grid=(batch, nt),
        in_specs=[tok(cols["rq"]), tok(cols["rk"]), tok(cols["rv"]), tok(cols["rg"]), tab, tab,
                  pl.BlockSpec((1, gw), lambda b, t: (0, 0))],
        out_specs=pl.BlockSpec((T_MIX, gw), lambda b, t: (b * nt + t, 0)),
        out_shape=jax.ShapeDtypeStruct((n, gw), BF16),
        scratch_shapes=[
            pltpu.VMEM((gw, gw), F32),
            pltpu.VMEM((N_HEADS, C_RET, C_RET), F32),
            pltpu.VMEM((3, C_RET, gw), F32),
            pltpu.VMEM((gw, gw), BF16),
        ],
        compiler_params=_cparams(("parallel", "arbitrary"), VMEM_MIXER),
        name="retention",
    )(pf, pf, pb, pf, cos_t, sin_t, norm_w.reshape(1, gw))


def _ssd_body(xbc_ref, z_ref, gcol_ref, grow_ref, cw_ref, cb_ref, d_ref, nw_ref, o_ref,
              stage_ref, act_ref, tail_ref, state_ref):
    tile = xbc_ref.shape[0]
    c = C_SSD
    gw = GROUP_WIDTH
    w = V7X_LANES
    nsub = V7X_SUBLANES
    span = tile // nsub
    pitch = stage_ref.shape[1] // nsub
    tiles = range(xbc_ref.shape[1] // w)
    taps = CONV_WIDTH - 1

    @pl.when(pl.program_id(1) == 0)
    def _():
        state_ref[...] = jnp.zeros_like(state_ref)
        tail_ref[...] = jnp.zeros_like(tail_ref)

    sub_id = lax.broadcasted_iota(jnp.int32, (nsub, w), 0)
    for u in tiles:
        for si in range(nsub):
            stage_ref[u, si * pitch:si * pitch + span, :] = xbc_ref[si * span:(si + 1) * span,
                                                                    u * w:(u + 1) * w]
        x = jnp.concatenate([stage_ref[u, pl.ds(t, nsub, stride=pitch), :] for t in range(span)],
                            axis=0)
        halo = []
        for j in range(taps):
            grp = x[(span - taps + j) * nsub:(span - taps + j + 1) * nsub]
            prev = tail_ref[u, j:j + 1, :]
            halo.append(jnp.where(sub_id == 0, prev, pltpu.roll(grp, 1, 0)))
            tail_ref[u, j:j + 1, :] = grp[nsub - 1:nsub]
        ext = jnp.concatenate(halo + [x], axis=0)
        conv = cb_ref[:, u * w:(u + 1) * w]
        for j in range(CONV_WIDTH):
            conv = conv + cw_ref[j:j + 1, u * w:(u + 1) * w] * ext[j * nsub:j * nsub + tile]
        act_ref[u] = _silu(conv)

    def act_rows(ci):
        per_sub = span // c
        si, off = ci // per_sub, (ci % per_sub) * c
        return jnp.concatenate(
            [act_ref[u, pl.ds(off * nsub + si, c, stride=nsub), :] for u in tiles], axis=1)

    lane_head = _head_of_lane((c, gw), 1)
    r = lax.broadcasted_iota(jnp.int32, (c, c), 0)
    s = lax.broadcasted_iota(jnp.int32, (c, c), 1)
    for ci in range(tile // c):
        rows = slice(ci * c, (ci + 1) * c)
        xbc = act_rows(ci)
        xs = xbc[:, 0:gw]
        bm = xbc[:, gw:gw + SSD_STATE]
        cm = xbc[:, gw + SSD_STATE:gw + 2 * SSD_STATE]
        gcol = gcol_ref[rows, :]
        dtx = _per_head([gcol[:, h:h + 1] for h in range(N_HEADS)], lane_head)
        bx = _per_head([gcol[:, N_HEADS + h:N_HEADS + h + 1] for h in range(N_HEADS)], lane_head)
        blast = bx[c - 1:c, :]
        v = xs * dtx
        vb = v.astype(BF16)

        sc = lax.dot_general(cm.astype(BF16), bm.astype(BF16), (((1,), (1,)), ((), ())),
                             preferred_element_type=F32)
        o = jnp.zeros((c, gw), F32)
        for h in range(N_HEADS):
            bcol = gcol[:, N_HEADS + h:N_HEADS + h + 1]
            brow = grow_ref[N_HEADS + h:N_HEADS + h + 1, rows]
            decay = jnp.where(s <= r, jnp.exp(jnp.minimum(bcol - brow, 0.0)), 0.0)
            p = (sc * decay).astype(BF16)
            o = o + jnp.where(lane_head == h, jnp.dot(p, vb, preferred_element_type=F32), 0.0)
        st = state_ref[...]
        o = o + jnp.dot(cm.astype(BF16), st.astype(BF16), preferred_element_type=F32) * jnp.exp(bx)
        vd = (v * jnp.exp(blast - bx)).astype(BF16)
        state_ref[...] = (st * jnp.exp(blast)
                          + jnp.dot(bm.T.astype(BF16), vd, preferred_element_type=F32))
        o = o + d_ref[...] * xs
        y = _rmsnorm(o * _silu(z_ref[rows, :]), nw_ref[...])
        o_ref[rows, :] = y.astype(o_ref.dtype)


def _ssd(pf, gcol, grow, conv_w, conv_b, d_x, norm_w, batch, cols):
    n = pf.shape[0]
    s = n // batch
    nt = s // T_MIX
    gw = GROUP_WIDTH
    xbc_w = gw + 2 * SSD_STATE
    const = lambda shape: pl.BlockSpec(shape, lambda b, t: (0, 0))
    return pl.pallas_call(
        _ssd_body,
        grid=(batch, nt),
        in_specs=[
            pl.BlockSpec((T_MIX, xbc_w), lambda b, t: (b * nt + t, cols["sxbc"])),
            pl.BlockSpec((T_MIX, gw), lambda b, t: (b * nt + t, cols["sz"])),
            pl.BlockSpec((T_MIX, 2 * N_HEADS), lambda b, t: (b * nt + t, 0)),
            pl.BlockSpec((2 * N_HEADS, T_MIX), lambda b, t: (0, b * nt + t)),
            const((CONV_WIDTH, xbc_w)), const((1, xbc_w)), const((1, gw)), const((1, gw)),
        ],
        out_specs=pl.BlockSpec((T_MIX, gw), lambda b, t: (b * nt + t, 0)),
        out_shape=jax.ShapeDtypeStruct((n, gw), BF16),
        scratch_shapes=[
            pltpu.VMEM((xbc_w // V7X_LANES, T_MIX + V7X_SUBLANES * V7X_SUBLANES, V7X_LANES), F32),
            pltpu.VMEM((xbc_w // V7X_LANES, T_MIX, V7X_LANES), F32),
            pltpu.VMEM((xbc_w // V7X_LANES, V7X_SUBLANES, V7X_LANES), F32),
            pltpu.VMEM((SSD_STATE, gw), F32),
        ],
        compiler_params=_cparams(("parallel", "arbitrary"), VMEM_MIXER),
        name="ssd",
    )(pf, pf, gcol, grow, conv_w, conv_b.reshape(1, xbc_w), d_x, norm_w.reshape(1, gw))


def _hgrn_body(hq_ref, hf_ref, hi_ref, hg_ref, lbw_ref, nw_ref, o_ref,
               state_ref, arr_ref, acc_ref, stage_ref, inter_ref, dec_ref, bd_ref, *, layer):
    c, nch = C_HGRN, V7X_SUBLANES
    tile = c * nch
    gw = GROUP_WIDTH
    w = V7X_LANES
    tiles = range(gw // w)
    pitch = stage_ref.shape[1] // nch
    k_, b_, v_, q_ = range(4)

    @pl.when(pl.program_id(1) == 0)
    def _():
        state_ref[...] = jnp.zeros_like(state_ref)
        arr_ref[:, :, 0:nch, :] = jnp.zeros((4, gw // w, nch, w), F32)
        bd_ref[...] = jnp.where(_same_head((gw, gw)), 1.0, 0.0).astype(BF16)

    def interleaved(ref):
        for u in tiles:
            for ci in range(nch):
                stage_ref[u, ci * pitch:ci * pitch + c, :] = ref[ci * c:(ci + 1) * c, u * w:(u + 1) * w]
        return jnp.concatenate(
            [jnp.concatenate([stage_ref[u, pl.ds(t, nch, stride=pitch), :] for t in range(c)], axis=0)
             for u in tiles], axis=1)

    def put(slot, value):
        for u in tiles:
            arr_ref[slot, u, nch:nch + tile, :] = value[:, u * w:(u + 1) * w]

    def get(slot, start, rows):
        return jnp.concatenate([arr_ref[slot, u, start:start + rows, :] for u in tiles], axis=1)

    lbw = lbw_ref[...]
    e = jnp.exp(lbw - jnp.max(lbw, axis=0, keepdims=True))
    sm = e / jnp.sum(e, axis=0, keepdims=True)
    cs = sm[0:1, :]
    first = cs
    for d in range(1, layer + 1):
        cs = cs + sm[d:d + 1, :]
    lb = cs - first
    log_f = jnp.logaddexp(jnp.log(jnp.maximum(lb, 0.0)),
                          jnp.log1p(-lb) + jax.nn.log_sigmoid(interleaved(hf_ref)))
    k = 1.0 - jnp.exp(log_f)
    b = log_f
    shift = 1
    while shift < c:
        rows = shift * nch
        b = b + jnp.concatenate([jnp.zeros((rows, gw), F32), b[:tile - rows]], axis=0)
        shift *= 2
    b2 = b * LOG2E
    q = interleaved(hq_ref)
    put(k_, k)
    put(b_, b2)
    put(v_, interleaved(hi_ref))
    put(q_, q)
    bd = bd_ref[...]

    blast = b2[tile - nch:tile]
    qe = q * jnp.exp2(b2)
    kd = k * jnp.exp2(jnp.concatenate([blast] * c, axis=0) - b2)
    for u in tiles:
        dec_ref[0, u] = qe[:, u * w:(u + 1) * w]
        dec_ref[1, u] = kd[:, u * w:(u + 1) * w]
    decay = jnp.exp2(blast)

    def dec_chunk(slot, ci):
        return jnp.concatenate([dec_ref[slot, u, pl.ds(ci, c, stride=nch), :] for u in tiles], axis=1)

    def state_step(ci):
        rows = slice(ci * c, (ci + 1) * c)
        st = state_ref[...]
        inter_ref[rows, :] = lax.dot_general(dec_chunk(0, ci).astype(BF16), st.astype(BF16),
                                             (((1,), (1,)), ((), ())), preferred_element_type=F32)
        upd = jnp.dot(hi_ref[rows, :].T.astype(BF16), dec_chunk(1, ci).astype(BF16),
                      preferred_element_type=F32)
        state_ref[...] = st * decay[ci:ci + 1] + jnp.where(_same_head((gw, gw)), upd, 0.0)

    steps_apart = (c // 2) // nch
    for m in range(c // 2):
        if m % steps_apart == 0:
            state_step(m // steps_apart)
        lo = nch * (1 + 2 * m)
        rows = tile + nch - lo
        qs = get(q_, lo, rows)
        bq = get(b_, lo, rows)
        starts = [lo - nch * d for d in (2 * m, 2 * m + 1)]
        wts = [qs * get(k_, s0, rows) * jnp.exp2(bq - get(b_, s0, rows)) for s0 in starts]
        seg = jnp.dot(jnp.concatenate(wts, axis=0).astype(BF16), bd, preferred_element_type=F32)
        term = sum(seg[j * rows:(j + 1) * rows] * get(v_, s0, rows) for j, s0 in enumerate(starts))
        for u in tiles:
            piece = term[:, u * w:(u + 1) * w]
            if m == 0:
                acc_ref[u] = piece
            else:
                acc_ref[u, lo - nch:tile, :] = acc_ref[u, lo - nch:tile, :] + piece

    intra = jnp.concatenate(
        [jnp.concatenate([acc_ref[u, pl.ds(ci, c, stride=nch), :] for u in tiles], axis=1)
         for ci in range(nch)], axis=0)
    y = _head_rmsnorm(intra + inter_ref[...], nw_ref[...], bd) * _silu(hg_ref[...])
    o_ref[...] = y.astype(o_ref.dtype)


def _hgrn(pf, lower_bounds, norm_w, layer, batch, cols):
    n = pf.shape[0]
    s = n // batch
    nch = V7X_SUBLANES
    tile = C_HGRN * nch
    nt = s // tile
    gw = GROUP_WIDTH
    lane_tiles = gw // V7X_LANES
    depth = lower_bounds.shape[0]

    def tok(col):
        return pl.BlockSpec((tile, gw), lambda b, t: (b * nt + t, col))

    return pl.pallas_call(
        functools.partial(_hgrn_body, layer=layer),
        grid=(batch, nt),
        in_specs=[tok(cols["hq"]), tok(cols["hf"]), tok(cols["hi"]), tok(cols["hg"]),
                  pl.BlockSpec((depth, gw), lambda b, t: (0, 0)),
                  pl.BlockSpec((1, gw), lambda b, t: (0, 0))],
        out_specs=pl.BlockSpec((tile, gw), lambda b, t: (b * nt + t, 0)),
        out_shape=jax.ShapeDtypeStruct((n, gw), BF16),
        scratch_shapes=[
            pltpu.VMEM((gw, gw), F32),
            pltpu.VMEM((4, lane_tiles, nch + tile, V7X_LANES), F32),
            pltpu.VMEM((lane_tiles, tile, V7X_LANES), F32),
            pltpu.VMEM((lane_tiles, nch * (C_HGRN + nch), V7X_LANES), F32),
            pltpu.VMEM((tile, gw), F32),
            pltpu.VMEM((2, lane_tiles, tile, V7X_LANES), F32),
            pltpu.VMEM((gw, gw), BF16),
        ],
        compiler_params=_cparams(("parallel", "arbitrary"), VMEM_MIXER),
        name="hgrn2",
    )(pf, pf, pf, pf, lower_bounds, norm_w.reshape(1, gw))


def _post_body(h_ref, y0_ref, y1_ref, y2_ref, y3_ref, p_ref, wo_ref, fnw_ref, wup_ref, wdn_ref,
               pnw_ref, wg_ref, wp_ref, fw_ref, o_ref, act_ref, *, final):
    gw = GROUP_WIDTH
    h = h_ref[...]
    for m, y_ref in enumerate((y0_ref, y1_ref, y2_ref, y3_ref)):
        h = h + jnp.dot(y_ref[...], wo_ref[m * gw:(m + 1) * gw, :], preferred_element_type=F32)
    h = _ffn_math(h, fnw_ref, wup_ref, wdn_ref, act_ref)
    x = _rmsnorm(h, pnw_ref[...]).astype(BF16)
    gate = jax.nn.sigmoid(jnp.dot(x, wg_ref[...], preferred_element_type=F32))
    emb = jnp.dot(p_ref[...].astype(BF16), wp_ref[...], preferred_element_type=F32)
    h = h + gate * emb
    if final:
        h = _rmsnorm(h, fw_ref[...])
    o_ref[...] = h


def _post(h, ys, p, w_out, ffn_norm, w_up, w_down, ple_norm, w_gate, w_proj, final_w, layer, final):
    n, d = h.shape
    gw = GROUP_WIDTH
    pd = p.shape[-1]
    d_ff = w_down.shape[1]
    tok = pl.BlockSpec((TM_FFN, d), lambda i: (i, 0))
    ytok = pl.BlockSpec((TM_FFN, gw), lambda i: (i, 0))
    return pl.pallas_call(
        functools.partial(_post_body, final=final),
        grid=(n // TM_FFN,),
        in_specs=[tok, ytok, ytok, ytok, ytok,
                  pl.BlockSpec((None, TM_FFN, pd), lambda i: (layer, i, 0)),
                  _resident(w_out.shape[1:], layer), _resident((1, d)),
                  _resident((d, 2 * d_ff), layer), _resident((d_ff, d), layer), _resident((1, d)),
                  _resident((d, d), layer), _resident((pd, d), layer), _resident((1, d))],
        out_specs=tok,
        out_shape=jax.ShapeDtypeStruct((n, d), F32),
        scratch_shapes=[pltpu.VMEM((TM_FFN, d_ff), BF16)],
        compiler_params=_cparams(("parallel",), VMEM_DENSE),
        name="post",
    )(h, *ys, p, w_out, ffn_norm.reshape(1, d), w_up, w_down, ple_norm.reshape(1, d),
      w_gate, w_proj, final_w.reshape(1, d))


def _split_w_in(w_in):
    gw = GROUP_WIDTH
    xbc_w = gw + 2 * SSD_STATE
    names = ["fq", "fk", "fv", "ff", "rq", "rk", "rv", "rg", "sz", "sxbc", "sdt", "hq", "hf", "hi", "hg"]
    sizes = [gw, gw, gw, N_HEADS, gw, gw, gw, gw, gw, xbc_w, N_HEADS, gw, gw, gw, gw]
    offs = dict(zip(names, np.concatenate([[0], np.cumsum(sizes)[:-1]]).tolist()))
    width = dict(zip(names, sizes))
    w_in = w_in.astype(BF16)

    def gather(group):
        return jnp.concatenate([w_in[:, offs[k]:offs[k] + width[k]] for k in group], axis=1)

    b_group = ["fv", "rv"]
    f_group = ["rq", "rk", "rg", "sz", "sxbc", "hq", "hf", "hi", "hg"]
    cols = {}
    for group in (b_group, f_group):
        pos = 0
        for k in group:
            cols[k] = pos // (xbc_w if k == "sxbc" else gw)
            pos += width[k]
    wqkt = gather(["fq", "fk"]).T
    wst = gather(["ff", "sdt"]).T
    return gather(b_group), gather(f_group), wqkt, wst, cols


def kernel(x, p, ffn1_norm, ffn1_w_up, ffn1_w_down, mix_norm, w_in, fox_f_bias, ret_norm, conv_w, conv_b, dt_bias, a_log, ssd_d, ssd_norm, hgrn_lower_bounds, hgrn_norm, w_out, ffn2_norm, ffn2_w_up, ffn2_w_down, ple_norm, ple_w_gate, ple_w_proj, final_norm):
    batch, s, d = x.shape
    depth = p.shape[0]
    n = batch * s
    h = x.reshape(n, d)
    cos_t, sin_t = _rope_tables(s)
    zeros4 = jnp.zeros((N_HEADS,), F32)
    w_up1, w_dn1 = ffn1_w_up.astype(BF16), ffn1_w_down.astype(BF16)
    w_up2, w_dn2 = ffn2_w_up.astype(BF16), ffn2_w_down.astype(BF16)
    w_o, w_g, w_p = w_out.astype(BF16), ple_w_gate.astype(BF16), ple_w_proj.astype(BF16)
    p_all = p.reshape(depth, n, p.shape[-1])
    for i in range(depth):
        h = _ffn(h, ffn1_norm[i], w_up1, w_dn1, i)

        wb, wf, wqkt, wst, cols = _split_w_in(w_in[i])
        pb, pf, qt, kt, st = _inproj(h, mix_norm[i], wb, wf, wqkt, wst)
        bias = jnp.concatenate([fox_f_bias[i], dt_bias[i]]).reshape(2 * N_HEADS, 1)
        alog = jnp.concatenate([zeros4, a_log[i]]).reshape(2 * N_HEADS, 1)
        g_row, dt_row = _gates(st, bias, alog, batch)
        c_pairs = g_row[:N_HEADS].reshape(N_HEADS // 2, 2, n)
        gcol = jnp.concatenate([dt_row[N_HEADS:], g_row[N_HEADS:]], axis=0).T
        d_x = jnp.repeat(ssd_d[i], HEAD_DIM).reshape(1, GROUP_WIDTH)

        y_fox = _fox(qt, kt, pb, c_pairs, _fox_plan(qt, kt, g_row, batch), batch, cols)
        y_ret = _retention(pf, pb, cos_t, sin_t, ret_norm[i], batch, cols)
        y_ssd = _ssd(pf, gcol, g_row, conv_w[i], conv_b[i], d_x, ssd_norm[i], batch, cols)
        y_hg = _hgrn(pf, hgrn_lower_bounds, hgrn_norm[i], i, batch, cols)
        h = _post(h, (y_fox, y_ret, y_ssd, y_hg), p_all, w_o, ffn2_norm[i], w_up2, w_dn2,
                  ple_norm[i], w_g, w_p, final_norm, layer=i, final=(i == depth - 1))
    return h.reshape(batch, s, d)
```

```python
import functools
import math

import jax
import jax.numpy as jnp
import numpy as np
from jax import lax
from jax.experimental import pallas as pl
from jax.experimental.pallas import tpu as pltpu

F32 = jnp.float32
BF16 = jnp.bfloat16

HEAD_DIM = 64
N_HEADS = 4
GROUP_WIDTH = N_HEADS * HEAD_DIM
SSD_STATE = 128
CONV_WIDTH = 4
RMS_EPS = 1e-6
ROPE_BASE = 10000.0
LOG2E = 1.4426950408889634

V7X_LANES = 128
V7X_SUBLANES = 8
V7X_VMEM_BYTES = 64 * 1024 * 1024
MIB = 1024 * 1024

TM_FFN = 1024
FF_CHUNK = 256
TQ_FOX = 512
C_RET = 128
C_SSD = 128
C_HGRN = 32
T_MIX = 1024
TL_GATES = 2048


VMEM_DENSE = 56
VMEM_FOX = 48
VMEM_MIXER = 24
VMEM_SMALL = 16


def _cparams(semantics, vmem_mib):
    assert vmem_mib * MIB < V7X_VMEM_BYTES
    return pltpu.CompilerParams(dimension_semantics=semantics, vmem_limit_bytes=vmem_mib * MIB)


def _rmsnorm(x, w):
    return x * lax.rsqrt(jnp.mean(x * x, axis=-1, keepdims=True) + RMS_EPS) * w


def _silu(x):
    return x * jax.nn.sigmoid(x)


def _resident(shape, layer=None):
    if layer is None:
        return pl.BlockSpec(shape, lambda *_: (0,) * len(shape), pipeline_mode=pl.Buffered(1))
    return pl.BlockSpec((None,) + tuple(shape), lambda *_: (layer,) + (0,) * len(shape),
                        pipeline_mode=pl.Buffered(1))


def _ffn_math(h, nw_ref, wup_ref, wdn_ref, act_ref):
    d_ff = wdn_ref.shape[0]
    x = _rmsnorm(h, nw_ref[...]).astype(BF16)
    for c in range(d_ff // FF_CHUNK):
        lo = c * FF_CHUNK
        gate = jnp.dot(x, wup_ref[:, lo:lo + FF_CHUNK], preferred_element_type=F32)
        up = jnp.dot(x, wup_ref[:, d_ff + lo:d_ff + lo + FF_CHUNK], preferred_element_type=F32)
        act_ref[:, lo:lo + FF_CHUNK] = (_silu(gate) * up).astype(BF16)
    return h + 0.5 * jnp.dot(act_ref[...], wdn_ref[...], preferred_element_type=F32)


def _ffn_body(h_ref, nw_ref, wup_ref, wdn_ref, o_ref, act_ref):
    o_ref[...] = _ffn_math(h_ref[...], nw_ref, wup_ref, wdn_ref, act_ref)


def _ffn(h, norm_w, w_up, w_down, layer):
    n, d = h.shape
    d_ff = w_down.shape[1]
    return pl.pallas_call(
        _ffn_body,
        grid=(n // TM_FFN,),
        in_specs=[
            pl.BlockSpec((TM_FFN, d), lambda i: (i, 0)),
            _resident((1, d)),
            _resident((d, 2 * d_ff), layer),
            _resident((d_ff, d), layer),
        ],
        out_specs=pl.BlockSpec((TM_FFN, d), lambda i: (i, 0)),
        out_shape=jax.ShapeDtypeStruct((n, d), F32),
        scratch_shapes=[pltpu.VMEM((TM_FFN, d_ff), BF16)],
        compiler_params=_cparams(("parallel",), VMEM_DENSE),
        name="ffn",
    )(h, norm_w.reshape(1, d), w_up, w_down)


def _hgrn_lower_bound(lbw, layer):
    e = jnp.exp(lbw - jnp.max(lbw, axis=0, keepdims=True))
    sm = e / jnp.sum(e, axis=0, keepdims=True)
    cs = sm[0:1, :]
    first = cs
    for d in range(1, layer + 1):
        cs = cs + sm[d:d + 1, :]
    return cs - first


def _inproj_body(h_ref, nw_ref, wb_ref, wf_ref, wqkt_ref, wst_ref, lbw_ref,
                 pb_ref, pf_ref, qt_ref, kt_ref, st_ref, *, q_scale, layer, hf_chunk):
    x = _rmsnorm(h_ref[...], nw_ref[...]).astype(BF16)
    gw = GROUP_WIDTH
    for c in range(wb_ref.shape[1] // gw):
        pb_ref[:, c * gw:(c + 1) * gw] = jnp.dot(x, wb_ref[:, c * gw:(c + 1) * gw],
                                                preferred_element_type=F32).astype(BF16)
    for c in range(wf_ref.shape[1] // gw):
        acc = jnp.dot(x, wf_ref[:, c * gw:(c + 1) * gw], preferred_element_type=F32)
        if c == hf_chunk:
            lb = _hgrn_lower_bound(lbw_ref[...], layer)
            acc = jnp.logaddexp(jnp.log(jnp.maximum(lb, 0.0)),
                                jnp.log1p(-lb) + jax.nn.log_sigmoid(acc))
        pf_ref[:, c * gw:(c + 1) * gw] = acc
    nt = (((1,), (1,)), ((), ()))
    qk = lax.dot_general(wqkt_ref[...], x, nt, preferred_element_type=F32)
    qt_ref[...] = (qk[0:gw] * q_scale).astype(BF16)
    kt_ref[...] = qk[gw:2 * gw].astype(BF16)
    st_ref[...] = lax.dot_general(wst_ref[...], x, nt, preferred_element_type=F32)


def _inproj(h, norm_w, wb, wf, wqkt, wst, lower_bounds, layer, cols):
    n, d = h.shape
    nb, nf = wb.shape[1], wf.shape[1]
    gw = GROUP_WIDTH
    lanes_out = pl.BlockSpec((gw, TM_FFN), lambda i: (0, i))
    return pl.pallas_call(
        functools.partial(_inproj_body, q_scale=HEAD_DIM ** -0.5 * LOG2E, layer=layer,
                          hf_chunk=cols["hf"]),
        grid=(n // TM_FFN,),
        in_specs=[
            pl.BlockSpec((TM_FFN, d), lambda i: (i, 0)),
            _resident((1, d)),
            _resident((d, nb)),
            _resident((d, nf)),
            _resident((2 * gw, d)),
            _resident((2 * N_HEADS, d)),
            _resident(lower_bounds.shape),
        ],
        out_specs=[
            pl.BlockSpec((TM_FFN, nb), lambda i: (i, 0)),
            pl.BlockSpec((TM_FFN, nf), lambda i: (i, 0)),
            lanes_out, lanes_out,
            pl.BlockSpec((2 * N_HEADS, TM_FFN), lambda i: (0, i)),
        ],
        out_shape=[
            jax.ShapeDtypeStruct((n, nb), BF16),
            jax.ShapeDtypeStruct((n, nf), F32),
            jax.ShapeDtypeStruct((gw, n), BF16),
            jax.ShapeDtypeStruct((gw, n), BF16),
            jax.ShapeDtypeStruct((2 * N_HEADS, n), F32),
        ],
        compiler_params=_cparams(("parallel",), VMEM_DENSE),
        name="inproj",
    )(h, norm_w.reshape(1, d), wb, wf, wqkt, wst, lower_bounds)


def _gates_body(st_ref, bias_ref, alog_ref, g_ref, dt_ref, carry_ref):
    @pl.when(pl.program_id(1) == 0)
    def _():
        carry_ref[...] = jnp.zeros_like(carry_ref)

    x = st_ref[...] + bias_ref[...]
    tl = x.shape[1]
    row = lax.broadcasted_iota(jnp.int32, x.shape, 0)
    lane = lax.broadcasted_iota(jnp.int32, x.shape, 1)
    lane_in_chunk = lane % C_SSD
    is_fox = row < N_HEADS
    dt = jax.nn.softplus(x)
    val = jnp.where(is_fox, jax.nn.log_sigmoid(x), dt * (-jnp.exp(alog_ref[...])))
    full, local = val, val
    shift = 1
    while shift < tl:
        full = full + jnp.where(lane >= shift, pltpu.roll(full, shift, 1), 0.0)
        if shift < C_SSD:
            local = local + jnp.where(lane_in_chunk >= shift, pltpu.roll(local, shift, 1), 0.0)
        shift *= 2
    cum = full + carry_ref[...]
    carry_ref[...] = jnp.sum(jnp.where(lane == tl - 1, cum, 0.0), axis=1, keepdims=True)
    g_ref[...] = jnp.where(is_fox, cum * LOG2E, local)
    dt_ref[...] = dt


def _gates(st, bias, alog, batch):
    rows, n = st.shape
    s = n // batch
    nt = s // TL_GATES
    spec = pl.BlockSpec((rows, TL_GATES), lambda b, t: (0, b * nt + t))
    col = pl.BlockSpec((rows, 1), lambda b, t: (0, 0))
    return pl.pallas_call(
        _gates_body,
        grid=(batch, nt),
        in_specs=[spec, col, col],
        out_specs=[spec] * 2,
        out_shape=[jax.ShapeDtypeStruct((rows, n), F32)] * 2,
        scratch_shapes=[pltpu.VMEM((rows, 1), F32)],
        compiler_params=_cparams(("parallel", "arbitrary"), VMEM_SMALL),
        name="gates",
    )(st, bias, alog)


def _rope_body(inv_ref, cos_ref, sin_ref, base_ref):
    rows = cos_ref.shape[0]
    inv = inv_ref[...]

    @pl.when(pl.program_id(0) == 0)
    def _():
        ang = lax.broadcasted_iota(jnp.int32, cos_ref.shape, 0).astype(F32) * inv
        base_ref[0] = jnp.cos(ang)
        base_ref[1] = jnp.sin(ang)

    off = (pl.program_id(0) * rows).astype(F32) * inv
    cos_off, sin_off = jnp.cos(off), jnp.sin(off)
    cos_t, sin_t = base_ref[0], base_ref[1]
    lane = lax.broadcasted_iota(jnp.int32, cos_ref.shape, 1)
    cos_ref[...] = cos_t * cos_off - sin_t * sin_off
    sin = sin_t * cos_off + cos_t * sin_off
    sin_ref[...] = jnp.where(lane % HEAD_DIM < HEAD_DIM // 2, -sin, sin)


def _rope_tables(s):
    half = HEAD_DIM // 2
    inv_freq = ROPE_BASE ** (-jnp.arange(half, dtype=F32) / half)
    inv = jnp.tile(inv_freq, V7X_LANES // half).reshape(1, V7X_LANES)
    rows = 1024
    return pl.pallas_call(
        _rope_body,
        grid=(s // rows,),
        in_specs=[pl.BlockSpec((1, V7X_LANES), lambda i: (0, 0))],
        out_specs=[pl.BlockSpec((rows, V7X_LANES), lambda i: (i, 0))] * 2,
        out_shape=[jax.ShapeDtypeStruct((s, V7X_LANES), F32)] * 2,
        scratch_shapes=[pltpu.VMEM((2, rows, V7X_LANES), F32)],
        compiler_params=_cparams(("arbitrary",), VMEM_SMALL),
        name="rope_tables",
    )(inv)


FOX_NEG = -1e30


FOX_SKIP_LOG2 = 160.0
FOX_FAST_LOG2 = 80.0
FOX_BOUND_SLACK = 1.01


def _fox_plan_body(qt_ref, kt_ref, c_ref, plan_ref, *, tq):
    gw, s = qt_ref.shape
    nblk = s // tq
    lanes = plan_ref.shape[-1]

    def head_norm2(x_ref):
        x = x_ref[...].astype(F32)
        x = x * x
        return jnp.concatenate([jnp.sum(x[h * HEAD_DIM:(h + 1) * HEAD_DIM], axis=0, keepdims=True)
                                for h in range(N_HEADS)], axis=0)

    qn2 = head_norm2(qt_ref)
    kmax2 = jnp.max(head_norm2(kt_ref), axis=1, keepdims=True)
    c = c_ref[0:N_HEADS, :]
    lane = lax.broadcasted_iota(jnp.int32, (N_HEADS, lanes), 1)
    qmax2 = jnp.zeros((N_HEADS, lanes), F32)
    cmax = jnp.zeros((N_HEADS, lanes), F32)
    cmin = jnp.full((N_HEADS, lanes), -jnp.inf, F32)
    for j in range(nblk):
        blk = slice(j * tq, (j + 1) * tq)
        qmax2 = jnp.where(lane == j, jnp.max(qn2[:, blk], axis=1, keepdims=True), qmax2)
        cmax = jnp.where(lane == j, jnp.max(c[:, blk], axis=1, keepdims=True), cmax)
        cmin = jnp.where(lane == j, jnp.min(c[:, blk], axis=1, keepdims=True), cmin)
    shift = 1
    while shift < nblk:
        cmin = jnp.minimum(cmin, jnp.where(lane >= shift, pltpu.roll(cmin, shift, 1), jnp.inf))
        shift *= 2
    spread = 2.0 * FOX_BOUND_SLACK * jnp.sqrt(qmax2 * kmax2)
    limit = cmax + spread + FOX_SKIP_LOG2
    skip = jnp.zeros((N_HEADS, lanes), F32)
    for i in range(nblk):
        lim_i = jnp.sum(jnp.where(lane == i, limit, 0.0), axis=1, keepdims=True)
        n_i = jnp.sum(jnp.where(cmin > lim_i, 1.0, 0.0), axis=1, keepdims=True)
        skip = jnp.where(lane == i, n_i, skip)
    fast = jnp.where(spread <= FOX_FAST_LOG2, 1.0, 0.0)
    rows = []
    for p in range(N_HEADS // 2):
        rows.append(jnp.minimum(skip[2 * p:2 * p + 1], skip[2 * p + 1:2 * p + 2]))
    for p in range(N_HEADS // 2):
        rows.append(jnp.minimum(fast[2 * p:2 * p + 1], fast[2 * p + 1:2 * p + 2]))
    rows.append(jnp.zeros((V7X_SUBLANES - len(rows), lanes), F32))
    plan_ref[...] = jnp.concatenate(rows, axis=0).astype(jnp.int32)


def _fox_plan(qt, kt, g_row, batch):
    gw, n = qt.shape
    s = n // batch
    lanes = V7X_LANES
    assert s // TQ_FOX <= lanes
    seq = pl.BlockSpec((gw, s), lambda b: (0, b))
    return pl.pallas_call(
        functools.partial(_fox_plan_body, tq=TQ_FOX),
        grid=(batch,),
        in_specs=[seq, seq, pl.BlockSpec((g_row.shape[0], s), lambda b: (0, b))],
        out_specs=pl.BlockSpec((None, V7X_SUBLANES, lanes), lambda b: (b, 0, 0)),
        out_shape=jax.ShapeDtypeStruct((batch, V7X_SUBLANES, lanes), jnp.int32),
        compiler_params=_cparams(("parallel",), VMEM_DENSE),
        name="fox_plan",
    )(qt, kt, g_row)


def _fox_body(skip_ref, fast_ref, qt_ref, kt_ref, v_ref, c_ref, o_ref,
              ref_ref, l_ref, acc_ref, m_ref, ls_ref):
    b, pair, i = pl.program_id(0), pl.program_id(1), pl.program_id(2)
    w, tq = qt_ref.shape
    tk = 2 * tq
    idx = (b * pl.num_programs(1) + pair) * pl.num_programs(2) + i
    skip = skip_ref[idx]
    q = qt_ref[...].astype(F32).T.astype(BF16)
    first = lax.broadcasted_iota(jnp.int32, (tq, w), 1) < HEAD_DIM
    zero = jnp.zeros_like(q)
    qs = jnp.concatenate([jnp.where(first, q, zero), jnp.where(first, zero, q)], axis=0)

    def scores(start, size):
        s = jnp.dot(qs, kt_ref[:, pl.ds(start, size)], preferred_element_type=F32)
        return [s[hh * tq:(hh + 1) * tq] - c_ref[hh:hh + 1, pl.ds(start, size)] for hh in range(2)]

    def weighted_values(start, size, p):
        stacked = jnp.concatenate(p, axis=0).astype(BF16)
        return jnp.dot(stacked, v_ref[pl.ds(start, size), :], preferred_element_type=F32)

    def earlier_blocks(block):
        def body(j, carry):
            block(pl.multiple_of(j * tk, tk), tk)
            return carry

        lax.fori_loop(skip // 2, i // 2, body, 0)

        @pl.when(jnp.logical_and(i % 2 == 1, skip < i))
        def _():
            block(pl.multiple_of((i - 1) * tq, tq), tq)

    diag_start = pl.multiple_of(i * tq, tq)
    qry = lax.broadcasted_iota(jnp.int32, (tq, tq), 0)
    key = lax.broadcasted_iota(jnp.int32, (tq, tq), 1)

    @pl.when(fast_ref[idx] == 1)
    def _():
        p = []
        for hh, s in enumerate(scores(diag_start, tq)):
            ref = jnp.sum(jnp.where(key == qry, s, 0.0), axis=1, keepdims=True)
            ref_ref[hh] = jnp.broadcast_to(ref, (tq, w))
            p.append(jnp.exp2(jnp.where(key <= qry, s - ref, FOX_NEG)))
            l_ref[hh] = sum(p[hh][:, u * w:(u + 1) * w] for u in range(tq // w))
        acc_ref[...] = weighted_values(diag_start, tq, p)

        def block(start, size):
            p = []
            for hh, s in enumerate(scores(start, size)):
                ref = ref_ref[hh]
                tiles = [jnp.exp2(s[:, u * w:(u + 1) * w] - ref) for u in range(size // w)]
                l_ref[hh] = l_ref[hh] + sum(tiles)
                p.append(jnp.concatenate(tiles, axis=1))
            acc_ref[...] = acc_ref[...] + weighted_values(start, size, p)

        earlier_blocks(block)

    @pl.when(fast_ref[idx] != 1)
    def _():
        m_ref[...] = jnp.full(m_ref.shape, FOX_NEG, F32)
        ls_ref[...] = jnp.zeros_like(ls_ref)
        acc_ref[...] = jnp.zeros_like(acc_ref)

        def update(start, size, s_heads):
            p, alpha = [], []
            for hh, s in enumerate(s_heads):
                m_old = m_ref[hh]
                m_new = jnp.maximum(m_old, jnp.max(s, axis=1, keepdims=True))
                alpha.append(jnp.exp2(m_old - m_new))
                p.append(jnp.exp2(s - m_new))
                ls_ref[hh] = alpha[hh] * ls_ref[hh] + jnp.sum(p[hh], axis=1, keepdims=True)
                m_ref[hh] = m_new
            acc_ref[...] = (jnp.concatenate(alpha, axis=0) * acc_ref[...]
                            + weighted_values(start, size, p))

        earlier_blocks(lambda start, size: update(start, size, scores(start, size)))
        update(diag_start, tq, [jnp.where(key <= qry, s, FOX_NEG) for s in scores(diag_start, tq)])
        lane0 = lax.broadcasted_iota(jnp.int32, (tq, w), 1) == 0
        for hh in range(2):
            l_ref[hh] = jnp.where(lane0, ls_ref[hh], 0.0)

    l = [jnp.sum(l_ref[hh], axis=1, keepdims=True) for hh in range(2)]
    acc = acc_ref[...]
    o_ref[...] = jnp.where(first, acc[0:tq] / l[0], acc[tq:2 * tq] / l[1]).astype(o_ref.dtype)


def _fox(qt, kt, pb, c_pairs, plan, batch, cols):
    n = pb.shape[0]
    s = n // batch
    nq = s // TQ_FOX
    w = V7X_LANES
    pairs = GROUP_WIDTH // w
    skip = plan[:, 0:pairs, 0:nq].reshape(-1)
    fast = plan[:, pairs:2 * pairs, 0:nq].reshape(-1)
    fv = cols["fv"] * pairs
    return pl.pallas_call(
        _fox_body,
        grid_spec=pltpu.PrefetchScalarGridSpec(
            num_scalar_prefetch=2,
            grid=(batch, pairs, nq),
            in_specs=[
                pl.BlockSpec((w, TQ_FOX), lambda b, p, i, *_: (p, b * nq + i)),
                pl.BlockSpec((w, s), lambda b, p, i, *_: (p, b)),
                pl.BlockSpec((s, w), lambda b, p, i, *_: (b, fv + p)),
                pl.BlockSpec((None, 2, s), lambda b, p, i, *_: (p, 0, b)),
            ],
            out_specs=pl.BlockSpec((TQ_FOX, w), lambda b, p, i, *_: (b * nq + i, p)),
            scratch_shapes=[
                pltpu.VMEM((2, TQ_FOX, w), F32),
                pltpu.VMEM((2, TQ_FOX, w), F32),
                pltpu.VMEM((2 * TQ_FOX, w), F32),
                pltpu.VMEM((2, TQ_FOX, 1), F32),
                pltpu.VMEM((2, TQ_FOX, 1), F32),
            ],
        ),
        out_shape=jax.ShapeDtypeStruct((n, GROUP_WIDTH), BF16),
        compiler_params=_cparams(("parallel", "parallel", "arbitrary"), VMEM_FOX),
        name="fox",
    )(skip, fast, qt, kt, pb, c_pairs)


def _head_of_lane(shape, axis):
    return lax.broadcasted_iota(jnp.int32, shape, axis) // HEAD_DIM


def _same_head(shape):
    return _head_of_lane(shape, 0) == _head_of_lane(shape, 1)


def _per_head(values, lane_head):
    out = values[N_HEADS - 1]
    for h in range(N_HEADS - 2, -1, -1):
        out = jnp.where(lane_head == h, values[h], out)
    return out


def _head_rmsnorm(o, w, ones_bd):
    ms = jnp.dot((o * o).astype(BF16), ones_bd, preferred_element_type=F32) * (1.0 / HEAD_DIM)
    return o * lax.rsqrt(ms + RMS_EPS) * w


def _ret_body(rq_ref, rk_ref, rv_ref, rg_ref, cos_ref, sin_ref, nw_ref, o_ref,
              state_ref, dmat_ref, tab_ref, bd_ref):
    c = C_RET
    gw = GROUP_WIDTH
    log_gamma = [math.log1p(-(2.0 ** (-5.0 - h))) for h in range(N_HEADS)]

    @pl.when(pl.program_id(1) == 0)
    def _():
        state_ref[...] = jnp.zeros_like(state_ref)
        r = lax.broadcasted_iota(jnp.int32, (c, c), 0)
        s = lax.broadcasted_iota(jnp.int32, (c, c), 1)
        dist = (r - s).astype(F32)
        for h in range(N_HEADS):
            dmat_ref[h] = jnp.where(s <= r, jnp.exp(dist * log_gamma[h]), 0.0)
        lg = _per_head(log_gamma, _head_of_lane((c, gw), 1))
        t = lax.broadcasted_iota(jnp.int32, (c, gw), 0).astype(F32)
        tab_ref[0] = jnp.exp((t + 1.0) * lg)
        tab_ref[1] = jnp.exp((c - 1.0 - t) * lg)
        tab_ref[2] = jnp.exp(c * lg)
        bd_ref[...] = jnp.where(_same_head((gw, gw)), 1.0, 0.0).astype(BF16)

    lane = lax.broadcasted_iota(jnp.int32, (c, gw), 1)
    lane_head = lane // HEAD_DIM
    first_half = lane % HEAD_DIM < HEAD_DIM // 2

    for ci in range(rq_ref.shape[0] // c):
        rows = slice(ci * c, (ci + 1) * c)
        cosx = jnp.concatenate([cos_ref[rows, :]] * (gw // V7X_LANES), axis=1)
        sinx = jnp.concatenate([sin_ref[rows, :]] * (gw // V7X_LANES), axis=1)

        def rotary(x):
            swapped = jnp.where(first_half, pltpu.roll(x, gw - HEAD_DIM // 2, 1),
                                pltpu.roll(x, HEAD_DIM // 2, 1))
            return x * cosx + swapped * sinx

        q = rotary(rq_ref[rows, :])
        k = rotary(rk_ref[rows, :]) * (HEAD_DIM ** -0.5)
        v = rv_ref[rows, :]
        qb, kb = q.astype(BF16), k.astype(BF16)
        zero = jnp.zeros_like(qb)
        o = jnp.zeros((c, gw), F32)
        for h in range(N_HEADS):
            sel = lane_head == h
            sc = lax.dot_general(jnp.where(sel, qb, zero), kb, (((1,), (1,)), ((), ())),
                                 preferred_element_type=F32)
            p = (sc * dmat_ref[h]).astype(BF16)
            o = o + jnp.where(sel, jnp.dot(p, v, preferred_element_type=F32), 0.0)
        st = state_ref[...]
        o = o + lax.dot_general(qb, st.astype(BF16), (((1,), (1,)), ((), ())),
                                preferred_element_type=F32) * tab_ref[0]
        kd = (k * tab_ref[1]).astype(BF16)
        vt = v.astype(F32).T.astype(BF16)
        upd = jnp.dot(vt, kd, preferred_element_type=F32)
        state_ref[...] = st * tab_ref[2, 0:1, :] + jnp.where(_same_head((gw, gw)), upd, 0.0)
        y = _head_rmsnorm(o, nw_ref[...], bd_ref[...]) * _silu(rg_ref[rows, :])
        o_ref[rows, :] = y.astype(o_ref.dtype)


def _retention(pf, pb, cos_t, sin_t, norm_w, batch, cols):
    n = pf.shape[0]
    s = n // batch
    nt = s // T_MIX
    gw = GROUP_WIDTH

    def tok(col):
        return pl.BlockSpec((T_MIX, gw), lambda b, t: (b * nt + t, col))

    tab = pl.BlockSpec((T_MIX, V7X_LANES), lambda b, t: (t, 0))
    return pl.pallas_call(
        _ret_body,
        grid=(batch, nt),
        in_specs=[tok(cols["rq"]), tok(cols["rk"]), tok(cols["rv"]), tok(cols["rg"]), tab, tab,
                  pl.BlockSpec((1, gw), lambda b, t: (0, 0))],
        out_specs=pl.BlockSpec((T_MIX, gw), lambda b, t: (b * nt + t, 0)),
        out_shape=jax.ShapeDtypeStruct((n, gw), BF16),
        scratch_shapes=[
            pltpu.VMEM((gw, gw), F32),
            pltpu.VMEM((N_HEADS, C_RET, C_RET), F32),
            pltpu.VMEM((3, C_RET, gw), F32),
            pltpu.VMEM((gw, gw), BF16),
        ],
        compiler_params=_cparams(("parallel", "arbitrary"), VMEM_MIXER),
        name="retention",
    )(pf, pf, pb, pf, cos_t, sin_t, norm_w.reshape(1, gw))


def _ssd_body(xbc_ref, z_ref, gcol_ref, grow_ref, cw_ref, cb_ref, d_ref, nw_ref, o_ref,
              stage_ref, act_ref, tail_ref, state_ref):
    tile = xbc_ref.shape[0]
    c = C_SSD
    gw = GROUP_WIDTH
    w = V7X_LANES
    nsub = V7X_SUBLANES
    span = tile // nsub
    pitch = stage_ref.shape[1] // nsub
    tiles = range(xbc_ref.shape[1] // w)
    taps = CONV_WIDTH - 1

    @pl.when(pl.program_id(1) == 0)
    def _():
        state_ref[...] = jnp.zeros_like(state_ref)
        tail_ref[...] = jnp.zeros_like(tail_ref)

    sub_id = lax.broadcasted_iota(jnp.int32, (nsub, w), 0)
    for u in tiles:
        for si in range(nsub):
            stage_ref[u, si * pitch:si * pitch + span, :] = xbc_ref[si * span:(si + 1) * span,
                                                                    u * w:(u + 1) * w]
        x = jnp.concatenate([stage_ref[u, pl.ds(t, nsub, stride=pitch), :] for t in range(span)],
                            axis=0)
        halo = []
        for j in range(taps):
            grp = x[(span - taps + j) * nsub:(span - taps + j + 1) * nsub]
            prev = tail_ref[u, j:j + 1, :]
            halo.append(jnp.where(sub_id == 0, prev, pltpu.roll(grp, 1, 0)))
            tail_ref[u, j:j + 1, :] = grp[nsub - 1:nsub]
        ext = jnp.concatenate(halo + [x], axis=0)
        conv = cb_ref[:, u * w:(u + 1) * w]
        for j in range(CONV_WIDTH):
            conv = conv + cw_ref[j:j + 1, u * w:(u + 1) * w] * ext[j * nsub:j * nsub + tile]
        act_ref[u] = _silu(conv)

    def act_rows(ci):
        per_sub = span // c
        si, off = ci // per_sub, (ci % per_sub) * c
        return jnp.concatenate(
            [act_ref[u, pl.ds(off * nsub + si, c, stride=nsub), :] for u in tiles], axis=1)

    lane_head = _head_of_lane((c, gw), 1)
    r = lax.broadcasted_iota(jnp.int32, (c, c), 0)
    s = lax.broadcasted_iota(jnp.int32, (c, c), 1)
    for ci in range(tile // c):
        rows = slice(ci * c, (ci + 1) * c)
        xbc = act_rows(ci)
        xs = xbc[:, 0:gw]
        bm = xbc[:, gw:gw + SSD_STATE]
        cm = xbc[:, gw + SSD_STATE:gw + 2 * SSD_STATE]
        gcol = gcol_ref[rows, :]
        dtx = _per_head([gcol[:, h:h + 1] for h in range(N_HEADS)], lane_head)
        bx = _per_head([gcol[:, N_HEADS + h:N_HEADS + h + 1] for h in range(N_HEADS)], lane_head)
        blast = bx[c - 1:c, :]
        v = xs * dtx
        vb = v.astype(BF16)

        sc = lax.dot_general(cm.astype(BF16), bm.astype(BF16), (((1,), (1,)), ((), ())),
                             preferred_element_type=F32)
        o = jnp.zeros((c, gw), F32)
        for h in range(N_HEADS):
            bcol = gcol[:, N_HEADS + h:N_HEADS + h + 1]
            brow = grow_ref[N_HEADS + h:N_HEADS + h + 1, rows]
            decay = jnp.where(s <= r, jnp.exp(jnp.minimum(bcol - brow, 0.0)), 0.0)
            p = (sc * decay).astype(BF16)
            o = o + jnp.where(lane_head == h, jnp.dot(p, vb, preferred_element_type=F32), 0.0)
        st = state_ref[...]
        o = o + jnp.dot(cm.astype(BF16), st.astype(BF16), preferred_element_type=F32) * jnp.exp(bx)
        vd = (v * jnp.exp(blast - bx)).astype(BF16)
        state_ref[...] = (st * jnp.exp(blast)
                          + jnp.dot(bm.T.astype(BF16), vd, preferred_element_type=F32))
        o = o + d_ref[...] * xs
        y = _rmsnorm(o * _silu(z_ref[rows, :]), nw_ref[...])
        o_ref[rows, :] = y.astype(o_ref.dtype)


def _ssd(pf, gcol, grow, conv_w, conv_b, d_x, norm_w, batch, cols):
    n = pf.shape[0]
    s = n // batch
    nt = s // T_MIX
    gw = GROUP_WIDTH
    xbc_w = gw + 2 * SSD_STATE
    const = lambda shape: pl.BlockSpec(shape, lambda b, t: (0, 0))
    return pl.pallas_call(
        _ssd_body,
        grid=(batch, nt),
        in_specs=[
            pl.BlockSpec((T_MIX, xbc_w), lambda b, t: (b * nt + t, cols["sxbc"])),
            pl.BlockSpec((T_MIX, gw), lambda b, t: (b * nt + t, cols["sz"])),
            pl.BlockSpec((T_MIX, 2 * N_HEADS), lambda b, t: (b * nt + t, 0)),
            pl.BlockSpec((2 * N_HEADS, T_MIX), lambda b, t: (0, b * nt + t)),
            const((CONV_WIDTH, xbc_w)), const((1, xbc_w)), const((1, gw)), const((1, gw)),
        ],
        out_specs=pl.BlockSpec((T_MIX, gw), lambda b, t: (b * nt + t, 0)),
        out_shape=jax.ShapeDtypeStruct((n, gw), BF16),
        scratch_shapes=[
            pltpu.VMEM((xbc_w // V7X_LANES, T_MIX + V7X_SUBLANES * V7X_SUBLANES, V7X_LANES), F32),
            pltpu.VMEM((xbc_w // V7X_LANES, T_MIX, V7X_LANES), F32),
            pltpu.VMEM((xbc_w // V7X_LANES, V7X_SUBLANES, V7X_LANES), F32),
            pltpu.VMEM((SSD_STATE, gw), F32),
        ],
        compiler_params=_cparams(("parallel", "arbitrary"), VMEM_MIXER),
        name="ssd",
    )(pf, pf, gcol, grow, conv_w, conv_b.reshape(1, xbc_w), d_x, norm_w.reshape(1, gw))


def _hgrn_body(hq_ref, lf_ref, hi_ref, hg_ref, nw_ref, o_ref,
               state_ref, arr_ref, acc_ref, stage_ref, inter_ref, dec_ref, bd_ref):
    c, nch = C_HGRN, V7X_SUBLANES
    tile = c * nch
    gw = GROUP_WIDTH
    w = V7X_LANES
    tiles = range(gw // w)
    pitch = stage_ref.shape[1] // nch
    k_, b_, v_, q_ = range(4)

    @pl.when(pl.program_id(1) == 0)
    def _():
        state_ref[...] = jnp.zeros_like(state_ref)
        arr_ref[:, :, 0:nch, :] = jnp.zeros((4, gw // w, nch, w), F32)
        bd_ref[...] = jnp.where(_same_head((gw, gw)), 1.0, 0.0).astype(BF16)

    def interleaved(ref):
        for u in tiles:
            for ci in range(nch):
                stage_ref[u, ci * pitch:ci * pitch + c, :] = ref[ci * c:(ci + 1) * c, u * w:(u + 1) * w]
        return jnp.concatenate(
            [jnp.concatenate([stage_ref[u, pl.ds(t, nch, stride=pitch), :] for t in range(c)], axis=0)
             for u in tiles], axis=1)

    def put(slot, value):
        for u in tiles:
            arr_ref[slot, u, nch:nch + tile, :] = value[:, u * w:(u + 1) * w]

    def get(slot, start, rows):
        return jnp.concatenate([arr_ref[slot, u, start:start + rows, :] for u in tiles], axis=1)

    log_f = interleaved(lf_ref)
    k = 1.0 - jnp.exp(log_f)
    b = log_f
    shift = 1
    while shift < c:
        rows = shift * nch
        b = b + jnp.concatenate([jnp.zeros((rows, gw), F32), b[:tile - rows]], axis=0)
        shift *= 2
    b2 = b * LOG2E
    q = interleaved(hq_ref)
    put(k_, k)
    put(b_, b2)
    put(v_, interleaved(hi_ref))
    put(q_, q)
    bd = bd_ref[...]

    blast = b2[tile - nch:tile]
    qe = q * jnp.exp2(b2)
    kd = k * jnp.exp2(jnp.concatenate([blast] * c, axis=0) - b2)
    for u in tiles:
        dec_ref[0, u] = qe[:, u * w:(u + 1) * w]
        dec_ref[1, u] = kd[:, u * w:(u + 1) * w]
    decay = jnp.exp2(blast)

    def dec_chunk(slot, ci):
        return jnp.concatenate([dec_ref[slot, u, pl.ds(ci, c, stride=nch), :] for u in tiles], axis=1)

    def state_step(ci):
        rows = slice(ci * c, (ci + 1) * c)
        st = state_ref[...]
        inter_ref[rows, :] = lax.dot_general(dec_chunk(0, ci).astype(BF16), st.astype(BF16),
                                             (((1,), (1,)), ((), ())), preferred_element_type=F32)
        upd = jnp.dot(hi_ref[rows, :].T.astype(BF16), dec_chunk(1, ci).astype(BF16),
                      preferred_element_type=F32)
        state_ref[...] = st * decay[ci:ci + 1] + jnp.where(_same_head((gw, gw)), upd, 0.0)

    steps_apart = (c // 2) // nch
    for m in range(c // 2):
        if m % steps_apart == 0:
            state_step(m // steps_apart)
        lo = nch * (1 + 2 * m)
        rows = tile + nch - lo
        qs = get(q_, lo, rows)
        bq = get(b_, lo, rows)
        starts = [lo - nch * d for d in (2 * m, 2 * m + 1)]
        wts = [qs * get(k_, s0, rows) * jnp.exp2(bq - get(b_, s0, rows)) for s0 in starts]
        seg = jnp.dot(jnp.concatenate(wts, axis=0).astype(BF16), bd, preferred_element_type=F32)
        term = sum(seg[j * rows:(j + 1) * rows] * get(v_, s0, rows) for j, s0 in enumerate(starts))
        for u in tiles:
            piece = term[:, u * w:(u + 1) * w]
            if m == 0:
                acc_ref[u] = piece
            else:
                acc_ref[u, lo - nch:tile, :] = acc_ref[u, lo - nch:tile, :] + piece

    intra = jnp.concatenate(
        [jnp.concatenate([acc_ref[u, pl.ds(ci, c, stride=nch), :] for u in tiles], axis=1)
         for ci in range(nch)], axis=0)
    y = _head_rmsnorm(intra + inter_ref[...], nw_ref[...], bd) * _silu(hg_ref[...])
    o_ref[...] = y.astype(o_ref.dtype)


def _hgrn(pf, norm_w, batch, cols):
    n = pf.shape[0]
    s = n // batch
    nch = V7X_SUBLANES
    tile = C_HGRN * nch
    nt = s // tile
    gw = GROUP_WIDTH
    lane_tiles = gw // V7X_LANES

    def tok(col):
        return pl.BlockSpec((tile, gw), lambda b, t: (b * nt + t, col))

    return pl.pallas_call(
        _hgrn_body,
        grid=(batch, nt),
        in_specs=[tok(cols["hq"]), tok(cols["hf"]), tok(cols["hi"]), tok(cols["hg"]),
                  pl.BlockSpec((1, gw), lambda b, t: (0, 0))],
        out_specs=pl.BlockSpec((tile, gw), lambda b, t: (b * nt + t, 0)),
        out_shape=jax.ShapeDtypeStruct((n, gw), BF16),
        scratch_shapes=[
            pltpu.VMEM((gw, gw), F32),
            pltpu.VMEM((4, lane_tiles, nch + tile, V7X_LANES), F32),
            pltpu.VMEM((lane_tiles, tile, V7X_LANES), F32),
            pltpu.VMEM((lane_tiles, nch * (C_HGRN + nch), V7X_LANES), F32),
            pltpu.VMEM((tile, gw), F32),
            pltpu.VMEM((2, lane_tiles, tile, V7X_LANES), F32),
            pltpu.VMEM((gw, gw), BF16),
        ],
        compiler_params=_cparams(("parallel", "arbitrary"), VMEM_MIXER),
        name="hgrn2",
    )(pf, pf, pf, pf, norm_w.reshape(1, gw))


def _post_body(h_ref, y0_ref, y1_ref, y2_ref, y3_ref, p_ref, wo_ref, fnw_ref, wup_ref, wdn_ref,
               pnw_ref, wg_ref, wp_ref, fw_ref, o_ref, act_ref, *, final):
    gw = GROUP_WIDTH
    h = h_ref[...]
    for m, y_ref in enumerate((y0_ref, y1_ref, y2_ref, y3_ref)):
        h = h + jnp.dot(y_ref[...], wo_ref[m * gw:(m + 1) * gw, :], preferred_element_type=F32)
    h = _ffn_math(h, fnw_ref, wup_ref, wdn_ref, act_ref)
    x = _rmsnorm(h, pnw_ref[...]).astype(BF16)
    gate = jax.nn.sigmoid(jnp.dot(x, wg_ref[...], preferred_element_type=F32))
    emb = jnp.dot(p_ref[...].astype(BF16), wp_ref[...], preferred_element_type=F32)
    h = h + gate * emb
    if final:
        h = _rmsnorm(h, fw_ref[...])
    o_ref[...] = h


def _post(h, ys, p, w_out, ffn_norm, w_up, w_down, ple_norm, w_gate, w_proj, final_w, layer, final):
    n, d = h.shape
    gw = GROUP_WIDTH
    pd = p.shape[-1]
    d_ff = w_down.shape[1]
    tok = pl.BlockSpec((TM_FFN, d), lambda i: (i, 0))
    ytok = pl.BlockSpec((TM_FFN, gw), lambda i: (i, 0))
    return pl.pallas_call(
        functools.partial(_post_body, final=final),
        grid=(n // TM_FFN,),
        in_specs=[tok, ytok, ytok, ytok, ytok,
                  pl.BlockSpec((None, TM_FFN, pd), lambda i: (layer, i, 0)),
                  _resident(w_out.shape[1:], layer), _resident((1, d)),
                  _resident((d, 2 * d_ff), layer), _resident((d_ff, d), layer), _resident((1, d)),
                  _resident((d, d), layer), _resident((pd, d), layer), _resident((1, d))],
        out_specs=tok,
        out_shape=jax.ShapeDtypeStruct((n, d), F32),
        scratch_shapes=[pltpu.VMEM((TM_FFN, d_ff), BF16)],
        compiler_params=_cparams(("parallel",), VMEM_DENSE),
        name="post",
    )(h, *ys, p, w_out, ffn_norm.reshape(1, d), w_up, w_down, ple_norm.reshape(1, d),
      w_gate, w_proj, final_w.reshape(1, d))


def _split_w_in(w_in):
    gw = GROUP_WIDTH
    xbc_w = gw + 2 * SSD_STATE
    names = ["fq", "fk", "fv", "ff", "rq", "rk", "rv", "rg", "sz", "sxbc", "sdt", "hq", "hf", "hi", "hg"]
    sizes = [gw, gw, gw, N_HEADS, gw, gw, gw, gw, gw, xbc_w, N_HEADS, gw, gw, gw, gw]
    offs = dict(zip(names, np.concatenate([[0], np.cumsum(sizes)[:-1]]).tolist()))
    width = dict(zip(names, sizes))
    w_in = w_in.astype(BF16)

    def gather(group):
        return jnp.concatenate([w_in[:, offs[k]:offs[k] + width[k]] for k in group], axis=1)

    b_group = ["fv", "rv"]
    f_group = ["rq", "rk", "rg", "sz", "sxbc", "hq", "hf", "hi", "hg"]
    cols = {}
    for group in (b_group, f_group):
        pos = 0
        for k in group:
            cols[k] = pos // (xbc_w if k == "sxbc" else gw)
            pos += width[k]
    wqkt = gather(["fq", "fk"]).T
    wst = gather(["ff", "sdt"]).T
    return gather(b_group), gather(f_group), wqkt, wst, cols


def kernel(x, p, ffn1_norm, ffn1_w_up, ffn1_w_down, mix_norm, w_in, fox_f_bias, ret_norm, conv_w, conv_b, dt_bias, a_log, ssd_d, ssd_norm, hgrn_lower_bounds, hgrn_norm, w_out, ffn2_norm, ffn2_w_up, ffn2_w_down, ple_norm, ple_w_gate, ple_w_proj, final_norm):
    batch, s, d = x.shape
    depth = p.shape[0]
    n = batch * s
    h = x.reshape(n, d)
    cos_t, sin_t = _rope_tables(s)
    zeros4 = jnp.zeros((N_HEADS,), F32)
    w_up1, w_dn1 = ffn1_w_up.astype(BF16), ffn1_w_down.astype(BF16)
    w_up2, w_dn2 = ffn2_w_up.astype(BF16), ffn2_w_down.astype(BF16)
    w_o, w_g, w_p = w_out.astype(BF16), ple_w_gate.astype(BF16), ple_w_proj.astype(BF16)
    p_all = p.reshape(depth, n, p.shape[-1])
    for i in range(depth):
        h = _ffn(h, ffn1_norm[i], w_up1, w_dn1, i)

        wb, wf, wqkt, wst, cols = _split_w_in(w_in[i])
        pb, pf, qt, kt, st = _inproj(h, mix_norm[i], wb, wf, wqkt, wst, hgrn_lower_bounds, i, cols)
        bias = jnp.concatenate([fox_f_bias[i], dt_bias[i]]).reshape(2 * N_HEADS, 1)
        alog = jnp.concatenate([zeros4, a_log[i]]).reshape(2 * N_HEADS, 1)
        g_row, dt_row = _gates(st, bias, alog, batch)
        c_pairs = g_row[:N_HEADS].reshape(N_HEADS // 2, 2, n)
        gcol = jnp.concatenate([dt_row[N_HEADS:], g_row[N_HEADS:]], axis=0).T
        d_x = jnp.repeat(ssd_d[i], HEAD_DIM).reshape(1, GROUP_WIDTH)

        y_fox = _fox(qt, kt, pb, c_pairs, _fox_plan(qt, kt, g_row, batch), batch, cols)
        y_ret = _retention(pf, pb, cos_t, sin_t, ret_norm[i], batch, cols)
        y_ssd = _ssd(pf, gcol, g_row, conv_w[i], conv_b[i], d_x, ssd_norm[i], batch, cols)
        y_hg = _hgrn(pf, hgrn_norm[i], batch, cols)
        h = _post(h, (y_fox, y_ret, y_ssd, y_hg), p_all, w_o, ffn2_norm[i], w_up2, w_dn2,
                  ple_norm[i], w_g, w_p, final_norm, layer=i, final=(i == depth - 1))
    return h.reshape(batch, s, d)
```
